```python
import math
import jax
import jax.numpy as jnp
from jax import lax
import numpy as np

D_MODEL = 2048
BATCH = 2
SEQ = 4096
DEPTH = 2
DEC_BATCH = 128
DEC_SEQ = 4
PAST_LEN = 2048
PAGE_SIZE = 128

HEAD_DIM = 128
GLA_HEADS = 4
GLA_DK = 128
GLA_DV = 256
GLA_GATE_RANK = 16
GLA_TAU = 16.0
GLA_CHUNK = 16
NSA_HEADS = 8
NSA_KV = 2
NSA_CMP_STRIDE = 16
NSA_CMP_LEN = 2 * NSA_CMP_STRIDE
NSA_CMP_HIDDEN = 256
NSA_SEL_BLOCK = 64
NSA_N_SEL = 16
NSA_WINDOW = 512
MOBA_HEADS = 16
MOBA_KV = 4
MOBA_BLOCK = 256
MOBA_TOPK = 3
D_FF = 5632
FFN_CONV = 3
N_AB = (DEPTH + 1) // 2
N_C = DEPTH // 2
AB_WIDTHS = (GLA_HEADS * GLA_DK, GLA_HEADS * GLA_DK, GLA_HEADS * GLA_DV, GLA_HEADS * GLA_DV, GLA_GATE_RANK,
             NSA_HEADS * HEAD_DIM, NSA_HEADS * 3, 6 * NSA_KV * HEAD_DIM)
AB_MIX = GLA_HEADS * GLA_DV + NSA_HEADS * HEAD_DIM
C_WIDTHS = (MOBA_HEADS * HEAD_DIM, MOBA_KV * HEAD_DIM, MOBA_KV * HEAD_DIM)
C_MIX = MOBA_HEADS * HEAD_DIM
SEL_Q_BLOCK = 128
WIN_Q_BLOCK = 128
MOBA_Q_BLOCK = 16
B_BLOCK = 8
NORM_EPS = 1e-6

kernel_name = 'hybrid_gla_nsa_moba_convffn_step'


def rmsnorm(x, g):
    xf = x.astype(jnp.float32)
    y = xf * lax.rsqrt(jnp.mean(xf * xf, axis=-1, keepdims=True) + NORM_EPS)
    return (y * g.astype(jnp.float32)).astype(x.dtype)


def masked_softmax(s, mask):
    s = jnp.where(mask, s.astype(jnp.float32), -jnp.inf)
    m = jnp.max(s, axis=-1, keepdims=True)
    e = jnp.exp(s - jnp.where(jnp.isfinite(m), m, 0.0))
    d = jnp.sum(e, axis=-1, keepdims=True)
    return e / jnp.where(d > 0, d, 1.0)


def split_cols(z, widths):
    return jnp.split(z, np.cumsum(widths)[:-1].tolist(), axis=-1)


def blocked_over_queries(core, per_q, q_pos, per_b, q_block, b_block):
    B, T = per_q[0].shape[:2]
    qb, bb = math.gcd(T, q_block), math.gcd(B, b_block)
    nq, nb = T // qb, B // bb
    pos_blocks = q_pos.reshape(nq, qb)

    def outer(args):
        pq, pb = args
        pq = tuple(a.reshape(bb, nq, qb, *a.shape[2:]).swapaxes(0, 1) for a in pq)
        out = lax.map(lambda a: core(a[0], a[1], pb), (pq, pos_blocks))
        return out.swapaxes(0, 1).reshape(bb, T, *out.shape[3:])

    pq = tuple(a.reshape(nb, bb, *a.shape[1:]) for a in per_q)
    pb = tuple(a.reshape(nb, bb, *a.shape[1:]) for a in per_b)
    out = lax.map(outer, (pq, pb))
    return out.reshape(B, T, *out.shape[3:])


def gla_recurrence(q, k, v, a_log, s0):
    B, T, H, DK = q.shape
    DV = v.shape[-1]
    C = math.gcd(T, GLA_CHUNK)
    n = T // C
    causal = jnp.tril(jnp.ones((C, C), dtype=bool))

    def to_chunks(a):
        return a.astype(jnp.float32).reshape(B, n, C, *a.shape[2:]).swapaxes(0, 1)

    def step(S, inp):
        qc, kc, vc, ac = inp
        cum = jnp.cumsum(ac, axis=1)
        o_inter = jnp.einsum('bchk,bhkv->bchv', qc * jnp.exp(cum), S)
        diff = cum[:, :, None] - cum[:, None, :]
        decay = jnp.exp(jnp.where(causal[None, :, :, None, None], diff, -jnp.inf))
        att = jnp.einsum('bthk,bshk,btshk->btsh', qc, kc, decay)
        o_intra = jnp.einsum('btsh,bshv->bthv', att, vc)
        last = cum[:, -1]
        S = jnp.exp(last)[..., None] * S + jnp.einsum('bshk,bshv->bhkv', kc * jnp.exp(last[:, None] - cum), vc)
        return S, o_inter + o_intra

    S, o = lax.scan(step, s0.astype(jnp.float32), tuple(to_chunks(a) for a in (q, k, v, a_log)))
    return o.swapaxes(0, 1).reshape(B, T, H, DV), S


def _nsa_sel_core(per_q, q_pos, per_b):
    qg, idx, ok = per_q
    kb, vb = per_b
    b, tq, G, n = idx.shape
    bi = jnp.arange(b)[:, None, None, None]
    gi = jnp.arange(G)[None, None, :, None]
    kg = kb[bi, gi, idx]
    vg = vb[bi, gi, idx]
    k_pos = idx[..., None] * NSA_SEL_BLOCK + jnp.arange(NSA_SEL_BLOCK)
    mask = ok[..., None] & (k_pos <= q_pos[None, :, None, None, None])
    s = jnp.einsum('btgrd,btgnsd->btgrns', qg, kg).reshape(b, tq, G, qg.shape[3], n * NSA_SEL_BLOCK)
    p = masked_softmax(s, mask.reshape(b, tq, G, 1, n * NSA_SEL_BLOCK))
    return jnp.einsum('btgrm,btgmd->btgrd', p.astype(vg.dtype), vg.reshape(b, tq, G, n * NSA_SEL_BLOCK, HEAD_DIM))


def nsa_mixer(q, gates, kv_new, win_new, kv_past, win_buf, past_len, q_gain, k_gain, cw1, cb1, cw2, cb2, cpe):
    B, T = q.shape[:2]
    G, R = NSA_KV, NSA_HEADS // NSA_KV
    q_pos = past_len + jnp.arange(T, dtype=jnp.int32)
    qg = (rmsnorm(q, q_gain) * HEAD_DIM ** -0.5).reshape(B, T, G, R, HEAD_DIM)
    kv_all = jnp.concatenate([kv_past, kv_new], axis=1)
    Ta = kv_all.shape[1]

    n_ch = Ta // NSA_CMP_STRIDE
    n_cmp = n_ch - 1
    ch = kv_all[:, :n_ch * NSA_CMP_STRIDE, 0:2].reshape(B, n_ch, NSA_CMP_STRIDE, 2, G, HEAD_DIM)
    first = jnp.einsum('bcsegd,esdf->bcegf', ch, cw1[:, :NSA_CMP_STRIDE])
    second = jnp.einsum('bcsegd,esdf->bcegf', ch, cw1[:, NSA_CMP_STRIDE:])
    pe_bias = jnp.einsum('esd,esdf->ef', cpe, cw1) + cb1
    hid = jax.nn.gelu(first[:, :-1] + second[:, 1:] + pe_bias[:, None, :])
    ckv = jnp.einsum('bcegf,efd->bcegd', hid, cw2) + cb2[:, None, :]
    kc = rmsnorm(ckv[:, :, 0], k_gain[0])
    vc = ckv[:, :, 1]
    c_end = NSA_CMP_STRIDE * jnp.arange(n_cmp) + NSA_CMP_LEN - 1
    c_mask = c_end[None, :] <= q_pos[:, None]
    p_cmp = masked_softmax(jnp.einsum('btgrd,bcgd->btgrc', qg, kc), c_mask[None, :, None, None, :])
    o_cmp = jnp.einsum('btgrc,bcgd->btgrd', p_cmp.astype(vc.dtype), vc)

    ratio = NSA_SEL_BLOCK // NSA_CMP_STRIDE
    n_sb = -(-Ta // NSA_SEL_BLOCK)
    pg = jnp.pad(p_cmp.sum(axis=3), ((0, 0), (0, 0), (0, 0), (1, ratio * (n_sb + 1) - 1 - n_cmp)))
    pg = pg.reshape(B, T, G, n_sb + 1, ratio)
    imp = pg[..., :-1, :].sum(-1) + pg[..., 1:, 0]
    blk = jnp.arange(n_sb)[None, :]
    cur = (q_pos // NSA_SEL_BLOCK)[:, None]
    valid = blk <= cur
    forced = (blk == 0) | (blk == cur) | (blk == cur - 1)
    score = jnp.where(forced[None, :, None, :], jnp.inf, jnp.where(valid[None, :, None, :], imp, -jnp.inf))
    top, sel_idx = lax.top_k(score, min(NSA_N_SEL, n_sb))
    sel_ok = top > -jnp.inf
    pad = n_sb * NSA_SEL_BLOCK - Ta
    ks = jnp.pad(rmsnorm(kv_all[:, :, 2], k_gain[1]), ((0, 0), (0, pad), (0, 0), (0, 0)))
    vs = jnp.pad(kv_all[:, :, 3], ((0, 0), (0, pad), (0, 0), (0, 0)))
    to_blocks = lambda a: a.reshape(B, n_sb, NSA_SEL_BLOCK, G, HEAD_DIM).transpose(0, 3, 1, 2, 4)
    o_sel = blocked_over_queries(_nsa_sel_core, (qg, sel_idx, sel_ok), q_pos, (to_blocks(ks), to_blocks(vs)),
                                 SEL_Q_BLOCK, B_BLOCK)

    wb = win_buf.shape[1]
    w_all = jnp.concatenate([win_buf, win_new], axis=1)
    w_pad = jnp.pad(w_all, ((0, 0), (NSA_WINDOW - wb, 0), (0, 0), (0, 0), (0, 0)))
    qb = math.gcd(T, WIN_Q_BLOCK)
    nblk = T // qb
    k_idx = jnp.arange(nblk)[:, None] * qb + jnp.arange(NSA_WINDOW + qb)[None, :]
    band = w_pad[:, k_idx]
    kw = rmsnorm(band[:, :, :, 0], k_gain[2])
    vw = band[:, :, :, 1]
    k_pos = past_len - NSA_WINDOW + k_idx
    qp = q_pos.reshape(nblk, qb)
    w_mask = (k_pos[:, None, :] >= 0) & (k_pos[:, None, :] <= qp[:, :, None]) & (qp[:, :, None] - k_pos[:, None, :] < NSA_WINDOW)
    qw = qg.reshape(B, nblk, qb, G, R, HEAD_DIM)
    p_win = masked_softmax(jnp.einsum('bnqgrd,bnkgd->bnqgrk', qw, kw), w_mask[None, :, :, None, None, :])
    o_win = jnp.einsum('bnqgrk,bnkgd->bnqgrd', p_win.astype(vw.dtype), vw).reshape(B, T, G, R, HEAD_DIM)

    g = jax.nn.sigmoid(gates.astype(jnp.float32)).astype(q.dtype).reshape(B, T, G, R, 3)
    o = g[..., 0:1] * o_cmp + g[..., 1:2] * o_sel + g[..., 2:3] * o_win
    new_win = w_all[:, w_all.shape[1] - min(NSA_WINDOW, w_all.shape[1]):]
    return o.reshape(B, T, NSA_HEADS * HEAD_DIM), new_win


def _moba_core(per_q, q_pos, per_b):
    qh, idx, ok = per_q
    kb, vb = per_b
    b, tq, H, n = idx.shape
    bi = jnp.arange(b)[:, None, None, None]
    hi = (jnp.arange(H) // (H // kb.shape[1]))[None, None, :, None]
    kg = kb[bi, hi, idx]
    vg = vb[bi, hi, idx]
    k_pos = idx[..., None] * MOBA_BLOCK + jnp.arange(MOBA_BLOCK)
    mask = ok[..., None] & (k_pos <= q_pos[None, :, None, None, None])
    s = jnp.einsum('bthd,bthnsd->bthns', qh, kg).reshape(b, tq, H, n * MOBA_BLOCK)
    p = masked_softmax(s, mask.reshape(b, tq, H, n * MOBA_BLOCK))
    return jnp.einsum('bthm,bthmd->bthd', p.astype(vg.dtype), vg.reshape(b, tq, H, n * MOBA_BLOCK, HEAD_DIM))


def moba_mixer(q, kv_new, kv_past, past_len, q_gain, k_gain):
    B, T = q.shape[:2]
    R = MOBA_HEADS // MOBA_KV
    q_pos = past_len + jnp.arange(T, dtype=jnp.int32)
    qh = rmsnorm(q, q_gain) * HEAD_DIM ** -0.5
    kv_all = jnp.concatenate([kv_past, kv_new], axis=1)
    Ta = kv_all.shape[1]
    nb = -(-Ta // MOBA_BLOCK)
    pad = ((0, 0), (0, nb * MOBA_BLOCK - Ta), (0, 0), (0, 0))
    to_blocks = lambda a: jnp.pad(a, pad).reshape(B, nb, MOBA_BLOCK, MOBA_KV, HEAD_DIM).transpose(0, 3, 1, 2, 4)
    kb = to_blocks(rmsnorm(kv_all[:, :, 0], k_gain))
    vb = to_blocks(kv_all[:, :, 1])
    k_mean = jnp.mean(kb.astype(jnp.float32), axis=3)
    gs = jnp.einsum('btgrd,bgnd->btgrn', qh.reshape(B, T, MOBA_KV, R, HEAD_DIM).astype(jnp.float32), k_mean)
    gs = gs.reshape(B, T, MOBA_HEADS, nb)
    cur = q_pos // MOBA_BLOCK
    past_ok = jnp.arange(nb)[None, :] < cur[:, None]
    gs = jnp.where(past_ok[None, :, None, :], gs, -jnp.inf)
    top, idx = lax.top_k(gs, min(MOBA_TOPK, nb))
    own = jnp.broadcast_to(cur[None, :, None, None], (B, T, MOBA_HEADS, 1)).astype(idx.dtype)
    blk_idx = jnp.concatenate([idx, own], axis=-1)
    blk_ok = jnp.concatenate([top > -jnp.inf, jnp.ones((B, T, MOBA_HEADS, 1), dtype=bool)], axis=-1)
    o = blocked_over_queries(_moba_core, (qh, blk_idx, blk_ok), q_pos, (kb, vb), MOBA_Q_BLOCK, B_BLOCK)
    return o.reshape(B, T, MOBA_HEADS * HEAD_DIM)


def conv_ffn(x, conv_state, w_up, conv_w, conv_b, w_down):
    T = x.shape[1]
    gate, val = jnp.split(x @ w_up, 2, axis=-1)
    gp = jnp.concatenate([conv_state.astype(gate.dtype), gate], axis=1)
    c = conv_b + sum(conv_w[i] * gp[:, i:i + T] for i in range(FFN_CONV))
    return (jax.nn.silu(c) * val) @ w_down, gp[:, gp.shape[1] - (FFN_CONV - 1):]


def run_trunk(x, past_len, nsa_past, win_bufs, gla_states, moba_past, conv_states, p):
    B, T = x.shape[:2]
    h = x
    new_nsa, new_win, new_gla, new_moba, new_conv = [], [], [], [], []
    for l in range(DEPTH):
        hn = rmsnorm(h, p['norm_mix'][l])
        if l % 2 == 0:
            i = l // 2
            gq, gk, gv, gr, ga, nq, ngt, nkv = split_cols(hn @ p['w_in_ab'][i], AB_WIDTHS)
            a_log = jax.nn.log_sigmoid((ga @ p['gla_a_w2'][i] + p['gla_a_b'][i]).astype(jnp.float32)) / GLA_TAU
            o_gla, s_new = gla_recurrence(gq.reshape(B, T, GLA_HEADS, GLA_DK) * GLA_DK ** -0.5,
                                          gk.reshape(B, T, GLA_HEADS, GLA_DK),
                                          gv.reshape(B, T, GLA_HEADS, GLA_DV),
                                          a_log.reshape(B, T, GLA_HEADS, GLA_DK), gla_states[i])
            o_gla = rmsnorm(o_gla.astype(h.dtype), p['gla_o_norm'][i]) * jax.nn.silu(gr.reshape(B, T, GLA_HEADS, GLA_DV))
            kv6 = nkv.reshape(B, T, 6, NSA_KV, HEAD_DIM)
            o_nsa, w_new = nsa_mixer(nq.reshape(B, T, NSA_HEADS, HEAD_DIM), ngt.reshape(B, T, NSA_HEADS, 3),
                                     kv6[:, :, :4], kv6[:, :, 4:], nsa_past[i], win_bufs[i], past_len,
                                     p['nsa_q_norm'][i], p['nsa_k_norm'][i], p['nsa_cmp_w1'][i], p['nsa_cmp_b1'][i],
                                     p['nsa_cmp_w2'][i], p['nsa_cmp_b2'][i], p['nsa_cmp_pe'][i])
            mix = jnp.concatenate([o_gla.reshape(B, T, GLA_HEADS * GLA_DV), o_nsa], axis=-1) @ p['w_out_ab'][i]
            new_nsa.append(kv6[:, :, :4])
            new_win.append(w_new)
            new_gla.append(s_new.astype(h.dtype))
        else:
            j = l // 2
            mq, mk, mv = split_cols(hn @ p['w_in_c'][j], C_WIDTHS)
            kv_new = jnp.stack([mk.reshape(B, T, MOBA_KV, HEAD_DIM), mv.reshape(B, T, MOBA_KV, HEAD_DIM)], axis=2)
            o = moba_mixer(mq.reshape(B, T, MOBA_HEADS, HEAD_DIM), kv_new, moba_past[j], past_len,
                           p['moba_q_norm'][j], p['moba_k_norm'][j])
            mix = o @ p['w_out_c'][j]
            new_moba.append(kv_new)
        h = h + mix
        f, cs = conv_ffn(rmsnorm(h, p['norm_ffn'][l]), conv_states[l], p['ffn_w_up'][l], p['ffn_conv_w'][l],
                         p['ffn_conv_b'][l], p['ffn_w_down'][l])
        h = h + f
        new_conv.append(cs)
    return (h, jnp.stack(new_nsa), jnp.stack(new_win), jnp.stack(new_gla), jnp.stack(new_moba), jnp.stack(new_conv))


def setup_inputs(seed: int = 0) -> dict:
    key = jax.random.key(seed)
    ks = iter(jax.random.split(key, 40))
    nrm = lambda shape, scale: scale * jax.random.normal(next(ks), shape, jnp.float32)
    gain = lambda shape: 1.0 + 0.05 * jax.random.normal(next(ks), shape, jnp.float32)
    n_pages = PAST_LEN // PAGE_SIZE
    n_phys = (5 * DEC_BATCH * n_pages + 3) // 4
    win_len = min(NSA_WINDOW, PAST_LEN)
    page_table = jax.random.permutation(next(ks), n_phys)[:DEC_BATCH * n_pages].reshape(DEC_BATCH, n_pages).astype(jnp.int32)
    return {
        'x_prompt': nrm((BATCH, SEQ, D_MODEL), 1.0),
        'x_sample': nrm((DEC_BATCH, DEC_SEQ, D_MODEL), 1.0),
        'page_table': page_table,
        'cache_nsa_kv': nrm((N_AB, n_phys, PAGE_SIZE, 4, NSA_KV, HEAD_DIM), 1.0),
        'cache_nsa_win': nrm((N_AB, DEC_BATCH, win_len, 2, NSA_KV, HEAD_DIM), 1.0),
        'state_gla': nrm((N_AB, DEC_BATCH, GLA_HEADS, GLA_DK, GLA_DV), 0.3),
        'cache_moba_kv': nrm((N_C, n_phys, PAGE_SIZE, 2, MOBA_KV, HEAD_DIM), 1.0),
        'state_ffn_conv': nrm((DEPTH, DEC_BATCH, FFN_CONV - 1, D_FF), 1.0),
        'norm_mix': gain((DEPTH, D_MODEL)),
        'w_in_ab': nrm((N_AB, D_MODEL, sum(AB_WIDTHS)), D_MODEL ** -0.5),
        'gla_a_w2': nrm((N_AB, GLA_GATE_RANK, GLA_HEADS * GLA_DK), GLA_GATE_RANK ** -0.5),
        'gla_a_b': nrm((N_AB, GLA_HEADS * GLA_DK), 0.1),
        'gla_o_norm': gain((N_AB, GLA_DV)),
        'nsa_q_norm': gain((N_AB, HEAD_DIM)),
        'nsa_k_norm': gain((N_AB, 3, HEAD_DIM)),
        'nsa_cmp_w1': nrm((N_AB, 2, NSA_CMP_LEN, HEAD_DIM, NSA_CMP_HIDDEN), (NSA_CMP_LEN * HEAD_DIM) ** -0.5),
        'nsa_cmp_b1': nrm((N_AB, 2, NSA_CMP_HIDDEN), 0.02),
        'nsa_cmp_w2': nrm((N_AB, 2, NSA_CMP_HIDDEN, HEAD_DIM), NSA_CMP_HIDDEN ** -0.5),
        'nsa_cmp_b2': nrm((N_AB, 2, HEAD_DIM), 0.02),
        'nsa_cmp_pe': nrm((N_AB, 2, NSA_CMP_LEN, HEAD_DIM), 0.1),
        'w_out_ab': nrm((N_AB, AB_MIX, D_MODEL), AB_MIX ** -0.5),
        'w_in_c': nrm((N_C, D_MODEL, sum(C_WIDTHS)), D_MODEL ** -0.5),
        'moba_q_norm': gain((N_C, HEAD_DIM)),
        'moba_k_norm': gain((N_C, HEAD_DIM)),
        'w_out_c': nrm((N_C, C_MIX, D_MODEL), C_MIX ** -0.5),
        'norm_ffn': gain((DEPTH, D_MODEL)),
        'ffn_w_up': nrm((DEPTH, D_MODEL, 2 * D_FF), D_MODEL ** -0.5),
        'ffn_conv_w': nrm((DEPTH, FFN_CONV, D_FF), FFN_CONV ** -0.5),
        'ffn_conv_b': nrm((DEPTH, D_FF), 0.02),
        'ffn_w_down': nrm((DEPTH, D_FF, D_MODEL), D_FF ** -0.5),
    }


def reference(x_prompt, x_sample, page_table, cache_nsa_kv, cache_nsa_win, state_gla, cache_moba_kv, state_ffn_conv,
              norm_mix, w_in_ab, gla_a_w2, gla_a_b, gla_o_norm, nsa_q_norm, nsa_k_norm, nsa_cmp_w1, nsa_cmp_b1,
              nsa_cmp_w2, nsa_cmp_b2, nsa_cmp_pe, w_out_ab, w_in_c, moba_q_norm, moba_k_norm, w_out_c,
              norm_ffn, ffn_w_up, ffn_conv_w, ffn_conv_b, ffn_w_down):
    p = dict(norm_mix=norm_mix, w_in_ab=w_in_ab, gla_a_w2=gla_a_w2, gla_a_b=gla_a_b, gla_o_norm=gla_o_norm,
             nsa_q_norm=nsa_q_norm, nsa_k_norm=nsa_k_norm, nsa_cmp_w1=nsa_cmp_w1, nsa_cmp_b1=nsa_cmp_b1,
             nsa_cmp_w2=nsa_cmp_w2, nsa_cmp_b2=nsa_cmp_b2, nsa_cmp_pe=nsa_cmp_pe, w_out_ab=w_out_ab,
             w_in_c=w_in_c, moba_q_norm=moba_q_norm, moba_k_norm=moba_k_norm, w_out_c=w_out_c,
             norm_ffn=norm_ffn, ffn_w_up=ffn_w_up, ffn_conv_w=ffn_conv_w, ffn_conv_b=ffn_conv_b, ffn_w_down=ffn_w_down)

    B, dt = x_prompt.shape[0], x_prompt.dtype
    y_prompt, nsa_kv_p, nsa_win_p, gla_p, moba_kv_p, ffn_conv_p = run_trunk(
        x_prompt, 0,
        [jnp.zeros((B, 0, 4, NSA_KV, HEAD_DIM), dt)] * N_AB,
        [jnp.zeros((B, 0, 2, NSA_KV, HEAD_DIM), dt)] * N_AB,
        [jnp.zeros((B, GLA_HEADS, GLA_DK, GLA_DV), dt)] * N_AB,
        [jnp.zeros((B, 0, 2, MOBA_KV, HEAD_DIM), dt)] * N_C,
        [jnp.zeros((B, FFN_CONV - 1, D_FF), dt)] * DEPTH, p)

    n_pages = PAST_LEN // PAGE_SIZE
    past_len = n_pages * PAGE_SIZE
    db = page_table.shape[0]
    from_pages = lambda pool: pool[page_table].reshape(db, past_len, *pool.shape[2:])
    y_sample, nsa_kv_s, nsa_win_s, gla_s, moba_kv_s, ffn_conv_s = run_trunk(
        x_sample, past_len,
        [from_pages(cache_nsa_kv[i]) for i in range(N_AB)],
        [cache_nsa_win[i] for i in range(N_AB)],
        [state_gla[i] for i in range(N_AB)],
        [from_pages(cache_moba_kv[j]) for j in range(N_C)],
        [state_ffn_conv[l] for l in range(DEPTH)], p)

    return (y_prompt, y_sample, nsa_kv_p, nsa_kv_s, nsa_win_p, nsa_win_s, gla_p, gla_s, moba_kv_p, moba_kv_s, ffn_conv_p, ffn_conv_s)
```

```python
import functools
import math

import jax
import jax.numpy as jnp
from jax import lax
from jax.experimental import pallas as pl
from jax.experimental.pallas import tpu as pltpu

F32 = jnp.float32
BF16 = jnp.bfloat16

HEAD_DIM = 128
GLA_HEADS = 4
GLA_DK = 128
GLA_DV = 256
GLA_GATE_RANK = 16
GLA_TAU = 16.0
GLA_CHUNK = 16
NSA_HEADS = 8
NSA_KV = 2
NSA_REP = NSA_HEADS // NSA_KV
NSA_CMP_STRIDE = 16
NSA_CMP_LEN = 2 * NSA_CMP_STRIDE
NSA_CMP_HIDDEN = 256
NSA_SEL_BLOCK = 64
NSA_N_SEL = 16
NSA_WINDOW = 512
MOBA_HEADS = 16
MOBA_KV = 4
MOBA_REP = MOBA_HEADS // MOBA_KV
MOBA_BLOCK = 256
MOBA_TOPK = 3
FFN_CONV = 3
NORM_EPS = 1e-6
PAGE_SIZE = 128

LANES = 128
SUBLANES = 8
NEG = -1e30
SAMPLE_ROWS = 8

AB_GQ, AB_GK, AB_GV, AB_GR, AB_NQ = 0, 512, 1024, 2048, 3072
AB_CMP, AB_SEL, AB_WIN, AB_MISC0, AB_MISC1, AB_TOTAL = 4096, 4608, 5120, 5632, 5760, 6144
MISC_GATE_COL = GLA_GATE_RANK


def _rms(x, g):
    return x * lax.rsqrt(jnp.mean(x * x, axis=-1, keepdims=True) + NORM_EPS) * g


def _dot_nt(a, b):
    return lax.dot_general(a, b, (((1,), (1,)), ((), ())), preferred_element_type=F32)


def _dot_tn(a, b):
    return lax.dot_general(a, b, (((0,), (0,)), ((), ())), preferred_element_type=F32)


def _dot(a, b):
    return jnp.dot(a, b, preferred_element_type=F32)


def _silu(x):
    return x * jax.nn.sigmoid(x)


def _norm_matmul_kernel(x_ref, g_ref, w_ref, o_ref, xn_ref):
    @pl.when(pl.program_id(1) == 0)
    def _():
        xn_ref[...] = _rms(x_ref[...], g_ref[...]).astype(BF16)

    o_ref[...] = _dot(xn_ref[...], w_ref[...])


def norm_matmul(x, g, w, *, tm, tn):
    n, k = x.shape
    nout = w.shape[1]
    return pl.pallas_call(
        _norm_matmul_kernel,
        grid=(n // tm, nout // tn),
        in_specs=[pl.BlockSpec((tm, k), lambda i, j: (i, 0)),
                  pl.BlockSpec((1, k), lambda i, j: (0, 0)),
                  pl.BlockSpec((k, tn), lambda i, j: (0, j))],
        out_specs=pl.BlockSpec((tm, tn), lambda i, j: (i, j)),
        out_shape=jax.ShapeDtypeStruct((n, nout), F32),
        scratch_shapes=[pltpu.VMEM((tm, k), BF16)],
        compiler_params=pltpu.CompilerParams(dimension_semantics=("parallel", "arbitrary")),
        name="norm_matmul",
    )(x, g.reshape(1, k), w)


def _matmul_res_kernel(a_ref, w_ref, r_ref, o_ref):
    d = _dot(a_ref[...], w_ref[...])

    @pl.when(pl.program_id(2) == 0)
    def _():
        o_ref[...] = r_ref[...] + d

    @pl.when(pl.program_id(2) > 0)
    def _():
        o_ref[...] += d


def matmul_residual(a, w, res, *, tm, tn, tk):
    n, k = a.shape
    nout = w.shape[1]
    return pl.pallas_call(
        _matmul_res_kernel,
        grid=(n // tm, nout // tn, k // tk),
        in_specs=[pl.BlockSpec((tm, tk), lambda i, j, kk: (i, kk)),
                  pl.BlockSpec((tk, tn), lambda i, j, kk: (kk, j)),
                  pl.BlockSpec((tm, tn), lambda i, j, kk: (i, j))],
        out_specs=pl.BlockSpec((tm, tn), lambda i, j, kk: (i, j)),
        out_shape=jax.ShapeDtypeStruct((n, nout), F32),
        compiler_params=pltpu.CompilerParams(dimension_semantics=("parallel", "parallel", "arbitrary")),
        name="matmul_residual",
    )(a, w, res)


def _conv_act(g, p1, p2, val, cw_ref, cb_ref):
    c = cb_ref[...] + cw_ref[0:1, :] * p2 + cw_ref[1:2, :] * p1 + cw_ref[2:3, :] * g
    return (_silu(c) * val).astype(BF16)


def _ffn_act_prompt_kernel(g_ref, v_ref, halo_ref, cw_ref, cb_ref, o_ref):
    g = g_ref[...]
    tm = g.shape[0]
    halo = jnp.where(pl.program_id(1) == 0, 0.0, halo_ref[...])
    row = lax.broadcasted_iota(jnp.int32, (tm, 1), 0)
    p1 = jnp.where(row == 0, halo[7:8, :], pltpu.roll(g, 1, 0))
    p2 = jnp.where(row == 0, halo[6:7, :], jnp.where(row == 1, halo[7:8, :], pltpu.roll(g, 2, 0)))
    o_ref[...] = _conv_act(g, p1, p2, v_ref[...], cw_ref, cb_ref)


def ffn_act_prompt(u, conv_w, conv_b, *, batch, seq, tm, tn):
    d_ff = u.shape[1] // 2
    nj = d_ff // tn
    nt = seq // tm
    hb = tm // SUBLANES
    return pl.pallas_call(
        _ffn_act_prompt_kernel,
        grid=(batch, nt, nj),
        in_specs=[pl.BlockSpec((tm, tn), lambda b, i, j: (b * nt + i, j)),
                  pl.BlockSpec((tm, tn), lambda b, i, j: (b * nt + i, nj + j)),
                  pl.BlockSpec((SUBLANES, tn), lambda b, i, j: (jnp.maximum((b * nt + i) * hb - 1, 0), j)),
                  pl.BlockSpec((FFN_CONV, tn), lambda b, i, j: (0, j)),
                  pl.BlockSpec((1, tn), lambda b, i, j: (0, j))],
        out_specs=pl.BlockSpec((tm, tn), lambda b, i, j: (b * nt + i, j)),
        out_shape=jax.ShapeDtypeStruct((batch * seq, d_ff), BF16),
        compiler_params=pltpu.CompilerParams(dimension_semantics=("parallel", "parallel", "parallel")),
        name="ffn_act_prompt",
    )(u, u, u, conv_w, conv_b.reshape(1, d_ff))


def _ffn_act_sample_kernel(g_ref, v_ref, s1_ref, s2_ref, cw_ref, cb_ref, o_ref, *, t_new):
    g = g_ref[...]
    tm = g.shape[0]
    t = lax.broadcasted_iota(jnp.int32, (tm, 1), 0) % t_new
    p1 = jnp.where(t >= 1, pltpu.roll(g, 1, 0), s1_ref[...])
    p2 = jnp.where(t >= 2, pltpu.roll(g, 2, 0), s2_ref[...])
    o_ref[...] = _conv_act(g, p1, p2, v_ref[...], cw_ref, cb_ref)


def ffn_act_sample(u, row0, conv_state, conv_w, conv_b, *, t_new, tn):
    d_ff = u.shape[1] // 2
    nj = d_ff // tn
    db = conv_state.shape[0]
    tm = db * t_new
    rb = row0 // tm
    zero = jnp.zeros((db, t_new - 1, d_ff), F32)
    s1 = jnp.concatenate([conv_state[:, 1:2], zero], axis=1).reshape(tm, d_ff)
    s2 = jnp.concatenate([conv_state, jnp.zeros((db, t_new - 2, d_ff), F32)], axis=1).reshape(tm, d_ff)
    return pl.pallas_call(
        functools.partial(_ffn_act_sample_kernel, t_new=t_new),
        grid=(nj,),
        in_specs=[pl.BlockSpec((tm, tn), lambda j: (rb, j)),
                  pl.BlockSpec((tm, tn), lambda j: (rb, nj + j)),
                  pl.BlockSpec((tm, tn), lambda j: (0, j)),
                  pl.BlockSpec((tm, tn), lambda j: (0, j)),
                  pl.BlockSpec((FFN_CONV, tn), lambda j: (0, j)),
                  pl.BlockSpec((1, tn), lambda j: (0, j))],
        out_specs=pl.BlockSpec((tm, tn), lambda j: (0, j)),
        out_shape=jax.ShapeDtypeStruct((tm, d_ff), BF16),
        compiler_params=pltpu.CompilerParams(dimension_semantics=("parallel",)),
        name="ffn_act_sample",
    )(u, u, s1, s2, conv_w, conv_b.reshape(1, d_ff))


def _gla_kernel(q_ref, k_ref, v_ref, r_ref, m_ref, w2_ref, b2_ref, on_ref, s0_ref, o_ref, sout_ref,
                st_ref, cum_ref, *, chunk, n_valid, nsub):
    i = pl.program_id(1)
    tt = nsub * chunk

    @pl.when(i == 0)
    def _():
        for h in range(GLA_HEADS):
            st_ref[h] = s0_ref[0, h].T

    a = _dot(m_ref[0].astype(BF16), w2_ref[...]) + b2_ref[...]
    a = (jnp.minimum(a, 0.0) - jnp.log(1.0 + jnp.exp(-jnp.abs(a)))) / GLA_TAU
    pos = lax.broadcasted_iota(jnp.int32, (tt, 1), 0) % chunk
    if n_valid < chunk:
        a = jnp.where(pos < n_valid, a, 0.0)
    cum = a
    sh = 1
    while sh < chunk:
        cum = cum + jnp.where(pos >= sh, pltpu.roll(cum, sh, 0), 0.0)
        sh *= 2
    cum_ref[...] = cum

    ti = lax.broadcasted_iota(jnp.int32, (chunk, 1), 0)

    def step(c, carry):
        rows = pl.ds(pl.multiple_of(c * chunk, chunk), chunk)
        for h in range(GLA_HEADS):
            kcols = slice(h * GLA_DK, (h + 1) * GLA_DK)
            vcols = slice(h * GLA_DV, (h + 1) * GLA_DV)
            qh = q_ref[0, rows, kcols] * GLA_DK ** -0.5
            kh = k_ref[0, rows, kcols]
            vh = v_ref[0, rows, vcols]
            ch = cum_ref[rows, kcols]
            st = st_ref[h]
            o = _dot_nt((qh * jnp.exp(ch)).astype(BF16), st.astype(BF16))
            for s in range(chunk):
                d = jnp.exp(jnp.where(ti >= s, ch - ch[s:s + 1, :], NEG))
                w = jnp.sum(qh * kh[s:s + 1, :] * d, axis=-1, keepdims=True)
                o = o + w * vh[s:s + 1, :]
            last = ch[chunk - 1:chunk, :]
            kt = kh * jnp.exp(last - ch)
            st_ref[h] = st * jnp.exp(last) + _dot_tn(vh.astype(BF16), kt.astype(BF16))
            rh = r_ref[0, rows, vcols]
            o_ref[0, rows, vcols] = (_rms(o, on_ref[...]) * _silu(rh)).astype(o_ref.dtype)
        return carry

    lax.fori_loop(0, nsub, step, 0)

    @pl.when(i == pl.num_programs(1) - 1)
    def _():
        for h in range(GLA_HEADS):
            sout_ref[0, h] = st_ref[h].T


def gla_mixer(z3, s0, w2p, b2, onorm, *, n_seq, tiles_per_seq, chunk, n_valid, out_dtype):
    tt = z3.shape[1]
    nsub = tt // chunk
    dqk = GLA_HEADS * GLA_DK
    dv = GLA_HEADS * GLA_DV
    tile = lambda b, i: b * tiles_per_seq + i
    return pl.pallas_call(
        functools.partial(_gla_kernel, chunk=chunk, n_valid=n_valid, nsub=nsub),
        grid=(n_seq, tiles_per_seq),
        in_specs=[pl.BlockSpec((1, tt, dqk), lambda b, i: (tile(b, i), 0, AB_GQ // dqk)),
                  pl.BlockSpec((1, tt, dqk), lambda b, i: (tile(b, i), 0, AB_GK // dqk)),
                  pl.BlockSpec((1, tt, dv), lambda b, i: (tile(b, i), 0, AB_GV // dv)),
                  pl.BlockSpec((1, tt, dv), lambda b, i: (tile(b, i), 0, AB_GR // dv)),
                  pl.BlockSpec((1, tt, LANES), lambda b, i: (tile(b, i), 0, AB_MISC0 // LANES)),
                  pl.BlockSpec((LANES, dqk), lambda b, i: (0, 0)),
                  pl.BlockSpec((1, dqk), lambda b, i: (0, 0)),
                  pl.BlockSpec((1, GLA_DV), lambda b, i: (0, 0)),
                  pl.BlockSpec((1, GLA_HEADS, GLA_DK, GLA_DV), lambda b, i: (b, 0, 0, 0))],
        out_specs=[pl.BlockSpec((1, tt, dv), lambda b, i: (tile(b, i), 0, 0)),
                   pl.BlockSpec((1, GLA_HEADS, GLA_DK, GLA_DV), lambda b, i: (b, 0, 0, 0))],
        out_shape=[jax.ShapeDtypeStruct((n_seq * tiles_per_seq, tt, dv), out_dtype),
                   jax.ShapeDtypeStruct((n_seq, GLA_HEADS, GLA_DK, GLA_DV), F32)],
        scratch_shapes=[pltpu.VMEM((GLA_HEADS, GLA_DV, GLA_DK), F32), pltpu.VMEM((tt, dqk), F32)],
        compiler_params=pltpu.CompilerParams(dimension_semantics=("parallel", "arbitrary")),
        name="gla_mixer",
    )(z3, z3, z3, z3, z3, w2p, b2.reshape(1, dqk), onorm.reshape(1, GLA_DV), s0)


def _masked_softmax_rows(s, mask):
    sm = jnp.where(mask[None], s, NEG)
    m = jnp.max(sm, axis=-1, keepdims=True)
    e = jnp.where(mask[None], jnp.exp(sm - m), 0.0)
    d = jnp.sum(e, axis=-1, keepdims=True)
    return e / jnp.where(d > 0, d, 1.0)


def _dot_exact_rhs(x, m):
    hi = x.astype(BF16)
    r1 = x - hi.astype(F32)
    mid = r1.astype(BF16)
    lo = (r1 - mid.astype(F32)).astype(BF16)
    return _dot(hi, m) + _dot(mid, m) + _dot(lo, m)


def _topk_mask(score, n_rounds):
    lane = lax.broadcasted_iota(jnp.int32, score.shape, 1).astype(F32)
    sel = jnp.zeros(score.shape, F32)
    work = score
    for _ in range(n_rounds):
        mx = jnp.max(work, axis=-1, keepdims=True)
        first = jnp.min(jnp.where(work == mx, lane, 1e9), axis=-1, keepdims=True)
        pick = lane == first
        sel = jnp.where(pick & (mx > 0.5 * NEG), 1.0, sel)
        work = jnp.where(pick, 2.0 * NEG, work)
    return sel


def _block_expand(n_blocks_pad, n_keys, block, key0=0):
    bi = lax.broadcasted_iota(jnp.int32, (n_blocks_pad, n_keys), 0)
    ki = lax.broadcasted_iota(jnp.int32, (n_blocks_pad, n_keys), 1) + key0
    return (bi == jnp.right_shift(ki, int(math.log2(block)))).astype(BF16)


def _gelu_tanh(x):
    return x * (0.5 * (1.0 + jnp.tanh(math.sqrt(2.0 / math.pi) * (x + 0.044715 * (x * x * x)))))


def _cmp_mlp(load_x, n_ch, w1_ref, cpe_ref, cb1_ref, w2_ref, cb2_ref, kg_ref):
    hid_w = NSA_CMP_HIDDEN
    out = {}
    for e in range(2):
        accp = jnp.zeros((SUBLANES, 2 * hid_w), F32)
        for s in range(NSA_CMP_STRIDE):
            accp = accp + _dot(cpe_ref[e, s], w1_ref[e, s])
        pe = accp[0:1, :hid_w] + accp[1:2, hid_w:] + cb1_ref[e]
        for g in range(NSA_KV):
            acc = jnp.zeros((n_ch, 2 * hid_w), F32)
            for s in range(NSA_CMP_STRIDE):
                acc = acc + _dot(load_x(s, e * NSA_KV + g).astype(BF16), w1_ref[e, s])
            hid = _gelu_tanh(acc[:, :hid_w] + pltpu.roll(acc[:, hid_w:], n_ch - 1, 0) + pe)
            out[e, g] = _dot(hid.astype(BF16), w2_ref[e]) + cb2_ref[e]
    kc = [_rms(out[0, g], kg_ref[...]) for g in range(NSA_KV)]
    vc = [out[1, g] for g in range(NSA_KV)]
    return kc, vc


def _nsa_q4(q, qg):
    return jnp.concatenate([_rms(q[:, r * HEAD_DIM:(r + 1) * HEAD_DIM], qg) * HEAD_DIM ** -0.5
                            for r in range(NSA_REP)], axis=0)


def _nsa_cmp_branch(q4, kc, vc, qpos, n_cmp):
    nq = qpos.shape[0]
    n_ch = kc.shape[0]
    sc = _dot_nt(q4, kc.astype(BF16)).reshape(NSA_REP, nq, n_ch)
    cidx = lax.broadcasted_iota(jnp.int32, (1, n_ch), 1)
    cmask = (NSA_CMP_STRIDE * cidx + NSA_CMP_LEN - 1 <= qpos) & (cidx < n_cmp)
    p = _masked_softmax_rows(sc, cmask)
    o = _dot(p.reshape(NSA_REP * nq, n_ch).astype(BF16), vc.astype(BF16))
    return o, jnp.sum(p, axis=0)


def _nsa_select(pcs, qpos, n_sb):
    n_ch = pcs.shape[1]
    ratio = NSA_SEL_BLOCK // NSA_CMP_STRIDE
    ci = lax.broadcasted_iota(jnp.int32, (n_ch, LANES), 0)
    ji = lax.broadcasted_iota(jnp.int32, (n_ch, LANES), 1)
    mimp = ((ci >= ratio * ji - 1) & (ci <= ratio * ji + ratio - 1)).astype(BF16)
    imp = _dot_exact_rhs(pcs, mimp)
    blk = lax.broadcasted_iota(jnp.int32, (1, LANES), 1)
    cur = jnp.right_shift(qpos, int(math.log2(NSA_SEL_BLOCK)))
    valid = (blk <= cur) & (blk < n_sb)
    forced = (blk == 0) | (blk == cur) | (blk == cur - 1)
    score = jnp.where(forced, -NEG, jnp.where(valid, imp, NEG))
    return _topk_mask(score, min(NSA_N_SEL, n_sb))


def _nsa_window_mask(kpos, qpos):
    return (kpos <= qpos) & (qpos - kpos < NSA_WINDOW) & (kpos >= 0)


def _nsa_gate_mix(gsig, o_cmp, o_sel, o_win, nq):
    outs = []
    for r in range(NSA_REP):
        c0 = MISC_GATE_COL + 3 * r
        rows = slice(r * nq, (r + 1) * nq)
        outs.append(gsig[:, c0:c0 + 1] * o_cmp[rows] + gsig[:, c0 + 1:c0 + 2] * o_sel[rows]
                    + gsig[:, c0 + 2:c0 + 3] * o_win[rows])
    return jnp.concatenate(outs, axis=1)


def _nsa_cmp_kernel(*refs, n_ch):
    x_refs = refs[:NSA_CMP_STRIDE]
    w1_ref, cpe_ref, cb1_ref, w2_ref, cb2_ref, kg_ref, kc_ref, vc_ref = refs[NSA_CMP_STRIDE:]

    def load_x(s, eg):
        return x_refs[s][:, eg * HEAD_DIM:(eg + 1) * HEAD_DIM]

    kc, vc = _cmp_mlp(load_x, n_ch, w1_ref, cpe_ref, cb1_ref, w2_ref, cb2_ref, kg_ref)
    for g in range(NSA_KV):
        kc_ref[0, g] = kc[g]
        vc_ref[0, g] = vc[g]


def _cmp_weight_specs(nd):
    z = (0,) * 8
    full = lambda shape: pl.BlockSpec(shape, lambda *a: z[:len(shape)])
    hid2 = 2 * NSA_CMP_HIDDEN
    return [full((2, NSA_CMP_STRIDE, HEAD_DIM, hid2)), full((2, NSA_CMP_STRIDE, SUBLANES, HEAD_DIM)),
            full((2, 1, NSA_CMP_HIDDEN)), full((2, NSA_CMP_HIDDEN, HEAD_DIM)), full((2, 1, HEAD_DIM)),
            full((1, HEAD_DIM))]


def nsa_cmp_prompt(z, cmpw, *, batch, seq):
    n_ch = seq // NSA_CMP_STRIDE
    seg = 4 * HEAD_DIM
    st = NSA_CMP_STRIDE
    width = z.shape[1]
    zp = z if z.shape[0] % st == 0 else z[:batch * seq]
    zc = zp.reshape(zp.shape[0] // st, st * width)
    shp = jax.ShapeDtypeStruct((batch, NSA_KV, n_ch, HEAD_DIM), F32)
    ospec = pl.BlockSpec((1, NSA_KV, n_ch, HEAD_DIM), lambda b: (b, 0, 0, 0))
    xspec = lambda s: pl.BlockSpec((n_ch, seg), lambda b: (b, (s * width + AB_CMP) // seg))
    return pl.pallas_call(
        functools.partial(_nsa_cmp_kernel, n_ch=n_ch),
        grid=(batch,),
        in_specs=[xspec(s) for s in range(st)] + _cmp_weight_specs(1),
        out_specs=[ospec, ospec],
        out_shape=[shp, shp],
        compiler_params=pltpu.CompilerParams(dimension_semantics=("parallel",)),
        name="nsa_cmp_prompt",
    )(*([zc] * st), *cmpw)


def _nsa_prep_kernel(sel_ref, win_ref, kg_ref, o_ref):
    d = HEAD_DIM
    parts = []
    for ref, row in ((sel_ref, 1), (win_ref, 2)):
        x = ref[...]
        for g in range(NSA_KV):
            parts.append(_rms(x[:, g * d:(g + 1) * d], kg_ref[row:row + 1, :]))
        parts.append(x[:, NSA_KV * d:])
    o_ref[...] = jnp.concatenate(parts, axis=1).astype(BF16)


def nsa_prep_prompt(z, k_gain, *, rows, tm):
    seg = 4 * HEAD_DIM
    return pl.pallas_call(
        _nsa_prep_kernel,
        grid=(rows // tm,),
        in_specs=[pl.BlockSpec((tm, seg), lambda i: (i, AB_SEL // seg)),
                  pl.BlockSpec((tm, seg), lambda i: (i, AB_WIN // seg)),
                  pl.BlockSpec((3, HEAD_DIM), lambda i: (0, 0))],
        out_specs=pl.BlockSpec((tm, 2 * seg), lambda i: (i, 0)),
        out_shape=jax.ShapeDtypeStruct((rows, 2 * seg), BF16),
        compiler_params=pltpu.CompilerParams(dimension_semantics=("parallel",)),
        name="nsa_prep_prompt",
    )(z, z, k_gain)


def _nsa_attn_kernel(q_ref, m_ref, kc_ref, vc_ref, ks_ref, vs_ref, kw_ref, vw_ref, qg_ref, o_ref, *, seq, kt):
    i = pl.program_id(2)
    nq = q_ref.shape[0]
    q4 = _nsa_q4(q_ref[...], qg_ref[...]).astype(BF16)
    qpos = i * nq + lax.broadcasted_iota(jnp.int32, (nq, 1), 0)
    n_ch = seq // NSA_CMP_STRIDE
    n_sb = -(-seq // NSA_SEL_BLOCK)

    o_cmp, pcs = _nsa_cmp_branch(q4, kc_ref[0, 0], vc_ref[0, 0], qpos, n_ch - 1)
    selb = _nsa_select(pcs, qpos, n_sb).astype(BF16)

    def sel_step(j, carry):
        m, l, acc = carry
        rows = pl.ds(pl.multiple_of(j * kt, kt), kt)
        s = _dot_nt(q4, ks_ref[rows, :]).reshape(NSA_REP, nq, kt)
        kpos = j * kt + lax.broadcasted_iota(jnp.int32, (1, kt), 1)
        mask = ((_dot(selb, _block_expand(LANES, kt, NSA_SEL_BLOCK, j * kt)) > 0.5) & (kpos <= qpos))[None]
        sm = jnp.where(mask, s, NEG)
        mn = jnp.maximum(m, jnp.max(sm, axis=-1, keepdims=True))
        p = jnp.where(mask, jnp.exp(sm - mn), 0.0)
        alpha = jnp.exp(m - mn)
        l = alpha * l + jnp.sum(p, axis=-1, keepdims=True)
        pv = _dot(p.reshape(NSA_REP * nq, kt).astype(BF16), vs_ref[rows, :])
        return mn, l, alpha * acc + pv.reshape(NSA_REP, nq, HEAD_DIM)

    n_kv = (i * nq + nq - 1) // kt + 1
    init = (jnp.full((NSA_REP, nq, 1), NEG, F32), jnp.zeros((NSA_REP, nq, 1), F32),
            jnp.zeros((NSA_REP, nq, HEAD_DIM), F32))
    _, l, acc = lax.fori_loop(0, n_kv, sel_step, init)
    o_sel = (acc / l).reshape(NSA_REP * nq, HEAD_DIM)

    span = min(NSA_WINDOW + nq, seq)
    start = pl.multiple_of(jnp.clip(i * nq - NSA_WINDOW, 0, seq - span), nq)
    s = _dot_nt(q4, kw_ref[pl.ds(start, span), :]).reshape(NSA_REP, nq, span)
    kpos = start + lax.broadcasted_iota(jnp.int32, (1, span), 1)
    p = _masked_softmax_rows(s, _nsa_window_mask(kpos, qpos))
    o_win = _dot(p.reshape(NSA_REP * nq, span).astype(BF16), vw_ref[pl.ds(start, span), :])

    o_ref[...] = _nsa_gate_mix(jax.nn.sigmoid(m_ref[...]), o_cmp, o_sel, o_win, nq).astype(o_ref.dtype)


def nsa_attn_prompt(z, kvp, kc, vc, q_gain, *, batch, seq, tq):
    nt = seq // tq
    gw = NSA_REP * HEAD_DIM
    kt = min(4 * tq, seq)
    row = lambda b, g, i: b * nt + i
    cspec = pl.BlockSpec((1, 1, seq // NSA_CMP_STRIDE, HEAD_DIM), lambda b, g, i: (b, g, 0, 0))
    kvspec = lambda c: pl.BlockSpec((seq, HEAD_DIM), lambda b, g, i: (b, c + g))
    return pl.pallas_call(
        functools.partial(_nsa_attn_kernel, seq=seq, kt=kt),
        grid=(batch, NSA_KV, nt),
        in_specs=[pl.BlockSpec((tq, gw), lambda b, g, i: (row(b, g, i), AB_NQ // gw + g)),
                  pl.BlockSpec((tq, LANES), lambda b, g, i: (row(b, g, i), AB_MISC0 // LANES + g)),
                  cspec, cspec, kvspec(0), kvspec(2), kvspec(4), kvspec(6),
                  pl.BlockSpec((1, HEAD_DIM), lambda b, g, i: (0, 0))],
        out_specs=pl.BlockSpec((tq, gw), lambda b, g, i: (row(b, g, i), g)),
        out_shape=jax.ShapeDtypeStruct((batch * seq, NSA_KV * gw), BF16),
        compiler_params=pltpu.CompilerParams(dimension_semantics=("parallel", "parallel", "arbitrary")),
        name="nsa_attn_prompt",
    )(z, z, kc, vc, kvp, kvp, kvp, kvp, q_gain.reshape(1, HEAD_DIM))


def nsa_cmp_weights(cw1, cb1, cw2, cb2, cpe, k_gain):
    st = NSA_CMP_STRIDE
    w1cat = jnp.concatenate([cw1[:, :st], cw1[:, st:]], axis=-1).astype(BF16)
    pe_rows = jnp.stack([cpe[:, :st], cpe[:, st:]], axis=2)
    pe_rows = jnp.pad(pe_rows, ((0, 0), (0, 0), (0, SUBLANES - 2), (0, 0))).astype(BF16)
    return (w1cat, pe_rows, cb1.reshape(2, 1, NSA_CMP_HIDDEN), cw2.astype(BF16), cb2.reshape(2, 1, HEAD_DIM),
            k_gain[0].reshape(1, HEAD_DIM))


NSA_ROW = 4 * NSA_KV * HEAD_DIM
CHUNKS_PER_PAGE = PAGE_SIZE // NSA_CMP_STRIDE


def _nsa_sample_kernel(pt_ref, *refs, n_pages, past, t_new):
    del pt_ref
    pages = refs[:n_pages]
    (q_ref, sel_ref, win_ref, m0_ref, m1_ref, wb_ref, w1_ref, cpe_ref, cb1_ref, w2_ref, cb2_ref, kg0_ref,
     kg_ref, qg_ref, o_ref) = refs[n_pages:]
    nq = SAMPLE_ROWS
    d = HEAD_DIM
    n_ch = n_pages * CHUNKS_PER_PAGE
    n_cmp = (past + t_new) // NSA_CMP_STRIDE - 1
    n_sb = -(-(past + t_new) // NSA_SEL_BLOCK)

    def piece(p, s, col):
        off = s * NSA_ROW + col * d
        return pages[p][0, :, off:off + d]

    def load_x(s, eg):
        return jnp.concatenate([piece(p, s, eg) for p in range(n_pages)], axis=0)

    def past_rows(col):
        return jnp.concatenate([piece(p, s, col) for p in range(n_pages) for s in range(NSA_CMP_STRIDE)], axis=0)

    kc, vc = _cmp_mlp(load_x, n_ch, w1_ref, cpe_ref, cb1_ref, w2_ref, cb2_ref, kg0_ref)

    qpos = past + lax.broadcasted_iota(jnp.int32, (nq, 1), 0)
    nk = past + LANES
    idx = lax.broadcasted_iota(jnp.int32, (1, nk), 1)
    kpos = jnp.where(idx < past,
                     (idx & ~(PAGE_SIZE - 1)) + NSA_CMP_STRIDE * (idx & (CHUNKS_PER_PAGE - 1))
                     + (jnp.right_shift(idx, 3) & (NSA_CMP_STRIDE - 1)), idx)
    bi = lax.broadcasted_iota(jnp.int32, (LANES, nk), 0)
    esel = (bi == jnp.right_shift(kpos, int(math.log2(NSA_SEL_BLOCK)))).astype(BF16)
    causal = kpos <= qpos
    nw = NSA_WINDOW + LANES
    wmask = _nsa_window_mask(past - NSA_WINDOW + lax.broadcasted_iota(jnp.int32, (1, nw), 1), qpos)
    zpad = jnp.zeros((LANES - nq, d), F32)
    q = q_ref[0]
    sel_new = sel_ref[0]
    win_new = win_ref[0]
    wb = wb_ref[0]
    kg_sel = kg_ref[1:2, :]
    kg_win = kg_ref[2:3, :]
    col = lambda x, c: x[:, c * d:(c + 1) * d]

    for g in range(NSA_KV):
        q4 = _nsa_q4(q[:, g * NSA_REP * d:(g + 1) * NSA_REP * d], qg_ref[...]).astype(BF16)
        o_cmp, pcs = _nsa_cmp_branch(q4, kc[g], vc[g], qpos, n_cmp)
        selb = _nsa_select(pcs, qpos, n_sb).astype(BF16)

        ks = jnp.concatenate([_rms(past_rows(2 * NSA_KV + g), kg_sel), _rms(col(sel_new, g), kg_sel), zpad], axis=0)
        vs = jnp.concatenate([past_rows(3 * NSA_KV + g), col(sel_new, NSA_KV + g), zpad], axis=0)
        s = _dot_nt(q4, ks.astype(BF16)).reshape(NSA_REP, nq, nk)
        p = _masked_softmax_rows(s, (_dot(selb, esel) > 0.5) & causal)
        o_sel = _dot(p.reshape(NSA_REP * nq, nk).astype(BF16), vs.astype(BF16))

        kw = jnp.concatenate([_rms(col(wb, g), kg_win), _rms(col(win_new, g), kg_win), zpad], axis=0)
        vw = jnp.concatenate([col(wb, NSA_KV + g), col(win_new, NSA_KV + g), zpad], axis=0)
        s = _dot_nt(q4, kw.astype(BF16)).reshape(NSA_REP, nq, nw)
        p = _masked_softmax_rows(s, wmask)
        o_win = _dot(p.reshape(NSA_REP * nq, nw).astype(BF16), vw.astype(BF16))

        gsig = jax.nn.sigmoid((m0_ref, m1_ref)[g][0])
        o_ref[0, :, g * NSA_REP * d:(g + 1) * NSA_REP * d] = _nsa_gate_mix(gsig, o_cmp, o_sel, o_win, nq).astype(o_ref.dtype)


def nsa_sample(zs, page_table, cache_kv, cache_win, cmpw, k_gain, q_gain, *, t_new):
    db, n_pages = page_table.shape
    past = n_pages * PAGE_SIZE
    assert cache_win.shape[1] == NSA_WINDOW and (past + t_new) // NSA_CMP_STRIDE == n_pages * CHUNKS_PER_PAGE
    n_phys = cache_kv.shape[0]
    page_w = NSA_CMP_STRIDE * NSA_ROW
    pages = cache_kv.reshape(n_phys, CHUNKS_PER_PAGE, page_w)
    wb = cache_win.reshape(db, NSA_WINDOW, 2 * NSA_KV * HEAD_DIM)
    seg = 4 * HEAD_DIM
    qw = NSA_HEADS * HEAD_DIM
    zspec = lambda w, off: pl.BlockSpec((1, SAMPLE_ROWS, w), lambda b, pt: (b, 0, off // w))
    page_spec = lambda p: pl.BlockSpec((1, CHUNKS_PER_PAGE, page_w), lambda b, pt: (pt[b, p], 0, 0))
    grid_spec = pltpu.PrefetchScalarGridSpec(
        num_scalar_prefetch=1,
        grid=(db,),
        in_specs=[page_spec(p) for p in range(n_pages)]
        + [zspec(qw, AB_NQ), zspec(seg, AB_SEL), zspec(seg, AB_WIN), zspec(LANES, AB_MISC0), zspec(LANES, AB_MISC1),
           pl.BlockSpec((1, NSA_WINDOW, seg), lambda b, pt: (b, 0, 0))]
        + _cmp_weight_specs(2)
        + [pl.BlockSpec((3, HEAD_DIM), lambda b, pt: (0, 0)), pl.BlockSpec((1, HEAD_DIM), lambda b, pt: (0, 0))],
        out_specs=pl.BlockSpec((1, SAMPLE_ROWS, qw), lambda b, pt: (b, 0, 0)),
    )
    return pl.pallas_call(
        functools.partial(_nsa_sample_kernel, n_pages=n_pages, past=past, t_new=t_new),
        grid_spec=grid_spec,
        out_shape=jax.ShapeDtypeStruct((db, SAMPLE_ROWS, qw), BF16),
        compiler_params=pltpu.CompilerParams(dimension_semantics=("parallel",)),
        name="nsa_sample",
    )(page_table, *([pages] * n_pages), zs, zs, zs, zs, zs, wb, *cmpw, k_gain, q_gain.reshape(1, HEAD_DIM))


C_Q, C_K, C_V, C_TOTAL = 0, MOBA_HEADS * HEAD_DIM, (MOBA_HEADS + MOBA_KV) * HEAD_DIM, (MOBA_HEADS + 2 * MOBA_KV) * HEAD_DIM


def _masked_softmax2d(s, mask):
    sm = jnp.where(mask, s, NEG)
    m = jnp.max(sm, axis=-1, keepdims=True)
    e = jnp.where(mask, jnp.exp(sm - m), 0.0)
    dsum = jnp.sum(e, axis=-1, keepdims=True)
    return e / jnp.where(dsum > 0, dsum, 1.0)


def _moba_q4(q, qg):
    return jnp.concatenate([_rms(q[:, r * HEAD_DIM:(r + 1) * HEAD_DIM], qg) * HEAD_DIM ** -0.5
                            for r in range(MOBA_REP)], axis=0)


def _moba_select(q4, kmean_pad, cur, n_blocks):
    gs = _dot_nt(q4, kmean_pad.astype(BF16))
    blk = lax.broadcasted_iota(jnp.int32, (1, LANES), 1)
    score = jnp.where((blk < cur) & (blk < n_blocks), gs, NEG)
    sel = _topk_mask(score, min(MOBA_TOPK, n_blocks))
    return jnp.where(blk == cur, 1.0, sel)


def _moba_prep_kernel(k_ref, v_ref, kg_ref, o_ref, km_ref):
    k = k_ref[...]
    kn = jnp.concatenate([_rms(k[:, h * HEAD_DIM:(h + 1) * HEAD_DIM], kg_ref[...]) for h in range(MOBA_KV)], axis=1)
    km_ref[0] = jnp.mean(kn, axis=0, keepdims=True)
    o_ref[...] = jnp.concatenate([kn, v_ref[...]], axis=1).astype(BF16)


def moba_prep_prompt(zc, k_gain, *, rows):
    w = MOBA_KV * HEAD_DIM
    nblk = rows // MOBA_BLOCK
    return pl.pallas_call(
        _moba_prep_kernel,
        grid=(nblk,),
        in_specs=[pl.BlockSpec((MOBA_BLOCK, w), lambda i: (i, C_K // w)),
                  pl.BlockSpec((MOBA_BLOCK, w), lambda i: (i, C_V // w)),
                  pl.BlockSpec((1, HEAD_DIM), lambda i: (0, 0))],
        out_specs=[pl.BlockSpec((MOBA_BLOCK, 2 * w), lambda i: (i, 0)),
                   pl.BlockSpec((1, 1, w), lambda i: (i, 0, 0))],
        out_shape=[jax.ShapeDtypeStruct((rows, 2 * w), BF16), jax.ShapeDtypeStruct((nblk, 1, w), F32)],
        compiler_params=pltpu.CompilerParams(dimension_semantics=("parallel",)),
        name="moba_prep_prompt",
    )(zc, zc, k_gain.reshape(1, HEAD_DIM))


def _moba_attn_kernel(q_ref, km_ref, k_ref, v_ref, qg_ref, o_ref, *, seq):
    i = pl.program_id(2)
    nq = q_ref.shape[0]
    nb = seq // MOBA_BLOCK
    rq = MOBA_REP * nq
    q4 = _moba_q4(q_ref[...], qg_ref[...]).astype(BF16)
    km = jnp.concatenate([km_ref[0], jnp.zeros((LANES - nb, HEAD_DIM), F32)], axis=0)
    sel = _moba_select(q4, km, i, nb)
    qpos = i * nq + lax.broadcasted_iota(jnp.int32, (nq, 1), 0)
    qpos4 = jnp.concatenate([qpos] * MOBA_REP, axis=0)
    blk = lax.broadcasted_iota(jnp.int32, (1, LANES), 1)

    def step(n, carry):
        m, l, acc = carry
        rows = pl.ds(pl.multiple_of(n * MOBA_BLOCK, MOBA_BLOCK), MOBA_BLOCK)
        s = _dot_nt(q4, k_ref[rows, :])
        chosen = jnp.sum(jnp.where(blk == n, sel, 0.0), axis=-1, keepdims=True) > 0.5
        kpos = n * MOBA_BLOCK + lax.broadcasted_iota(jnp.int32, (1, MOBA_BLOCK), 1)
        mask = chosen & (kpos <= qpos4)
        sm = jnp.where(mask, s, NEG)
        mn = jnp.maximum(m, jnp.max(sm, axis=-1, keepdims=True))
        p = jnp.where(mask, jnp.exp(sm - mn), 0.0)
        alpha = jnp.exp(m - mn)
        l = alpha * l + jnp.sum(p, axis=-1, keepdims=True)
        return mn, l, alpha * acc + _dot(p.astype(BF16), v_ref[rows, :])

    init = (jnp.full((rq, 1), NEG, F32), jnp.zeros((rq, 1), F32), jnp.zeros((rq, HEAD_DIM), F32))
    _, l, acc = lax.fori_loop(0, i + 1, step, init)
    o = acc / l
    o_ref[...] = jnp.concatenate([o[r * nq:(r + 1) * nq] for r in range(MOBA_REP)], axis=1).astype(o_ref.dtype)


def moba_attn_prompt(zc, kvp, kmean, q_gain, *, batch, seq):
    nt = seq // MOBA_BLOCK
    gw = MOBA_REP * HEAD_DIM
    km = kmean.reshape(batch, nt, MOBA_KV * HEAD_DIM)
    return pl.pallas_call(
        functools.partial(_moba_attn_kernel, seq=seq),
        grid=(batch, MOBA_KV, nt),
        in_specs=[pl.BlockSpec((MOBA_BLOCK, gw), lambda b, g, i: (b * nt + i, g)),
                  pl.BlockSpec((1, nt, HEAD_DIM), lambda b, g, i: (b, 0, g)),
                  pl.BlockSpec((seq, HEAD_DIM), lambda b, g, i: (b, g)),
                  pl.BlockSpec((seq, HEAD_DIM), lambda b, g, i: (b, MOBA_KV + g)),
                  pl.BlockSpec((1, HEAD_DIM), lambda b, g, i: (0, 0))],
        out_specs=pl.BlockSpec((MOBA_BLOCK, gw), lambda b, g, i: (b * nt + i, g)),
        out_shape=jax.ShapeDtypeStruct((batch * seq, MOBA_HEADS * HEAD_DIM), BF16),
        compiler_params=pltpu.CompilerParams(dimension_semantics=("parallel", "parallel", "arbitrary")),
        name="moba_attn_prompt",
    )(zc, km, kvp, kvp, q_gain.reshape(1, HEAD_DIM))


def _moba_sample_kernel(pt_ref, *refs, n_pages, past, t_new):
    del pt_ref
    pages = refs[:n_pages]
    q_ref, kn_ref, vn_ref, kg_ref, qg_ref, o_ref = refs[n_pages:]
    nq = SAMPLE_ROWS
    d = HEAD_DIM
    rq = MOBA_REP * nq
    nb = -(-(past + t_new) // MOBA_BLOCK)
    n_past_blocks = past // MOBA_BLOCK
    nk = past + LANES
    qpos = past + lax.broadcasted_iota(jnp.int32, (nq, 1), 0)
    qpos4 = jnp.concatenate([qpos] * MOBA_REP, axis=0)
    cur = jnp.right_shift(qpos4, int(math.log2(MOBA_BLOCK)))
    kpos = lax.broadcasted_iota(jnp.int32, (1, nk), 1)
    expand = _block_expand(LANES, nk, MOBA_BLOCK)
    causal = kpos <= qpos4
    zpad = jnp.zeros((LANES - nq, d), F32)
    q = q_ref[0]
    k_new = kn_ref[0]
    v_new = vn_ref[0]
    for g in range(MOBA_KV):
        kn = _rms(jnp.concatenate([pg[0, :, g * d:(g + 1) * d] for pg in pages], axis=0), kg_ref[...])
        km = jnp.concatenate([jnp.mean(kn[n * MOBA_BLOCK:(n + 1) * MOBA_BLOCK], axis=0, keepdims=True)
                              for n in range(n_past_blocks)] + [jnp.zeros((LANES - n_past_blocks, d), F32)], axis=0)
        q4 = _moba_q4(q[:, g * MOBA_REP * d:(g + 1) * MOBA_REP * d], qg_ref[...]).astype(BF16)
        sel = _moba_select(q4, km, cur, nb)
        keys = jnp.concatenate([kn, _rms(k_new[:, g * d:(g + 1) * d], kg_ref[...]), zpad], axis=0)
        vals = jnp.concatenate([pg[0, :, (MOBA_KV + g) * d:(MOBA_KV + g + 1) * d] for pg in pages]
                               + [v_new[:, g * d:(g + 1) * d], zpad], axis=0)
        s = _dot_nt(q4, keys.astype(BF16))
        p = _masked_softmax2d(s, (_dot(sel.astype(BF16), expand) > 0.5) & causal)
        o = _dot(p.astype(BF16), vals.astype(BF16))
        for r in range(MOBA_REP):
            h = g * MOBA_REP + r
            o_ref[0, :, h * d:(h + 1) * d] = o[r * nq:(r + 1) * nq].astype(o_ref.dtype)


def moba_sample(zcs, page_table, cache_kv, k_gain, q_gain, *, t_new):
    db, n_pages = page_table.shape
    past = n_pages * PAGE_SIZE
    assert past % MOBA_BLOCK == 0 and t_new <= MOBA_BLOCK
    n_phys = cache_kv.shape[0]
    w = MOBA_KV * HEAD_DIM
    pages = cache_kv.reshape(n_phys, PAGE_SIZE, 2 * w)
    qw = MOBA_HEADS * HEAD_DIM
    zspec = lambda wd, off: pl.BlockSpec((1, SAMPLE_ROWS, wd), lambda b, pt: (b, 0, off // wd))
    page_spec = lambda p: pl.BlockSpec((1, PAGE_SIZE, 2 * w), lambda b, pt: (pt[b, p], 0, 0))
    grid_spec = pltpu.PrefetchScalarGridSpec(
        num_scalar_prefetch=1,
        grid=(db,),
        in_specs=[page_spec(p) for p in range(n_pages)]
        + [zspec(qw, C_Q), zspec(w, C_K), zspec(w, C_V),
           pl.BlockSpec((1, HEAD_DIM), lambda b, pt: (0, 0)), pl.BlockSpec((1, HEAD_DIM), lambda b, pt: (0, 0))],
        out_specs=pl.BlockSpec((1, SAMPLE_ROWS, qw), lambda b, pt: (b, 0, 0)),
    )
    return pl.pallas_call(
        functools.partial(_moba_sample_kernel, n_pages=n_pages, past=past, t_new=t_new),
        grid_spec=grid_spec,
        out_shape=jax.ShapeDtypeStruct((db, SAMPLE_ROWS, qw), BF16),
        compiler_params=pltpu.CompilerParams(dimension_semantics=("parallel",)),
        name="moba_sample",
    )(page_table, *([pages] * n_pages), zcs, zcs, zcs, k_gain.reshape(1, HEAD_DIM), q_gain.reshape(1, HEAD_DIM))


ROW_TILE = 512
GLA_TILE = 128
NSA_Q_TILE = 128


def _ab_weight_layout(w_in_ab):
    d = w_in_ab.shape[0]
    widths = (GLA_HEADS * GLA_DK, GLA_HEADS * GLA_DK, GLA_HEADS * GLA_DV, GLA_HEADS * GLA_DV, GLA_GATE_RANK,
              NSA_HEADS * HEAD_DIM, NSA_HEADS * 3, 6 * NSA_KV * HEAD_DIM)
    offs = [0]
    for w in widths:
        offs.append(offs[-1] + w)
    gq, gk, gv, gr, ga, nq, ngt, nkv = (w_in_ab[:, offs[i]:offs[i + 1]] for i in range(len(widths)))
    half = NSA_REP * 3
    zeros = lambda n: jnp.zeros((d, n), w_in_ab.dtype)
    misc0 = jnp.concatenate([ga, ngt[:, :half], zeros(LANES - GLA_GATE_RANK - half)], axis=1)
    misc1 = jnp.concatenate([zeros(GLA_GATE_RANK), ngt[:, half:], zeros(LANES - GLA_GATE_RANK - half)], axis=1)
    w = jnp.concatenate([gq, gk, gv, gr, nq, nkv, misc0, misc1, zeros(AB_TOTAL - AB_MISC1 - LANES)], axis=1)
    return w.astype(BF16)


def _pad_sample_rows(z, db, t_new):
    return jnp.pad(z.reshape(db, t_new, z.shape[-1]), ((0, 0), (0, SAMPLE_ROWS - t_new), (0, 0)))


def _conv_ffn(h, l, n_p, batch, seq, t_new, state_ffn_conv, norm_ffn, ffn_w_up, ffn_conv_w, ffn_conv_b, ffn_w_down):
    d_ff = ffn_conv_w.shape[-1]
    db = state_ffn_conv.shape[1]
    u = norm_matmul(h, norm_ffn[l], ffn_w_up[l].astype(BF16), tm=ROW_TILE, tn=1024)
    act = jnp.concatenate([
        ffn_act_prompt(u, ffn_conv_w[l], ffn_conv_b[l], batch=batch, seq=seq, tm=256, tn=512),
        ffn_act_sample(u, n_p, state_ffn_conv[l], ffn_conv_w[l], ffn_conv_b[l], t_new=t_new, tn=512)], axis=0)
    h = matmul_residual(act, ffn_w_down[l].astype(BF16), h, tm=ROW_TILE, tn=512, tk=512)
    keep = FFN_CONV - 1
    gate_p = u[:n_p, :d_ff].reshape(batch, seq, d_ff)[:, seq - keep:]
    gate_s = u[n_p:, :d_ff].reshape(db, t_new, d_ff)[:, t_new - keep:]
    return h, gate_p, gate_s


def kernel(x_prompt, x_sample, page_table, cache_nsa_kv, cache_nsa_win, state_gla, cache_moba_kv, state_ffn_conv, norm_mix, w_in_ab, gla_a_w2, gla_a_b, gla_o_norm, nsa_q_norm, nsa_k_norm, nsa_cmp_w1, nsa_cmp_b1, nsa_cmp_w2, nsa_cmp_b2, nsa_cmp_pe, w_out_ab, w_in_c, moba_q_norm, moba_k_norm, w_out_c, norm_ffn, ffn_w_up, ffn_conv_w, ffn_conv_b, ffn_w_down):
    batch, seq, d_model = x_prompt.shape
    db, t_new, _ = x_sample.shape
    n_p, n_s = batch * seq, db * t_new
    n = n_p + n_s
    assert norm_mix.shape[0] == 2 and w_in_ab.shape[0] == 1 and w_in_c.shape[0] == 1
    assert FFN_CONV - 1 <= t_new <= SAMPLE_ROWS and n % ROW_TILE == 0 and n_p % ROW_TILE == 0
    ffn_args = (state_ffn_conv, norm_ffn, ffn_w_up, ffn_conv_w, ffn_conv_b, ffn_w_down)

    h = jnp.concatenate([x_prompt.reshape(n_p, d_model), x_sample.reshape(n_s, d_model)], axis=0)

    z = norm_matmul(h, norm_mix[0], _ab_weight_layout(w_in_ab[0]), tm=ROW_TILE, tn=1024)
    zs = _pad_sample_rows(z[n_p:], db, t_new)
    w2p = jnp.pad(gla_a_w2[0], ((0, LANES - GLA_GATE_RANK), (0, 0))).astype(BF16)
    gla_zero = jnp.zeros((batch,) + state_gla.shape[2:], F32)
    zg = z if n % GLA_TILE == 0 else z[:n_p]
    og_p, gla_p = gla_mixer(zg.reshape(-1, GLA_TILE, AB_TOTAL), gla_zero, w2p, gla_a_b[0], gla_o_norm[0],
                            n_seq=batch, tiles_per_seq=seq // GLA_TILE, chunk=GLA_CHUNK, n_valid=GLA_CHUNK, out_dtype=BF16)
    og_s, gla_s = gla_mixer(zs, state_gla[0], w2p, gla_a_b[0], gla_o_norm[0],
                            n_seq=db, tiles_per_seq=1, chunk=SAMPLE_ROWS, n_valid=t_new, out_dtype=BF16)
    cmpw = nsa_cmp_weights(nsa_cmp_w1[0], nsa_cmp_b1[0], nsa_cmp_w2[0], nsa_cmp_b2[0], nsa_cmp_pe[0], nsa_k_norm[0])
    kc, vc = nsa_cmp_prompt(z, cmpw, batch=batch, seq=seq)
    kvp = nsa_prep_prompt(z, nsa_k_norm[0], rows=n_p, tm=ROW_TILE)
    on_p = nsa_attn_prompt(z, kvp, kc, vc, nsa_q_norm[0], batch=batch, seq=seq, tq=NSA_Q_TILE)
    on_s = nsa_sample(zs, page_table, cache_nsa_kv[0], cache_nsa_win[0], cmpw, nsa_k_norm[0], nsa_q_norm[0], t_new=t_new)
    mix_p = jnp.concatenate([og_p.reshape(n_p, -1), on_p], axis=1)
    mix_s = jnp.concatenate([og_s[:, :t_new].reshape(n_s, -1), on_s[:, :t_new].reshape(n_s, -1)], axis=1)
    h = matmul_residual(jnp.concatenate([mix_p, mix_s], axis=0), w_out_ab[0].astype(BF16), h,
                        tm=ROW_TILE, tn=512, tk=mix_p.shape[1])
    h, conv_p0, conv_s0 = _conv_ffn(h, 0, n_p, batch, seq, t_new, *ffn_args)

    kv_w = 4 * NSA_KV * HEAD_DIM
    win_w = 2 * NSA_KV * HEAD_DIM
    win_keep = min(NSA_WINDOW, seq)
    nsa_kv_p = z[:n_p, AB_CMP:AB_CMP + kv_w].reshape(1, batch, seq, 4, NSA_KV, HEAD_DIM)
    nsa_kv_s = z[n_p:, AB_CMP:AB_CMP + kv_w].reshape(1, db, t_new, 4, NSA_KV, HEAD_DIM)
    nsa_win_p = z[:n_p, AB_WIN:AB_WIN + win_w].reshape(batch, seq, win_w)[:, seq - win_keep:]
    nsa_win_p = nsa_win_p.reshape(1, batch, win_keep, 2, NSA_KV, HEAD_DIM)
    win_new = z[n_p:, AB_WIN:AB_WIN + win_w].reshape(db, t_new, 2, NSA_KV, HEAD_DIM)
    nsa_win_s = jnp.concatenate([cache_nsa_win[0], win_new], axis=1)[None, :, -NSA_WINDOW:]

    zc = norm_matmul(h, norm_mix[1], w_in_c[0].astype(BF16), tm=ROW_TILE, tn=1024)
    zcs = _pad_sample_rows(zc[n_p:], db, t_new)
    kvm, kmean = moba_prep_prompt(zc, moba_k_norm[0], rows=n_p)
    om_p = moba_attn_prompt(zc, kvm, kmean, moba_q_norm[0], batch=batch, seq=seq)
    om_s = moba_sample(zcs, page_table, cache_moba_kv[0], moba_k_norm[0], moba_q_norm[0], t_new=t_new)
    om = jnp.concatenate([om_p, om_s[:, :t_new].reshape(n_s, -1)], axis=0)
    h = matmul_residual(om, w_out_c[0].astype(BF16), h, tm=ROW_TILE, tn=512, tk=om.shape[1])
    h, conv_p1, conv_s1 = _conv_ffn(h, 1, n_p, batch, seq, t_new, *ffn_args)

    moba_kv_p = zc[:n_p, C_K:].reshape(1, batch, seq, 2, MOBA_KV, HEAD_DIM)
    moba_kv_s = zc[n_p:, C_K:].reshape(1, db, t_new, 2, MOBA_KV, HEAD_DIM)

    return (h[:n_p].reshape(batch, seq, d_model), h[n_p:].reshape(db, t_new, d_model),
            nsa_kv_p, nsa_kv_s, nsa_win_p, nsa_win_s, gla_p[None], gla_s[None], moba_kv_p, moba_kv_s,
            jnp.stack([conv_p0, conv_p1]), jnp.stack([conv_s0, conv_s1]))
```

```python
import functools
import math

import jax
import jax.numpy as jnp
from jax import lax
from jax.experimental import pallas as pl
from jax.experimental.pallas import tpu as pltpu

F32 = jnp.float32
BF16 = jnp.bfloat16

HEAD_DIM = 128
GLA_HEADS = 4
GLA_DK = 128
GLA_DV = 256
GLA_GATE_RANK = 16
GLA_TAU = 16.0
GLA_CHUNK = 16
NSA_HEADS = 8
NSA_KV = 2
NSA_REP = NSA_HEADS // NSA_KV
NSA_CMP_STRIDE = 16
NSA_CMP_LEN = 2 * NSA_CMP_STRIDE
NSA_CMP_HIDDEN = 256
NSA_SEL_BLOCK = 64
NSA_N_SEL = 16
NSA_WINDOW = 512
MOBA_HEADS = 16
MOBA_KV = 4
MOBA_REP = MOBA_HEADS // MOBA_KV
MOBA_BLOCK = 256
MOBA_TOPK = 3
FFN_CONV = 3
NORM_EPS = 1e-6
PAGE_SIZE = 128

LANES = 128
SUBLANES = 8
NEG = -1e30
MASK_BIAS = -(2.0 ** 100)
SAMPLE_ROWS = 8

AB_GQ, AB_GK, AB_GV, AB_GR, AB_NQ = 0, 512, 1024, 2048, 3072
AB_CMP, AB_SEL, AB_WIN, AB_MISC0, AB_MISC1, AB_TOTAL = 4096, 4608, 5120, 5632, 5760, 6144
MISC_GATE_COL = GLA_GATE_RANK


def _rms(x, g):
    return x * lax.rsqrt(jnp.mean(x * x, axis=-1, keepdims=True) + NORM_EPS) * g


def _dot_nt(a, b):
    return lax.dot_general(a, b, (((1,), (1,)), ((), ())), preferred_element_type=F32)


def _dot_tn(a, b):
    return lax.dot_general(a, b, (((0,), (0,)), ((), ())), preferred_element_type=F32)


def _dot(a, b):
    return jnp.dot(a, b, preferred_element_type=F32)


def _silu(x):
    return x * jax.nn.sigmoid(x)


def _norm_matmul_kernel(x_ref, g_ref, w_ref, o_ref, xn_ref):
    @pl.when(pl.program_id(1) == 0)
    def _():
        xn_ref[...] = _rms(x_ref[...], g_ref[...]).astype(BF16)

    o_ref[...] = _dot(xn_ref[...], w_ref[...])


def norm_matmul(x, g, w, *, tm, tn):
    n, k = x.shape
    nout = w.shape[1]
    return pl.pallas_call(
        _norm_matmul_kernel,
        grid=(n // tm, nout // tn),
        in_specs=[pl.BlockSpec((tm, k), lambda i, j: (i, 0)),
                  pl.BlockSpec((1, k), lambda i, j: (0, 0)),
                  pl.BlockSpec((k, tn), lambda i, j: (0, j))],
        out_specs=pl.BlockSpec((tm, tn), lambda i, j: (i, j)),
        out_shape=jax.ShapeDtypeStruct((n, nout), F32),
        scratch_shapes=[pltpu.VMEM((tm, k), BF16)],
        compiler_params=pltpu.CompilerParams(dimension_semantics=("parallel", "arbitrary")),
        name="norm_matmul",
    )(x, g.reshape(1, k), w)


def _matmul_res_kernel(a_ref, w_ref, r_ref, o_ref):
    d = _dot(a_ref[...], w_ref[...])

    @pl.when(pl.program_id(2) == 0)
    def _():
        o_ref[...] = r_ref[...] + d

    @pl.when(pl.program_id(2) > 0)
    def _():
        o_ref[...] += d


def matmul_residual(a, w, res, *, tm, tn, tk):
    n, k = a.shape
    nout = w.shape[1]
    return pl.pallas_call(
        _matmul_res_kernel,
        grid=(n // tm, nout // tn, k // tk),
        in_specs=[pl.BlockSpec((tm, tk), lambda i, j, kk: (i, kk)),
                  pl.BlockSpec((tk, tn), lambda i, j, kk: (kk, j)),
                  pl.BlockSpec((tm, tn), lambda i, j, kk: (i, j))],
        out_specs=pl.BlockSpec((tm, tn), lambda i, j, kk: (i, j)),
        out_shape=jax.ShapeDtypeStruct((n, nout), F32),
        compiler_params=pltpu.CompilerParams(dimension_semantics=("parallel", "parallel", "arbitrary")),
        name="matmul_residual",
    )(a, w, res)


def _conv_act(g, p1, p2, val, cw_ref, cb_ref):
    c = cb_ref[...] + cw_ref[0:1, :] * p2 + cw_ref[1:2, :] * p1 + cw_ref[2:3, :] * g
    return (_silu(c) * val).astype(BF16)


def _ffn_act_prompt_kernel(g_ref, v_ref, halo_ref, cw_ref, cb_ref, o_ref):
    g = g_ref[...]
    tm = g.shape[0]
    halo = jnp.where(pl.program_id(1) == 0, 0.0, halo_ref[...])
    row = lax.broadcasted_iota(jnp.int32, (tm, 1), 0)
    p1 = jnp.where(row == 0, halo[7:8, :], pltpu.roll(g, 1, 0))
    p2 = jnp.where(row == 0, halo[6:7, :], jnp.where(row == 1, halo[7:8, :], pltpu.roll(g, 2, 0)))
    o_ref[...] = _conv_act(g, p1, p2, v_ref[...], cw_ref, cb_ref)


def ffn_act_prompt(u, conv_w, conv_b, *, batch, seq, tm, tn):
    d_ff = u.shape[1] // 2
    nj = d_ff // tn
    nt = seq // tm
    hb = tm // SUBLANES
    return pl.pallas_call(
        _ffn_act_prompt_kernel,
        grid=(batch, nt, nj),
        in_specs=[pl.BlockSpec((tm, tn), lambda b, i, j: (b * nt + i, j)),
                  pl.BlockSpec((tm, tn), lambda b, i, j: (b * nt + i, nj + j)),
                  pl.BlockSpec((SUBLANES, tn), lambda b, i, j: (jnp.maximum((b * nt + i) * hb - 1, 0), j)),
                  pl.BlockSpec((FFN_CONV, tn), lambda b, i, j: (0, j)),
                  pl.BlockSpec((1, tn), lambda b, i, j: (0, j))],
        out_specs=pl.BlockSpec((tm, tn), lambda b, i, j: (b * nt + i, j)),
        out_shape=jax.ShapeDtypeStruct((batch * seq, d_ff), BF16),
        compiler_params=pltpu.CompilerParams(dimension_semantics=("parallel", "parallel", "parallel")),
        name="ffn_act_prompt",
    )(u, u, u, conv_w, conv_b.reshape(1, d_ff))


def _ffn_act_sample_kernel(g_ref, v_ref, s1_ref, s2_ref, cw_ref, cb_ref, o_ref, *, t_new):
    g = g_ref[...]
    tm = g.shape[0]
    t = lax.broadcasted_iota(jnp.int32, (tm, 1), 0) % t_new
    p1 = jnp.where(t >= 1, pltpu.roll(g, 1, 0), s1_ref[...])
    p2 = jnp.where(t >= 2, pltpu.roll(g, 2, 0), s2_ref[...])
    o_ref[...] = _conv_act(g, p1, p2, v_ref[...], cw_ref, cb_ref)


def ffn_act_sample(u, row0, conv_state, conv_w, conv_b, *, t_new, tn):
    d_ff = u.shape[1] // 2
    nj = d_ff // tn
    db = conv_state.shape[0]
    tm = db * t_new
    rb = row0 // tm
    zero = jnp.zeros((db, t_new - 1, d_ff), F32)
    s1 = jnp.concatenate([conv_state[:, 1:2], zero], axis=1).reshape(tm, d_ff)
    s2 = jnp.concatenate([conv_state, jnp.zeros((db, t_new - 2, d_ff), F32)], axis=1).reshape(tm, d_ff)
    return pl.pallas_call(
        functools.partial(_ffn_act_sample_kernel, t_new=t_new),
        grid=(nj,),
        in_specs=[pl.BlockSpec((tm, tn), lambda j: (rb, j)),
                  pl.BlockSpec((tm, tn), lambda j: (rb, nj + j)),
                  pl.BlockSpec((tm, tn), lambda j: (0, j)),
                  pl.BlockSpec((tm, tn), lambda j: (0, j)),
                  pl.BlockSpec((FFN_CONV, tn), lambda j: (0, j)),
                  pl.BlockSpec((1, tn), lambda j: (0, j))],
        out_specs=pl.BlockSpec((tm, tn), lambda j: (0, j)),
        out_shape=jax.ShapeDtypeStruct((tm, d_ff), BF16),
        compiler_params=pltpu.CompilerParams(dimension_semantics=("parallel",)),
        name="ffn_act_sample",
    )(u, u, s1, s2, conv_w, conv_b.reshape(1, d_ff))


def _gla_kernel(q_ref, k_ref, v_ref, r_ref, m_ref, w2_ref, b2_ref, on_ref, s0_ref, o_ref, sout_ref,
                st_ref, cum_ref, *, chunk, n_valid, nsub):
    i = pl.program_id(1)
    tt = nsub * chunk

    @pl.when(i == 0)
    def _():
        for h in range(GLA_HEADS):
            st_ref[h] = s0_ref[0, h].T

    a = _dot(m_ref[0].astype(BF16), w2_ref[...]) + b2_ref[...]
    a = (jnp.minimum(a, 0.0) - jnp.log(1.0 + jnp.exp(-jnp.abs(a)))) / GLA_TAU
    pos = lax.broadcasted_iota(jnp.int32, (tt, 1), 0) % chunk
    if n_valid < chunk:
        a = jnp.where(pos < n_valid, a, 0.0)
    cum = a
    sh = 1
    while sh < chunk:
        cum = cum + jnp.where(pos >= sh, pltpu.roll(cum, sh, 0), 0.0)
        sh *= 2
    cum_ref[...] = cum

    ti = lax.broadcasted_iota(jnp.int32, (chunk, 1), 0)

    def step(c, carry):
        rows = pl.ds(pl.multiple_of(c * chunk, chunk), chunk)
        for h in range(GLA_HEADS):
            kcols = slice(h * GLA_DK, (h + 1) * GLA_DK)
            vcols = slice(h * GLA_DV, (h + 1) * GLA_DV)
            qh = q_ref[0, rows, kcols] * GLA_DK ** -0.5
            kh = k_ref[0, rows, kcols]
            vh = v_ref[0, rows, vcols]
            ch = cum_ref[rows, kcols]
            st = st_ref[h]
            o = _dot_nt((qh * jnp.exp(ch)).astype(BF16), st.astype(BF16))
            for s in range(chunk):
                d = jnp.exp(jnp.where(ti >= s, ch - ch[s:s + 1, :], NEG))
                w = jnp.sum(qh * kh[s:s + 1, :] * d, axis=-1, keepdims=True)
                o = o + w * vh[s:s + 1, :]
            last = ch[chunk - 1:chunk, :]
            kt = kh * jnp.exp(last - ch)
            st_ref[h] = st * jnp.exp(last) + _dot_tn(vh.astype(BF16), kt.astype(BF16))
            rh = r_ref[0, rows, vcols]
            o_ref[0, rows, vcols] = (_rms(o, on_ref[...]) * _silu(rh)).astype(o_ref.dtype)
        return carry

    lax.fori_loop(0, nsub, step, 0)

    @pl.when(i == pl.num_programs(1) - 1)
    def _():
        for h in range(GLA_HEADS):
            sout_ref[0, h] = st_ref[h].T


def gla_mixer(z3, s0, w2p, b2, onorm, *, n_seq, tiles_per_seq, chunk, n_valid, out_dtype):
    tt = z3.shape[1]
    nsub = tt // chunk
    dqk = GLA_HEADS * GLA_DK
    dv = GLA_HEADS * GLA_DV
    tile = lambda b, i: b * tiles_per_seq + i
    return pl.pallas_call(
        functools.partial(_gla_kernel, chunk=chunk, n_valid=n_valid, nsub=nsub),
        grid=(n_seq, tiles_per_seq),
        in_specs=[pl.BlockSpec((1, tt, dqk), lambda b, i: (tile(b, i), 0, AB_GQ // dqk)),
                  pl.BlockSpec((1, tt, dqk), lambda b, i: (tile(b, i), 0, AB_GK // dqk)),
                  pl.BlockSpec((1, tt, dv), lambda b, i: (tile(b, i), 0, AB_GV // dv)),
                  pl.BlockSpec((1, tt, dv), lambda b, i: (tile(b, i), 0, AB_GR // dv)),
                  pl.BlockSpec((1, tt, LANES), lambda b, i: (tile(b, i), 0, AB_MISC0 // LANES)),
                  pl.BlockSpec((LANES, dqk), lambda b, i: (0, 0)),
                  pl.BlockSpec((1, dqk), lambda b, i: (0, 0)),
                  pl.BlockSpec((1, GLA_DV), lambda b, i: (0, 0)),
                  pl.BlockSpec((1, GLA_HEADS, GLA_DK, GLA_DV), lambda b, i: (b, 0, 0, 0))],
        out_specs=[pl.BlockSpec((1, tt, dv), lambda b, i: (tile(b, i), 0, 0)),
                   pl.BlockSpec((1, GLA_HEADS, GLA_DK, GLA_DV), lambda b, i: (b, 0, 0, 0))],
        out_shape=[jax.ShapeDtypeStruct((n_seq * tiles_per_seq, tt, dv), out_dtype),
                   jax.ShapeDtypeStruct((n_seq, GLA_HEADS, GLA_DK, GLA_DV), F32)],
        scratch_shapes=[pltpu.VMEM((GLA_HEADS, GLA_DV, GLA_DK), F32), pltpu.VMEM((tt, dqk), F32)],
        compiler_params=pltpu.CompilerParams(dimension_semantics=("parallel", "arbitrary")),
        name="gla_mixer",
    )(z3, z3, z3, z3, z3, w2p, b2.reshape(1, dqk), onorm.reshape(1, GLA_DV), s0)


def _masked_softmax_rows(s, mask):
    sm = jnp.where(mask[None], s, NEG)
    m = jnp.max(sm, axis=-1, keepdims=True)
    e = jnp.where(mask[None], jnp.exp(sm - m), 0.0)
    d = jnp.sum(e, axis=-1, keepdims=True)
    return e / jnp.where(d > 0, d, 1.0)


def _dot_exact_rhs(x, m):
    hi = x.astype(BF16)
    r1 = x - hi.astype(F32)
    mid = r1.astype(BF16)
    lo = (r1 - mid.astype(F32)).astype(BF16)
    return _dot(hi, m) + _dot(mid, m) + _dot(lo, m)


def _topk_mask(score, n_rounds):
    lane = lax.broadcasted_iota(jnp.int32, score.shape, 1).astype(F32)
    sel = jnp.zeros(score.shape, F32)
    work = score
    for _ in range(n_rounds):
        mx = jnp.max(work, axis=-1, keepdims=True)
        first = jnp.min(jnp.where(work == mx, lane, 1e9), axis=-1, keepdims=True)
        pick = lane == first
        sel = jnp.where(pick & (mx > 0.5 * NEG), 1.0, sel)
        work = jnp.where(pick, 2.0 * NEG, work)
    return sel


def _block_expand(n_blocks_pad, n_keys, block, key0=0):
    bi = lax.broadcasted_iota(jnp.int32, (n_blocks_pad, n_keys), 0)
    ki = lax.broadcasted_iota(jnp.int32, (n_blocks_pad, n_keys), 1) + key0
    return (bi == jnp.right_shift(ki, int(math.log2(block)))).astype(BF16)


def _gelu_tanh(x):
    return x * (0.5 * (1.0 + jnp.tanh(math.sqrt(2.0 / math.pi) * (x + 0.044715 * (x * x * x)))))


def _cmp_mlp(chunk_rows, n_ch, w1_ref, cpe_ref, cb1_ref, w2_ref, cb2_ref, kg_ref):
    hid_w = NSA_CMP_HIDDEN
    out = {}
    for e in range(2):
        accp = _dot(cpe_ref[e], w1_ref[e])
        pe = accp[0:1, :hid_w] + accp[1:2, hid_w:] + cb1_ref[e]
        x = jnp.concatenate([chunk_rows(e * NSA_KV + g) for g in range(NSA_KV)], axis=0).astype(BF16)
        acc = _dot(x, w1_ref[e])
        hid = []
        for g in range(NSA_KV):
            a = acc[g * n_ch:(g + 1) * n_ch]
            hid.append(_gelu_tanh(a[:, :hid_w] + pltpu.roll(a[:, hid_w:], n_ch - 1, 0) + pe))
        ckv = _dot(jnp.concatenate(hid, axis=0).astype(BF16), w2_ref[e]) + cb2_ref[e]
        for g in range(NSA_KV):
            out[e, g] = ckv[g * n_ch:(g + 1) * n_ch]
    kc = [_rms(out[0, g], kg_ref[...]) for g in range(NSA_KV)]
    vc = [out[1, g] for g in range(NSA_KV)]
    return kc, vc


def _nsa_q4(q, qg):
    return jnp.concatenate([_rms(q[:, r * HEAD_DIM:(r + 1) * HEAD_DIM], qg) * HEAD_DIM ** -0.5
                            for r in range(NSA_REP)], axis=0)


def _nsa_cmp_branch(q4, kc, vc, qpos, n_cmp):
    nq = qpos.shape[0]
    n_ch = kc.shape[0]
    sc = _dot_nt(q4, kc.astype(BF16)).reshape(NSA_REP, nq, n_ch)
    cidx = lax.broadcasted_iota(jnp.int32, (1, n_ch), 1)
    cmask = (NSA_CMP_STRIDE * cidx + NSA_CMP_LEN - 1 <= qpos) & (cidx < n_cmp)
    p = _masked_softmax_rows(sc, cmask)
    o = _dot(p.reshape(NSA_REP * nq, n_ch).astype(BF16), vc.astype(BF16))
    return o, jnp.sum(p, axis=0)


def _nsa_select(pcs, qpos0, n_sb):
    n_ch = pcs.shape[1]
    ratio = NSA_SEL_BLOCK // NSA_CMP_STRIDE
    bi = lax.broadcasted_iota(jnp.int32, (LANES, n_ch), 0)
    ci = lax.broadcasted_iota(jnp.int32, (LANES, n_ch), 1)
    mimp = ((ci >= ratio * bi - 1) & (ci <= ratio * bi + ratio - 1)).astype(BF16)
    hi = pcs.astype(BF16)
    r1 = pcs - hi.astype(F32)
    mid = r1.astype(BF16)
    lo = (r1 - mid.astype(F32)).astype(BF16)
    imp = _dot_nt(mimp, hi) + _dot_nt(mimp, mid) + _dot_nt(mimp, lo)
    nbp = -(-n_sb // SUBLANES) * SUBLANES
    blk = lax.broadcasted_iota(jnp.int32, (nbp, 1), 0)
    qpos = qpos0 + lax.broadcasted_iota(jnp.int32, (1, LANES), 1)
    cur = jnp.right_shift(qpos, int(math.log2(NSA_SEL_BLOCK)))
    valid = (blk <= cur) & (blk < n_sb)
    forced = (blk == 0) | (blk == cur) | (blk == cur - 1)
    score = jnp.where(forced, -NEG, jnp.where(valid, imp[:nbp], NEG))
    rank = jnp.zeros((nbp, LANES), F32)
    for k in range(n_sb):
        sk = score[k:k + 1, :]
        rank = rank + jnp.where((sk > score) | ((sk == score) & (blk > k)), 1.0, 0.0)
    sel_t = jnp.where((rank < min(NSA_N_SEL, n_sb)) & (score > 0.5 * NEG), 1.0, 0.0)
    sel_t = jnp.concatenate([sel_t, jnp.zeros((LANES - nbp, LANES), F32)], axis=0)
    return sel_t.T


def _nsa_window_mask(kpos, qpos):
    return (kpos <= qpos) & (qpos - kpos < NSA_WINDOW) & (kpos >= 0)


def _nsa_gate_mix(gsig, o_cmp, o_sel, o_win, nq):
    outs = []
    for r in range(NSA_REP):
        c0 = MISC_GATE_COL + 3 * r
        rows = slice(r * nq, (r + 1) * nq)
        outs.append(gsig[:, c0:c0 + 1] * o_cmp[rows] + gsig[:, c0 + 1:c0 + 2] * o_sel[rows]
                    + gsig[:, c0 + 2:c0 + 3] * o_win[rows])
    return jnp.concatenate(outs, axis=1)


def _nsa_cmp_kernel(*refs, n_ch):
    x_refs = refs[:NSA_CMP_STRIDE]
    w1_ref, cpe_ref, cb1_ref, w2_ref, cb2_ref, kg_ref, kc_ref, vc_ref = refs[NSA_CMP_STRIDE:]

    def chunk_rows(eg):
        return jnp.concatenate([x_refs[s][:, eg * HEAD_DIM:(eg + 1) * HEAD_DIM] for s in range(NSA_CMP_STRIDE)], axis=1)

    kc, vc = _cmp_mlp(chunk_rows, n_ch, w1_ref, cpe_ref, cb1_ref, w2_ref, cb2_ref, kg_ref)
    for g in range(NSA_KV):
        kc_ref[0, g] = kc[g]
        vc_ref[0, g] = vc[g]


def _cmp_weight_specs():
    z = (0,) * 8
    full = lambda shape: pl.BlockSpec(shape, lambda *a: z[:len(shape)])
    taps = NSA_CMP_STRIDE * HEAD_DIM
    return [full((2, taps, 2 * NSA_CMP_HIDDEN)), full((2, SUBLANES, taps)),
            full((2, 1, NSA_CMP_HIDDEN)), full((2, NSA_CMP_HIDDEN, HEAD_DIM)), full((2, 1, HEAD_DIM)),
            full((1, HEAD_DIM))]


def nsa_cmp_prompt(z, cmpw, *, batch, seq):
    n_ch = seq // NSA_CMP_STRIDE
    seg = 4 * HEAD_DIM
    st = NSA_CMP_STRIDE
    width = z.shape[1]
    zp = z if z.shape[0] % st == 0 else z[:batch * seq]
    zc = zp.reshape(zp.shape[0] // st, st * width)
    shp = jax.ShapeDtypeStruct((batch, NSA_KV, n_ch, HEAD_DIM), F32)
    ospec = pl.BlockSpec((1, NSA_KV, n_ch, HEAD_DIM), lambda b: (b, 0, 0, 0))
    xspec = lambda s: pl.BlockSpec((n_ch, seg), lambda b: (b, (s * width + AB_CMP) // seg))
    return pl.pallas_call(
        functools.partial(_nsa_cmp_kernel, n_ch=n_ch),
        grid=(batch,),
        in_specs=[xspec(s) for s in range(st)] + _cmp_weight_specs(),
        out_specs=[ospec, ospec],
        out_shape=[shp, shp],
        compiler_params=pltpu.CompilerParams(dimension_semantics=("parallel",)),
        name="nsa_cmp_prompt",
    )(*([zc] * st), *cmpw)


def _nsa_prep_kernel(sel_ref, win_ref, kg_ref, o_ref):
    d = HEAD_DIM
    parts = []
    for ref, row in ((sel_ref, 1), (win_ref, 2)):
        x = ref[...]
        for g in range(NSA_KV):
            parts.append(_rms(x[:, g * d:(g + 1) * d], kg_ref[row:row + 1, :]))
        parts.append(x[:, NSA_KV * d:])
    o_ref[...] = jnp.concatenate(parts, axis=1).astype(BF16)


def nsa_prep_prompt(z, k_gain, *, rows, tm):
    seg = 4 * HEAD_DIM
    return pl.pallas_call(
        _nsa_prep_kernel,
        grid=(rows // tm,),
        in_specs=[pl.BlockSpec((tm, seg), lambda i: (i, AB_SEL // seg)),
                  pl.BlockSpec((tm, seg), lambda i: (i, AB_WIN // seg)),
                  pl.BlockSpec((3, HEAD_DIM), lambda i: (0, 0))],
        out_specs=pl.BlockSpec((tm, 2 * seg), lambda i: (i, 0)),
        out_shape=jax.ShapeDtypeStruct((rows, 2 * seg), BF16),
        compiler_params=pltpu.CompilerParams(dimension_semantics=("parallel",)),
        name="nsa_prep_prompt",
    )(z, z, k_gain)


def _nsa_attn_kernel(q_ref, m_ref, kc_ref, vc_ref, ks_ref, vs_ref, kw_ref, vw_ref, oh_ref, qg_ref, o_ref, *, seq, kt):
    i = pl.program_id(2)
    nq = q_ref.shape[0]
    assert nq == LANES
    q4 = _nsa_q4(q_ref[...], qg_ref[...]).astype(BF16)
    qpos = i * nq + lax.broadcasted_iota(jnp.int32, (nq, 1), 0)
    n_ch = seq // NSA_CMP_STRIDE
    n_sb = -(-seq // NSA_SEL_BLOCK)

    o_cmp, pcs = _nsa_cmp_branch(q4, kc_ref[0, 0], vc_ref[0, 0], qpos, n_ch - 1)
    sel = _nsa_select(pcs, i * nq, n_sb)
    bias = jnp.where(sel > 0.5, 0.0, MASK_BIAS).astype(BF16)
    q_aug = jnp.concatenate([q4, jnp.concatenate([bias] * NSA_REP, axis=0)], axis=1)

    def sel_step(j, carry, causal):
        m, l, acc = carry
        rows = pl.ds(pl.multiple_of(j * kt, kt), kt)
        k_aug = jnp.concatenate([ks_ref[rows, :], oh_ref[rows, :]], axis=1)
        s = _dot_nt(q_aug, k_aug).reshape(NSA_REP, nq, kt)
        if causal:
            kpos = j * kt + lax.broadcasted_iota(jnp.int32, (1, kt), 1)
            s = jnp.where((kpos <= qpos)[None], s, NEG)
        mn = jnp.maximum(m, jnp.max(s, axis=-1, keepdims=True))
        p = jnp.exp(s - mn)
        alpha = jnp.exp(m - mn)
        l = alpha * l + jnp.sum(p, axis=-1, keepdims=True)
        pv = _dot(p.reshape(NSA_REP * nq, kt).astype(BF16), vs_ref[rows, :])
        return mn, l, alpha * acc + pv.reshape(NSA_REP, nq, HEAD_DIM)

    last = (i * nq + nq - 1) // kt
    init = (jnp.full((NSA_REP, nq, 1), NEG, F32), jnp.zeros((NSA_REP, nq, 1), F32),
            jnp.zeros((NSA_REP, nq, HEAD_DIM), F32))
    carry = lax.fori_loop(0, last, functools.partial(sel_step, causal=False), init)
    _, l, acc = sel_step(last, carry, True)
    o_sel = (acc / l).reshape(NSA_REP * nq, HEAD_DIM)

    span = min(NSA_WINDOW + nq, seq)
    start = pl.multiple_of(jnp.clip(i * nq - NSA_WINDOW, 0, seq - span), nq)
    s = _dot_nt(q4, kw_ref[pl.ds(start, span), :]).reshape(NSA_REP, nq, span)
    kpos = start + lax.broadcasted_iota(jnp.int32, (1, span), 1)
    p = _masked_softmax_rows(s, _nsa_window_mask(kpos, qpos))
    o_win = _dot(p.reshape(NSA_REP * nq, span).astype(BF16), vw_ref[pl.ds(start, span), :])

    o_ref[...] = _nsa_gate_mix(jax.nn.sigmoid(m_ref[...]), o_cmp, o_sel, o_win, nq).astype(o_ref.dtype)


def nsa_attn_prompt(z, kvp, kc, vc, q_gain, *, batch, seq, tq):
    nt = seq // tq
    gw = NSA_REP * HEAD_DIM
    kt = min(4 * tq, seq)
    row = lambda b, g, i: b * nt + i
    cspec = pl.BlockSpec((1, 1, seq // NSA_CMP_STRIDE, HEAD_DIM), lambda b, g, i: (b, g, 0, 0))
    kvspec = lambda c: pl.BlockSpec((seq, HEAD_DIM), lambda b, g, i: (b, c + g))
    return pl.pallas_call(
        functools.partial(_nsa_attn_kernel, seq=seq, kt=kt),
        grid=(batch, NSA_KV, nt),
        in_specs=[pl.BlockSpec((tq, gw), lambda b, g, i: (row(b, g, i), AB_NQ // gw + g)),
                  pl.BlockSpec((tq, LANES), lambda b, g, i: (row(b, g, i), AB_MISC0 // LANES + g)),
                  cspec, cspec, kvspec(0), kvspec(2), kvspec(4), kvspec(6),
                  pl.BlockSpec((seq, LANES), lambda b, g, i: (0, 0)),
                  pl.BlockSpec((1, HEAD_DIM), lambda b, g, i: (0, 0))],
        out_specs=pl.BlockSpec((tq, gw), lambda b, g, i: (row(b, g, i), g)),
        out_shape=jax.ShapeDtypeStruct((batch * seq, NSA_KV * gw), BF16),
        compiler_params=pltpu.CompilerParams(dimension_semantics=("parallel", "parallel", "arbitrary")),
        name="nsa_attn_prompt",
    )(z, z, kc, vc, kvp, kvp, kvp, kvp, _block_onehot(seq, NSA_SEL_BLOCK), q_gain.reshape(1, HEAD_DIM))


def _block_onehot(n_keys, block):
    k = lax.broadcasted_iota(jnp.int32, (n_keys, LANES), 0) // block
    return (k == lax.broadcasted_iota(jnp.int32, (n_keys, LANES), 1)).astype(BF16)


def nsa_cmp_weights(cw1, cb1, cw2, cb2, cpe, k_gain):
    st = NSA_CMP_STRIDE
    taps = st * HEAD_DIM
    w1cat = jnp.concatenate([cw1[:, :st], cw1[:, st:]], axis=-1).reshape(2, taps, 2 * NSA_CMP_HIDDEN).astype(BF16)
    pe_rows = jnp.stack([cpe[:, :st].reshape(2, taps), cpe[:, st:].reshape(2, taps)], axis=1)
    pe_rows = jnp.pad(pe_rows, ((0, 0), (0, SUBLANES - 2), (0, 0))).astype(BF16)
    return (w1cat, pe_rows, cb1.reshape(2, 1, NSA_CMP_HIDDEN), cw2.astype(BF16), cb2.reshape(2, 1, HEAD_DIM),
            k_gain[0].reshape(1, HEAD_DIM))


NSA_ROW = 4 * NSA_KV * HEAD_DIM
CHUNKS_PER_PAGE = PAGE_SIZE // NSA_CMP_STRIDE


def _nsa_sample_kernel(pt_ref, *refs, n_pages, past, t_new):
    del pt_ref
    pages = refs[:n_pages]
    (q_ref, sel_ref, win_ref, m0_ref, m1_ref, wb_ref, w1_ref, cpe_ref, cb1_ref, w2_ref, cb2_ref, kg0_ref,
     kg_ref, qg_ref, o_ref) = refs[n_pages:]
    nq = SAMPLE_ROWS
    d = HEAD_DIM
    n_ch = n_pages * CHUNKS_PER_PAGE
    n_cmp = (past + t_new) // NSA_CMP_STRIDE - 1
    n_sb = -(-(past + t_new) // NSA_SEL_BLOCK)

    def piece(p, s, col):
        off = s * NSA_ROW + col * d
        return pages[p][0, :, off:off + d]

    def chunk_rows(eg):
        return jnp.concatenate(
            [jnp.concatenate([piece(p, s, eg) for s in range(NSA_CMP_STRIDE)], axis=1) for p in range(n_pages)], axis=0)

    def past_rows(col):
        return jnp.concatenate([piece(p, s, col) for p in range(n_pages) for s in range(NSA_CMP_STRIDE)], axis=0)

    kc, vc = _cmp_mlp(chunk_rows, n_ch, w1_ref, cpe_ref, cb1_ref, w2_ref, cb2_ref, kg0_ref)

    qpos = past + lax.broadcasted_iota(jnp.int32, (nq, 1), 0)
    nk = past + LANES
    idx = lax.broadcasted_iota(jnp.int32, (1, nk), 1)
    kpos = jnp.where(idx < past,
                     (idx & ~(PAGE_SIZE - 1)) + NSA_CMP_STRIDE * (idx & (CHUNKS_PER_PAGE - 1))
                     + (jnp.right_shift(idx, 3) & (NSA_CMP_STRIDE - 1)), idx)
    bi = lax.broadcasted_iota(jnp.int32, (LANES, nk), 0)
    esel = (bi == jnp.right_shift(kpos, int(math.log2(NSA_SEL_BLOCK)))).astype(BF16)
    causal = kpos <= qpos
    nw = NSA_WINDOW + LANES
    wmask = _nsa_window_mask(past - NSA_WINDOW + lax.broadcasted_iota(jnp.int32, (1, nw), 1), qpos)
    zpad = jnp.zeros((LANES - nq, d), F32)
    q = q_ref[0]
    sel_new = sel_ref[0]
    win_new = win_ref[0]
    wb = wb_ref[0]
    kg_sel = kg_ref[1:2, :]
    kg_win = kg_ref[2:3, :]
    col = lambda x, c: x[:, c * d:(c + 1) * d]

    for g in range(NSA_KV):
        q4 = _nsa_q4(q[:, g * NSA_REP * d:(g + 1) * NSA_REP * d], qg_ref[...]).astype(BF16)
        o_cmp, pcs = _nsa_cmp_branch(q4, kc[g], vc[g], qpos, n_cmp)
        pcs = jnp.concatenate([pcs, jnp.zeros((LANES - nq, n_ch), F32)], axis=0)
        selb = _nsa_select(pcs, past, n_sb)[:nq].astype(BF16)

        ks = jnp.concatenate([_rms(past_rows(2 * NSA_KV + g), kg_sel), _rms(col(sel_new, g), kg_sel), zpad], axis=0)
        vs = jnp.concatenate([past_rows(3 * NSA_KV + g), col(sel_new, NSA_KV + g), zpad], axis=0)
        s = _dot_nt(q4, ks.astype(BF16)).reshape(NSA_REP, nq, nk)
        p = _masked_softmax_rows(s, (_dot(selb, esel) > 0.5) & causal)
        o_sel = _dot(p.reshape(NSA_REP * nq, nk).astype(BF16), vs.astype(BF16))

        kw = jnp.concatenate([_rms(col(wb, g), kg_win), _rms(col(win_new, g), kg_win), zpad], axis=0)
        vw = jnp.concatenate([col(wb, NSA_KV + g), col(win_new, NSA_KV + g), zpad], axis=0)
        s = _dot_nt(q4, kw.astype(BF16)).reshape(NSA_REP, nq, nw)
        p = _masked_softmax_rows(s, wmask)
        o_win = _dot(p.reshape(NSA_REP * nq, nw).astype(BF16), vw.astype(BF16))

        gsig = jax.nn.sigmoid((m0_ref, m1_ref)[g][0])
        o_ref[0, :, g * NSA_REP * d:(g + 1) * NSA_REP * d] = _nsa_gate_mix(gsig, o_cmp, o_sel, o_win, nq).astype(o_ref.dtype)


def nsa_sample(zs, page_table, cache_kv, cache_win, cmpw, k_gain, q_gain, *, t_new):
    db, n_pages = page_table.shape
    past = n_pages * PAGE_SIZE
    assert cache_win.shape[1] == NSA_WINDOW and (past + t_new) // NSA_CMP_STRIDE == n_pages * CHUNKS_PER_PAGE
    n_phys = cache_kv.shape[0]
    page_w = NSA_CMP_STRIDE * NSA_ROW
    pages = cache_kv.reshape(n_phys, CHUNKS_PER_PAGE, page_w)
    wb = cache_win.reshape(db, NSA_WINDOW, 2 * NSA_KV * HEAD_DIM)
    seg = 4 * HEAD_DIM
    qw = NSA_HEADS * HEAD_DIM
    zspec = lambda w, off: pl.BlockSpec((1, SAMPLE_ROWS, w), lambda b, pt: (b, 0, off // w))
    page_spec = lambda p: pl.BlockSpec((1, CHUNKS_PER_PAGE, page_w), lambda b, pt: (pt[b, p], 0, 0))
    grid_spec = pltpu.PrefetchScalarGridSpec(
        num_scalar_prefetch=1,
        grid=(db,),
        in_specs=[page_spec(p) for p in range(n_pages)]
        + [zspec(qw, AB_NQ), zspec(seg, AB_SEL), zspec(seg, AB_WIN), zspec(LANES, AB_MISC0), zspec(LANES, AB_MISC1),
           pl.BlockSpec((1, NSA_WINDOW, seg), lambda b, pt: (b, 0, 0))]
        + _cmp_weight_specs()
        + [pl.BlockSpec((3, HEAD_DIM), lambda b, pt: (0, 0)), pl.BlockSpec((1, HEAD_DIM), lambda b, pt: (0, 0))],
        out_specs=pl.BlockSpec((1, SAMPLE_ROWS, qw), lambda b, pt: (b, 0, 0)),
    )
    return pl.pallas_call(
        functools.partial(_nsa_sample_kernel, n_pages=n_pages, past=past, t_new=t_new),
        grid_spec=grid_spec,
        out_shape=jax.ShapeDtypeStruct((db, SAMPLE_ROWS, qw), BF16),
        compiler_params=pltpu.CompilerParams(dimension_semantics=("parallel",)),
        name="nsa_sample",
    )(page_table, *([pages] * n_pages), zs, zs, zs, zs, zs, wb, *cmpw, k_gain, q_gain.reshape(1, HEAD_DIM))


C_Q, C_K, C_V, C_TOTAL = 0, MOBA_HEADS * HEAD_DIM, (MOBA_HEADS + MOBA_KV) * HEAD_DIM, (MOBA_HEADS + 2 * MOBA_KV) * HEAD_DIM


def _masked_softmax2d(s, mask):
    sm = jnp.where(mask, s, NEG)
    m = jnp.max(sm, axis=-1, keepdims=True)
    e = jnp.where(mask, jnp.exp(sm - m), 0.0)
    dsum = jnp.sum(e, axis=-1, keepdims=True)
    return e / jnp.where(dsum > 0, dsum, 1.0)


def _moba_q4(q, qg):
    return jnp.concatenate([_rms(q[:, r * HEAD_DIM:(r + 1) * HEAD_DIM], qg) * HEAD_DIM ** -0.5
                            for r in range(MOBA_REP)], axis=0)


def _moba_select(q4, kmean_pad, cur, n_blocks):
    gs = _dot_nt(q4, kmean_pad.astype(BF16))
    blk = lax.broadcasted_iota(jnp.int32, (1, LANES), 1)
    score = jnp.where((blk < cur) & (blk < n_blocks), gs, NEG)
    sel = _topk_mask(score, min(MOBA_TOPK, n_blocks))
    return jnp.where(blk == cur, 1.0, sel)


def _moba_prep_kernel(k_ref, v_ref, kg_ref, o_ref, km_ref):
    k = k_ref[...]
    kn = jnp.concatenate([_rms(k[:, h * HEAD_DIM:(h + 1) * HEAD_DIM], kg_ref[...]) for h in range(MOBA_KV)], axis=1)
    km_ref[0] = jnp.mean(kn, axis=0, keepdims=True)
    o_ref[...] = jnp.concatenate([kn, v_ref[...]], axis=1).astype(BF16)


def moba_prep_prompt(zc, k_gain, *, rows):
    w = MOBA_KV * HEAD_DIM
    nblk = rows // MOBA_BLOCK
    return pl.pallas_call(
        _moba_prep_kernel,
        grid=(nblk,),
        in_specs=[pl.BlockSpec((MOBA_BLOCK, w), lambda i: (i, C_K // w)),
                  pl.BlockSpec((MOBA_BLOCK, w), lambda i: (i, C_V // w)),
                  pl.BlockSpec((1, HEAD_DIM), lambda i: (0, 0))],
        out_specs=[pl.BlockSpec((MOBA_BLOCK, 2 * w), lambda i: (i, 0)),
                   pl.BlockSpec((1, 1, w), lambda i: (i, 0, 0))],
        out_shape=[jax.ShapeDtypeStruct((rows, 2 * w), BF16), jax.ShapeDtypeStruct((nblk, 1, w), F32)],
        compiler_params=pltpu.CompilerParams(dimension_semantics=("parallel",)),
        name="moba_prep_prompt",
    )(zc, zc, k_gain.reshape(1, HEAD_DIM))


def _moba_attn_kernel(q_ref, km_ref, k_ref, v_ref, oh_ref, qg_ref, o_ref, *, seq):
    i = pl.program_id(2)
    nq = q_ref.shape[0]
    assert nq == MOBA_BLOCK
    nb = seq // MOBA_BLOCK
    rq = MOBA_REP * nq
    q4 = _moba_q4(q_ref[...], qg_ref[...]).astype(BF16)
    km = jnp.concatenate([km_ref[0], jnp.zeros((LANES - nb, HEAD_DIM), F32)], axis=0)
    sel = _moba_select(q4, km, i, nb)
    q_aug = jnp.concatenate([q4, jnp.where(sel > 0.5, 0.0, MASK_BIAS).astype(BF16)], axis=1)

    def update(carry, s, rows):
        m, l, acc = carry
        mn = jnp.maximum(m, jnp.max(s, axis=-1, keepdims=True))
        p = jnp.exp(s - mn)
        alpha = jnp.exp(m - mn)
        l = alpha * l + jnp.sum(p, axis=-1, keepdims=True)
        return mn, l, alpha * acc + _dot(p.astype(BF16), v_ref[rows, :])

    def past_step(n, carry):
        rows = pl.ds(pl.multiple_of(n * MOBA_BLOCK, MOBA_BLOCK), MOBA_BLOCK)
        k_aug = jnp.concatenate([k_ref[rows, :], oh_ref[rows, :]], axis=1)
        return update(carry, _dot_nt(q_aug, k_aug), rows)

    init = (jnp.full((rq, 1), NEG, F32), jnp.zeros((rq, 1), F32), jnp.zeros((rq, HEAD_DIM), F32))
    carry = lax.fori_loop(0, i, past_step, init)
    rows = pl.ds(pl.multiple_of(i * MOBA_BLOCK, MOBA_BLOCK), MOBA_BLOCK)
    t = lax.broadcasted_iota(jnp.int32, (nq, 1), 0)
    causal = lax.broadcasted_iota(jnp.int32, (1, MOBA_BLOCK), 1) <= jnp.concatenate([t] * MOBA_REP, axis=0)
    _, l, acc = update(carry, jnp.where(causal, _dot_nt(q4, k_ref[rows, :]), NEG), rows)
    o = acc / l
    o_ref[...] = jnp.concatenate([o[r * nq:(r + 1) * nq] for r in range(MOBA_REP)], axis=1).astype(o_ref.dtype)


def moba_attn_prompt(zc, kvp, kmean, q_gain, *, batch, seq):
    nt = seq // MOBA_BLOCK
    gw = MOBA_REP * HEAD_DIM
    km = kmean.reshape(batch, nt, MOBA_KV * HEAD_DIM)
    return pl.pallas_call(
        functools.partial(_moba_attn_kernel, seq=seq),
        grid=(batch, MOBA_KV, nt),
        in_specs=[pl.BlockSpec((MOBA_BLOCK, gw), lambda b, g, i: (b * nt + i, g)),
                  pl.BlockSpec((1, nt, HEAD_DIM), lambda b, g, i: (b, 0, g)),
                  pl.BlockSpec((seq, HEAD_DIM), lambda b, g, i: (b, g)),
                  pl.BlockSpec((seq, HEAD_DIM), lambda b, g, i: (b, MOBA_KV + g)),
                  pl.BlockSpec((seq, LANES), lambda b, g, i: (0, 0)),
                  pl.BlockSpec((1, HEAD_DIM), lambda b, g, i: (0, 0))],
        out_specs=pl.BlockSpec((MOBA_BLOCK, gw), lambda b, g, i: (b * nt + i, g)),
        out_shape=jax.ShapeDtypeStruct((batch * seq, MOBA_HEADS * HEAD_DIM), BF16),
        compiler_params=pltpu.CompilerParams(dimension_semantics=("parallel", "parallel", "arbitrary")),
        name="moba_attn_prompt",
    )(zc, km, kvp, kvp, _block_onehot(seq, MOBA_BLOCK), q_gain.reshape(1, HEAD_DIM))


def _moba_sample_kernel(pt_ref, *refs, n_pages, past, t_new):
    del pt_ref
    pages = refs[:n_pages]
    q_ref, kn_ref, vn_ref, kg_ref, qg_ref, o_ref = refs[n_pages:]
    nq = SAMPLE_ROWS
    d = HEAD_DIM
    rq = MOBA_REP * nq
    nb = -(-(past + t_new) // MOBA_BLOCK)
    n_past_blocks = past // MOBA_BLOCK
    nk = past + LANES
    qpos = past + lax.broadcasted_iota(jnp.int32, (nq, 1), 0)
    qpos4 = jnp.concatenate([qpos] * MOBA_REP, axis=0)
    cur = jnp.right_shift(qpos4, int(math.log2(MOBA_BLOCK)))
    kpos = lax.broadcasted_iota(jnp.int32, (1, nk), 1)
    expand = _block_expand(LANES, nk, MOBA_BLOCK)
    causal = kpos <= qpos4
    zpad = jnp.zeros((LANES - nq, d), F32)
    q = q_ref[0]
    k_new = kn_ref[0]
    v_new = vn_ref[0]
    for g in range(MOBA_KV):
        kn = _rms(jnp.concatenate([pg[0, :, g * d:(g + 1) * d] for pg in pages], axis=0), kg_ref[...])
        km = jnp.concatenate([jnp.mean(kn[n * MOBA_BLOCK:(n + 1) * MOBA_BLOCK], axis=0, keepdims=True)
                              for n in range(n_past_blocks)] + [jnp.zeros((LANES - n_past_blocks, d), F32)], axis=0)
        q4 = _moba_q4(q[:, g * MOBA_REP * d:(g + 1) * MOBA_REP * d], qg_ref[...]).astype(BF16)
        sel = _moba_select(q4, km, cur, nb)
        keys = jnp.concatenate([kn, _rms(k_new[:, g * d:(g + 1) * d], kg_ref[...]), zpad], axis=0)
        vals = jnp.concatenate([pg[0, :, (MOBA_KV + g) * d:(MOBA_KV + g + 1) * d] for pg in pages]
                               + [v_new[:, g * d:(g + 1) * d], zpad], axis=0)
        s = _dot_nt(q4, keys.astype(BF16))
        p = _masked_softmax2d(s, (_dot(sel.astype(BF16), expand) > 0.5) & causal)
        o = _dot(p.astype(BF16), vals.astype(BF16))
        for r in range(MOBA_REP):
            h = g * MOBA_REP + r
            o_ref[0, :, h * d:(h + 1) * d] = o[r * nq:(r + 1) * nq].astype(o_ref.dtype)


def moba_sample(zcs, page_table, cache_kv, k_gain, q_gain, *, t_new):
    db, n_pages = page_table.shape
    past = n_pages * PAGE_SIZE
    assert past % MOBA_BLOCK == 0 and t_new <= MOBA_BLOCK
    n_phys = cache_kv.shape[0]
    w = MOBA_KV * HEAD_DIM
    pages = cache_kv.reshape(n_phys, PAGE_SIZE, 2 * w)
    qw = MOBA_HEADS * HEAD_DIM
    zspec = lambda wd, off: pl.BlockSpec((1, SAMPLE_ROWS, wd), lambda b, pt: (b, 0, off // wd))
    page_spec = lambda p: pl.BlockSpec((1, PAGE_SIZE, 2 * w), lambda b, pt: (pt[b, p], 0, 0))
    grid_spec = pltpu.PrefetchScalarGridSpec(
        num_scalar_prefetch=1,
        grid=(db,),
        in_specs=[page_spec(p) for p in range(n_pages)]
        + [zspec(qw, C_Q), zspec(w, C_K), zspec(w, C_V),
           pl.BlockSpec((1, HEAD_DIM), lambda b, pt: (0, 0)), pl.BlockSpec((1, HEAD_DIM), lambda b, pt: (0, 0))],
        out_specs=pl.BlockSpec((1, SAMPLE_ROWS, qw), lambda b, pt: (b, 0, 0)),
    )
    return pl.pallas_call(
        functools.partial(_moba_sample_kernel, n_pages=n_pages, past=past, t_new=t_new),
        grid_spec=grid_spec,
        out_shape=jax.ShapeDtypeStruct((db, SAMPLE_ROWS, qw), BF16),
        compiler_params=pltpu.CompilerParams(dimension_semantics=("parallel",)),
        name="moba_sample",
    )(page_table, *([pages] * n_pages), zcs, zcs, zcs, k_gain.reshape(1, HEAD_DIM), q_gain.reshape(1, HEAD_DIM))


ROW_TILE = 512
GLA_TILE = 128
NSA_Q_TILE = 128


def _ab_weight_layout(w_in_ab):
    d = w_in_ab.shape[0]
    widths = (GLA_HEADS * GLA_DK, GLA_HEADS * GLA_DK, GLA_HEADS * GLA_DV, GLA_HEADS * GLA_DV, GLA_GATE_RANK,
              NSA_HEADS * HEAD_DIM, NSA_HEADS * 3, 6 * NSA_KV * HEAD_DIM)
    offs = [0]
    for w in widths:
        offs.append(offs[-1] + w)
    gq, gk, gv, gr, ga, nq, ngt, nkv = (w_in_ab[:, offs[i]:offs[i + 1]] for i in range(len(widths)))
    half = NSA_REP * 3
    zeros = lambda n: jnp.zeros((d, n), w_in_ab.dtype)
    misc0 = jnp.concatenate([ga, ngt[:, :half], zeros(LANES - GLA_GATE_RANK - half)], axis=1)
    misc1 = jnp.concatenate([zeros(GLA_GATE_RANK), ngt[:, half:], zeros(LANES - GLA_GATE_RANK - half)], axis=1)
    w = jnp.concatenate([gq, gk, gv, gr, nq, nkv, misc0, misc1, zeros(AB_TOTAL - AB_MISC1 - LANES)], axis=1)
    return w.astype(BF16)


def _pad_sample_rows(z, db, t_new):
    return jnp.pad(z.reshape(db, t_new, z.shape[-1]), ((0, 0), (0, SAMPLE_ROWS - t_new), (0, 0)))


def _conv_ffn(h, l, n_p, batch, seq, t_new, state_ffn_conv, norm_ffn, ffn_w_up, ffn_conv_w, ffn_conv_b, ffn_w_down):
    d_ff = ffn_conv_w.shape[-1]
    db = state_ffn_conv.shape[1]
    u = norm_matmul(h, norm_ffn[l], ffn_w_up[l].astype(BF16), tm=ROW_TILE, tn=1024)
    act = jnp.concatenate([
        ffn_act_prompt(u, ffn_conv_w[l], ffn_conv_b[l], batch=batch, seq=seq, tm=512, tn=d_ff // 4),
        ffn_act_sample(u, n_p, state_ffn_conv[l], ffn_conv_w[l], ffn_conv_b[l], t_new=t_new, tn=d_ff // 4)], axis=0)
    h = matmul_residual(act, ffn_w_down[l].astype(BF16), h, tm=ROW_TILE, tn=min(1024, h.shape[1]), tk=d_ff // 2)
    keep = FFN_CONV - 1
    gate_p = u[:n_p, :d_ff].reshape(batch, seq, d_ff)[:, seq - keep:]
    gate_s = u[n_p:, :d_ff].reshape(db, t_new, d_ff)[:, t_new - keep:]
    return h, gate_p, gate_s


def kernel(x_prompt, x_sample, page_table, cache_nsa_kv, cache_nsa_win, state_gla, cache_moba_kv, state_ffn_conv, norm_mix, w_in_ab, gla_a_w2, gla_a_b, gla_o_norm, nsa_q_norm, nsa_k_norm, nsa_cmp_w1, nsa_cmp_b1, nsa_cmp_w2, nsa_cmp_b2, nsa_cmp_pe, w_out_ab, w_in_c, moba_q_norm, moba_k_norm, w_out_c, norm_ffn, ffn_w_up, ffn_conv_w, ffn_conv_b, ffn_w_down):
    batch, seq, d_model = x_prompt.shape
    db, t_new, _ = x_sample.shape
    n_p, n_s = batch * seq, db * t_new
    n = n_p + n_s
    assert norm_mix.shape[0] == 2 and w_in_ab.shape[0] == 1 and w_in_c.shape[0] == 1
    assert FFN_CONV - 1 <= t_new <= SAMPLE_ROWS and n % ROW_TILE == 0 and n_p % ROW_TILE == 0
    ffn_args = (state_ffn_conv, norm_ffn, ffn_w_up, ffn_conv_w, ffn_conv_b, ffn_w_down)

    h = jnp.concatenate([x_prompt.reshape(n_p, d_model), x_sample.reshape(n_s, d_model)], axis=0)
    nsa_cache = cache_nsa_kv.reshape(cache_nsa_kv.shape[1:])
    nsa_win = cache_nsa_win.reshape(cache_nsa_win.shape[1:])
    gla_state = state_gla.reshape(state_gla.shape[1:])
    moba_cache = cache_moba_kv.reshape(cache_moba_kv.shape[1:])
    tn_out = min(1024, d_model)

    z = norm_matmul(h, norm_mix[0], _ab_weight_layout(w_in_ab[0]), tm=ROW_TILE, tn=1024)
    zs = _pad_sample_rows(z[n_p:], db, t_new)
    w2p = jnp.pad(gla_a_w2[0], ((0, LANES - GLA_GATE_RANK), (0, 0))).astype(BF16)
    gla_zero = jnp.zeros((batch,) + state_gla.shape[2:], F32)
    zg = z if n % GLA_TILE == 0 else z[:n_p]
    og_p, gla_p = gla_mixer(zg.reshape(-1, GLA_TILE, AB_TOTAL), gla_zero, w2p, gla_a_b[0], gla_o_norm[0],
                            n_seq=batch, tiles_per_seq=seq // GLA_TILE, chunk=GLA_CHUNK, n_valid=GLA_CHUNK, out_dtype=BF16)
    og_s, gla_s = gla_mixer(zs, gla_state, w2p, gla_a_b[0], gla_o_norm[0],
                            n_seq=db, tiles_per_seq=1, chunk=SAMPLE_ROWS, n_valid=t_new, out_dtype=BF16)
    cmpw = nsa_cmp_weights(nsa_cmp_w1[0], nsa_cmp_b1[0], nsa_cmp_w2[0], nsa_cmp_b2[0], nsa_cmp_pe[0], nsa_k_norm[0])
    kc, vc = nsa_cmp_prompt(z, cmpw, batch=batch, seq=seq)
    kvp = nsa_prep_prompt(z, nsa_k_norm[0], rows=n_p, tm=ROW_TILE)
    on_p = nsa_attn_prompt(z, kvp, kc, vc, nsa_q_norm[0], batch=batch, seq=seq, tq=NSA_Q_TILE)
    on_s = nsa_sample(zs, page_table, nsa_cache, nsa_win, cmpw, nsa_k_norm[0], nsa_q_norm[0], t_new=t_new)
    mix_p = jnp.concatenate([og_p.reshape(n_p, -1), on_p], axis=1)
    mix_s = jnp.concatenate([og_s[:, :t_new].reshape(n_s, -1), on_s[:, :t_new].reshape(n_s, -1)], axis=1)
    h = matmul_residual(jnp.concatenate([mix_p, mix_s], axis=0), w_out_ab[0].astype(BF16), h,
                        tm=ROW_TILE, tn=tn_out, tk=mix_p.shape[1])
    h, conv_p0, conv_s0 = _conv_ffn(h, 0, n_p, batch, seq, t_new, *ffn_args)

    kv_w = 4 * NSA_KV * HEAD_DIM
    win_w = 2 * NSA_KV * HEAD_DIM
    win_keep = min(NSA_WINDOW, seq)
    nsa_kv_p = z[:n_p, AB_CMP:AB_CMP + kv_w].reshape(1, batch, seq, 4, NSA_KV, HEAD_DIM)
    nsa_kv_s = z[n_p:, AB_CMP:AB_CMP + kv_w].reshape(1, db, t_new, 4, NSA_KV, HEAD_DIM)
    nsa_win_p = z[:n_p, AB_WIN:AB_WIN + win_w].reshape(batch, seq, win_w)[:, seq - win_keep:]
    nsa_win_p = nsa_win_p.reshape(1, batch, win_keep, 2, NSA_KV, HEAD_DIM)
    win_new = z[n_p:, AB_WIN:AB_WIN + win_w].reshape(db, t_new, 2, NSA_KV, HEAD_DIM)
    nsa_win_s = jnp.concatenate([nsa_win, win_new], axis=1)[None, :, -NSA_WINDOW:]

    zc = norm_matmul(h, norm_mix[1], w_in_c[0].astype(BF16), tm=ROW_TILE, tn=1024)
    zcs = _pad_sample_rows(zc[n_p:], db, t_new)
    kvm, kmean = moba_prep_prompt(zc, moba_k_norm[0], rows=n_p)
    om_p = moba_attn_prompt(zc, kvm, kmean, moba_q_norm[0], batch=batch, seq=seq)
    om_s = moba_sample(zcs, page_table, moba_cache, moba_k_norm[0], moba_q_norm[0], t_new=t_new)
    om = jnp.concatenate([om_p, om_s[:, :t_new].reshape(n_s, -1)], axis=0)
    h = matmul_residual(om, w_out_c[0].astype(BF16), h, tm=ROW_TILE, tn=tn_out, tk=om.shape[1])
    h, conv_p1, conv_s1 = _conv_ffn(h, 1, n_p, batch, seq, t_new, *ffn_args)

    moba_kv_p = zc[:n_p, C_K:].reshape(1, batch, seq, 2, MOBA_KV, HEAD_DIM)
    moba_kv_s = zc[n_p:, C_K:].reshape(1, db, t_new, 2, MOBA_KV, HEAD_DIM)

    return (h[:n_p].reshape(batch, seq, d_model), h[n_p:].reshape(db, t_new, d_model),
            nsa_kv_p, nsa_kv_s, nsa_win_p, nsa_win_s, gla_p[None], gla_s[None], moba_kv_p, moba_kv_s,
            jnp.stack([conv_p0, conv_p1]), jnp.stack([conv_s0, conv_s1]))
```

```python
import functools
import math

import jax
import jax.numpy as jnp
from jax import lax
from jax.experimental import pallas as pl
from jax.experimental.pallas import tpu as pltpu

F32 = jnp.float32
BF16 = jnp.bfloat16

HEAD_DIM = 128
GLA_HEADS = 4
GLA_DK = 128
GLA_DV = 256
GLA_GATE_RANK = 16
GLA_TAU = 16.0
GLA_CHUNK = 16
NSA_HEADS = 8
NSA_KV = 2
NSA_REP = NSA_HEADS // NSA_KV
NSA_CMP_STRIDE = 16
NSA_CMP_LEN = 2 * NSA_CMP_STRIDE
NSA_CMP_HIDDEN = 256
NSA_SEL_BLOCK = 64
NSA_N_SEL = 16
NSA_WINDOW = 512
MOBA_HEADS = 16
MOBA_KV = 4
MOBA_REP = MOBA_HEADS // MOBA_KV
MOBA_BLOCK = 256
MOBA_TOPK = 3
FFN_CONV = 3
NORM_EPS = 1e-6
PAGE_SIZE = 128

LANES = 128
SUBLANES = 8
NEG = -1e30
MASK_BIAS = -(2.0 ** 100)
SAMPLE_ROWS = 8

AB_GQ, AB_GK, AB_GV, AB_GR, AB_NQ = 0, 512, 1024, 2048, 3072
AB_CMP, AB_SEL, AB_WIN, AB_MISC0, AB_MISC1, AB_TOTAL = 4096, 4608, 5120, 5632, 5760, 6144
MISC_GATE_COL = GLA_GATE_RANK


def _rms(x, g):
    return x * lax.rsqrt(jnp.mean(x * x, axis=-1, keepdims=True) + NORM_EPS) * g


def _dot_nt(a, b):
    return lax.dot_general(a, b, (((1,), (1,)), ((), ())), preferred_element_type=F32)


def _dot_tn(a, b):
    return lax.dot_general(a, b, (((0,), (0,)), ((), ())), preferred_element_type=F32)


def _dot(a, b):
    return jnp.dot(a, b, preferred_element_type=F32)


def _silu(x):
    return x * jax.nn.sigmoid(x)


def _norm_matmul_kernel(x_ref, g_ref, w_ref, o_ref, xn_ref):
    @pl.when(pl.program_id(1) == 0)
    def _():
        xn_ref[...] = _rms(x_ref[...], g_ref[...]).astype(BF16)

    o_ref[...] = _dot(xn_ref[...], w_ref[...])


def norm_matmul(x, g, w, *, tm, tn):
    n, k = x.shape
    nout = w.shape[1]
    return pl.pallas_call(
        _norm_matmul_kernel,
        grid=(n // tm, nout // tn),
        in_specs=[pl.BlockSpec((tm, k), lambda i, j: (i, 0)),
                  pl.BlockSpec((1, k), lambda i, j: (0, 0)),
                  pl.BlockSpec((k, tn), lambda i, j: (0, j))],
        out_specs=pl.BlockSpec((tm, tn), lambda i, j: (i, j)),
        out_shape=jax.ShapeDtypeStruct((n, nout), F32),
        scratch_shapes=[pltpu.VMEM((tm, k), BF16)],
        compiler_params=pltpu.CompilerParams(dimension_semantics=("parallel", "arbitrary")),
        name="norm_matmul",
    )(x, g.reshape(1, k), w)


def _matmul_res_kernel(a_ref, w_ref, r_ref, o_ref):
    d = _dot(a_ref[...], w_ref[...])

    @pl.when(pl.program_id(2) == 0)
    def _():
        o_ref[...] = r_ref[...] + d

    @pl.when(pl.program_id(2) > 0)
    def _():
        o_ref[...] += d


def matmul_residual(a, w, res, *, tm, tn, tk):
    n, k = a.shape
    nout = w.shape[1]
    return pl.pallas_call(
        _matmul_res_kernel,
        grid=(n // tm, nout // tn, k // tk),
        in_specs=[pl.BlockSpec((tm, tk), lambda i, j, kk: (i, kk)),
                  pl.BlockSpec((tk, tn), lambda i, j, kk: (kk, j)),
                  pl.BlockSpec((tm, tn), lambda i, j, kk: (i, j))],
        out_specs=pl.BlockSpec((tm, tn), lambda i, j, kk: (i, j)),
        out_shape=jax.ShapeDtypeStruct((n, nout), F32),
        compiler_params=pltpu.CompilerParams(dimension_semantics=("parallel", "parallel", "arbitrary")),
        name="matmul_residual",
    )(a, w, res)


def _conv_act(g, p1, p2, val, cw_ref, cb_ref):
    c = cb_ref[...] + cw_ref[0:1, :] * p2 + cw_ref[1:2, :] * p1 + cw_ref[2:3, :] * g
    return (_silu(c) * val).astype(BF16)


def _ffn_act_prompt_kernel(g_ref, v_ref, halo_ref, cw_ref, cb_ref, o_ref):
    g = g_ref[...]
    tm = g.shape[0]
    halo = jnp.where(pl.program_id(1) == 0, 0.0, halo_ref[...])
    row = lax.broadcasted_iota(jnp.int32, (tm, 1), 0)
    p1 = jnp.where(row == 0, halo[7:8, :], pltpu.roll(g, 1, 0))
    p2 = jnp.where(row == 0, halo[6:7, :], jnp.where(row == 1, halo[7:8, :], pltpu.roll(g, 2, 0)))
    o_ref[...] = _conv_act(g, p1, p2, v_ref[...], cw_ref, cb_ref)


def ffn_act_prompt(u, conv_w, conv_b, *, batch, seq, tm, tn):
    d_ff = u.shape[1] // 2
    nj = d_ff // tn
    nt = seq // tm
    hb = tm // SUBLANES
    return pl.pallas_call(
        _ffn_act_prompt_kernel,
        grid=(batch, nt, nj),
        in_specs=[pl.BlockSpec((tm, tn), lambda b, i, j: (b * nt + i, j)),
                  pl.BlockSpec((tm, tn), lambda b, i, j: (b * nt + i, nj + j)),
                  pl.BlockSpec((SUBLANES, tn), lambda b, i, j: (jnp.maximum((b * nt + i) * hb - 1, 0), j)),
                  pl.BlockSpec((FFN_CONV, tn), lambda b, i, j: (0, j)),
                  pl.BlockSpec((1, tn), lambda b, i, j: (0, j))],
        out_specs=pl.BlockSpec((tm, tn), lambda b, i, j: (b * nt + i, j)),
        out_shape=jax.ShapeDtypeStruct((batch * seq, d_ff), BF16),
        compiler_params=pltpu.CompilerParams(dimension_semantics=("parallel", "parallel", "parallel")),
        name="ffn_act_prompt",
    )(u, u, u, conv_w, conv_b.reshape(1, d_ff))


def _ffn_act_sample_kernel(g_ref, v_ref, s1_ref, s2_ref, cw_ref, cb_ref, o_ref, *, t_new):
    g = g_ref[...]
    tm = g.shape[0]
    t = lax.broadcasted_iota(jnp.int32, (tm, 1), 0) % t_new
    p1 = jnp.where(t >= 1, pltpu.roll(g, 1, 0), s1_ref[...])
    p2 = jnp.where(t >= 2, pltpu.roll(g, 2, 0), s2_ref[...])
    o_ref[...] = _conv_act(g, p1, p2, v_ref[...], cw_ref, cb_ref)


def ffn_act_sample(u, row0, conv_state, conv_w, conv_b, *, t_new, tn):
    d_ff = u.shape[1] // 2
    nj = d_ff // tn
    db = conv_state.shape[0]
    tm = db * t_new
    rb = row0 // tm
    zero = jnp.zeros((db, t_new - 1, d_ff), F32)
    s1 = jnp.concatenate([conv_state[:, 1:2], zero], axis=1).reshape(tm, d_ff)
    s2 = jnp.concatenate([conv_state, jnp.zeros((db, t_new - 2, d_ff), F32)], axis=1).reshape(tm, d_ff)
    return pl.pallas_call(
        functools.partial(_ffn_act_sample_kernel, t_new=t_new),
        grid=(nj,),
        in_specs=[pl.BlockSpec((tm, tn), lambda j: (rb, j)),
                  pl.BlockSpec((tm, tn), lambda j: (rb, nj + j)),
                  pl.BlockSpec((tm, tn), lambda j: (0, j)),
                  pl.BlockSpec((tm, tn), lambda j: (0, j)),
                  pl.BlockSpec((FFN_CONV, tn), lambda j: (0, j)),
                  pl.BlockSpec((1, tn), lambda j: (0, j))],
        out_specs=pl.BlockSpec((tm, tn), lambda j: (0, j)),
        out_shape=jax.ShapeDtypeStruct((tm, d_ff), BF16),
        compiler_params=pltpu.CompilerParams(dimension_semantics=("parallel",)),
        name="ffn_act_sample",
    )(u, u, s1, s2, conv_w, conv_b.reshape(1, d_ff))


def _gla_kernel(q_ref, k_ref, v_ref, r_ref, m_ref, w2_ref, b2_ref, on_ref, s0_ref, o_ref, sout_ref,
                st_ref, cum_ref, *, chunk, n_valid, nsub):
    i = pl.program_id(1)
    tt = nsub * chunk

    @pl.when(i == 0)
    def _():
        for h in range(GLA_HEADS):
            st_ref[h] = s0_ref[0, h].T

    a = _dot(m_ref[0].astype(BF16), w2_ref[...]) + b2_ref[...]
    a = (jnp.minimum(a, 0.0) - jnp.log(1.0 + jnp.exp(-jnp.abs(a)))) / GLA_TAU
    pos = lax.broadcasted_iota(jnp.int32, (tt, 1), 0) % chunk
    if n_valid < chunk:
        a = jnp.where(pos < n_valid, a, 0.0)
    cum = a
    sh = 1
    while sh < chunk:
        cum = cum + jnp.where(pos >= sh, pltpu.roll(cum, sh, 0), 0.0)
        sh *= 2
    cum_ref[...] = cum

    ti = lax.broadcasted_iota(jnp.int32, (chunk, 1), 0)

    def step(c, carry):
        rows = pl.ds(pl.multiple_of(c * chunk, chunk), chunk)
        for h in range(GLA_HEADS):
            kcols = slice(h * GLA_DK, (h + 1) * GLA_DK)
            vcols = slice(h * GLA_DV, (h + 1) * GLA_DV)
            qh = q_ref[0, rows, kcols] * GLA_DK ** -0.5
            kh = k_ref[0, rows, kcols]
            vh = v_ref[0, rows, vcols]
            ch = cum_ref[rows, kcols]
            st = st_ref[h]
            o = _dot_nt((qh * jnp.exp(ch)).astype(BF16), st.astype(BF16))
            for s in range(chunk):
                d = jnp.exp(jnp.where(ti >= s, ch - ch[s:s + 1, :], NEG))
                w = jnp.sum(qh * kh[s:s + 1, :] * d, axis=-1, keepdims=True)
                o = o + w * vh[s:s + 1, :]
            last = ch[chunk - 1:chunk, :]
            kt = kh * jnp.exp(last - ch)
            st_ref[h] = st * jnp.exp(last) + _dot_tn(vh.astype(BF16), kt.astype(BF16))
            rh = r_ref[0, rows, vcols]
            o_ref[0, rows, vcols] = (_rms(o, on_ref[...]) * _silu(rh)).astype(o_ref.dtype)
        return carry

    lax.fori_loop(0, nsub, step, 0)

    @pl.when(i == pl.num_programs(1) - 1)
    def _():
        for h in range(GLA_HEADS):
            sout_ref[0, h] = st_ref[h].T


def gla_mixer(z3, s0, w2p, b2, onorm, *, n_seq, tiles_per_seq, chunk, n_valid, out_dtype):
    tt = z3.shape[1]
    nsub = tt // chunk
    dqk = GLA_HEADS * GLA_DK
    dv = GLA_HEADS * GLA_DV
    tile = lambda b, i: b * tiles_per_seq + i
    return pl.pallas_call(
        functools.partial(_gla_kernel, chunk=chunk, n_valid=n_valid, nsub=nsub),
        grid=(n_seq, tiles_per_seq),
        in_specs=[pl.BlockSpec((1, tt, dqk), lambda b, i: (tile(b, i), 0, AB_GQ // dqk)),
                  pl.BlockSpec((1, tt, dqk), lambda b, i: (tile(b, i), 0, AB_GK // dqk)),
                  pl.BlockSpec((1, tt, dv), lambda b, i: (tile(b, i), 0, AB_GV // dv)),
                  pl.BlockSpec((1, tt, dv), lambda b, i: (tile(b, i), 0, AB_GR // dv)),
                  pl.BlockSpec((1, tt, LANES), lambda b, i: (tile(b, i), 0, AB_MISC0 // LANES)),
                  pl.BlockSpec((LANES, dqk), lambda b, i: (0, 0)),
                  pl.BlockSpec((1, dqk), lambda b, i: (0, 0)),
                  pl.BlockSpec((1, GLA_DV), lambda b, i: (0, 0)),
                  pl.BlockSpec((1, GLA_HEADS, GLA_DK, GLA_DV), lambda b, i: (b, 0, 0, 0))],
        out_specs=[pl.BlockSpec((1, tt, dv), lambda b, i: (tile(b, i), 0, 0)),
                   pl.BlockSpec((1, GLA_HEADS, GLA_DK, GLA_DV), lambda b, i: (b, 0, 0, 0))],
        out_shape=[jax.ShapeDtypeStruct((n_seq * tiles_per_seq, tt, dv), out_dtype),
                   jax.ShapeDtypeStruct((n_seq, GLA_HEADS, GLA_DK, GLA_DV), F32)],
        scratch_shapes=[pltpu.VMEM((GLA_HEADS, GLA_DV, GLA_DK), F32), pltpu.VMEM((tt, dqk), F32)],
        compiler_params=pltpu.CompilerParams(dimension_semantics=("parallel", "arbitrary")),
        name="gla_mixer",
    )(z3, z3, z3, z3, z3, w2p, b2.reshape(1, dqk), onorm.reshape(1, GLA_DV), s0)


def _masked_softmax_rows(s, mask):
    sm = jnp.where(mask[None], s, NEG)
    m = jnp.max(sm, axis=-1, keepdims=True)
    e = jnp.where(mask[None], jnp.exp(sm - m), 0.0)
    d = jnp.sum(e, axis=-1, keepdims=True)
    return e / jnp.where(d > 0, d, 1.0)


def _dot_exact_rhs(x, m):
    hi = x.astype(BF16)
    r1 = x - hi.astype(F32)
    mid = r1.astype(BF16)
    lo = (r1 - mid.astype(F32)).astype(BF16)
    return _dot(hi, m) + _dot(mid, m) + _dot(lo, m)


def _topk_mask(score, n_rounds):
    lane = lax.broadcasted_iota(jnp.int32, score.shape, 1).astype(F32)
    sel = jnp.zeros(score.shape, F32)
    work = score
    for _ in range(n_rounds):
        mx = jnp.max(work, axis=-1, keepdims=True)
        first = jnp.min(jnp.where(work == mx, lane, 1e9), axis=-1, keepdims=True)
        pick = lane == first
        sel = jnp.where(pick & (mx > 0.5 * NEG), 1.0, sel)
        work = jnp.where(pick, 2.0 * NEG, work)
    return sel


def _block_expand(n_blocks_pad, n_keys, block, key0=0):
    bi = lax.broadcasted_iota(jnp.int32, (n_blocks_pad, n_keys), 0)
    ki = lax.broadcasted_iota(jnp.int32, (n_blocks_pad, n_keys), 1) + key0
    return (bi == jnp.right_shift(ki, int(math.log2(block)))).astype(BF16)


def _gelu_tanh(x):
    return x * (0.5 * (1.0 + jnp.tanh(math.sqrt(2.0 / math.pi) * (x + 0.044715 * (x * x * x)))))


def _cmp_mlp(chunk_rows, n_ch, w1_ref, cpe_ref, cb1_ref, w2_ref, cb2_ref, kg_ref):
    hid_w = NSA_CMP_HIDDEN
    out = {}
    for e in range(2):
        accp = _dot(cpe_ref[e], w1_ref[e])
        pe = accp[0:1, :hid_w] + accp[1:2, hid_w:] + cb1_ref[e]
        x = jnp.concatenate([chunk_rows(e * NSA_KV + g) for g in range(NSA_KV)], axis=0).astype(BF16)
        acc = _dot(x, w1_ref[e])
        hid = []
        for g in range(NSA_KV):
            a = acc[g * n_ch:(g + 1) * n_ch]
            hid.append(_gelu_tanh(a[:, :hid_w] + pltpu.roll(a[:, hid_w:], n_ch - 1, 0) + pe))
        ckv = _dot(jnp.concatenate(hid, axis=0).astype(BF16), w2_ref[e]) + cb2_ref[e]
        for g in range(NSA_KV):
            out[e, g] = ckv[g * n_ch:(g + 1) * n_ch]
    kc = [_rms(out[0, g], kg_ref[...]) for g in range(NSA_KV)]
    vc = [out[1, g] for g in range(NSA_KV)]
    return kc, vc


def _nsa_q4(q, qg):
    return jnp.concatenate([_rms(q[:, r * HEAD_DIM:(r + 1) * HEAD_DIM], qg) * HEAD_DIM ** -0.5
                            for r in range(NSA_REP)], axis=0)


def _nsa_cmp_branch(q4, kc, vc, qpos, n_cmp):
    nq = qpos.shape[0]
    n_ch = kc.shape[0]
    sc = _dot_nt(q4, kc.astype(BF16)).reshape(NSA_REP, nq, n_ch)
    cidx = lax.broadcasted_iota(jnp.int32, (1, n_ch), 1)
    cmask = (NSA_CMP_STRIDE * cidx + NSA_CMP_LEN - 1 <= qpos) & (cidx < n_cmp)
    p = _masked_softmax_rows(sc, cmask)
    o = _dot(p.reshape(NSA_REP * nq, n_ch).astype(BF16), vc.astype(BF16))
    return o, jnp.sum(p, axis=0)


def _nsa_select(pcs, qpos0, n_sb):
    n_ch = pcs.shape[1]
    ratio = NSA_SEL_BLOCK // NSA_CMP_STRIDE
    bi = lax.broadcasted_iota(jnp.int32, (LANES, n_ch), 0)
    ci = lax.broadcasted_iota(jnp.int32, (LANES, n_ch), 1)
    mimp = ((ci >= ratio * bi - 1) & (ci <= ratio * bi + ratio - 1)).astype(BF16)
    hi = pcs.astype(BF16)
    r1 = pcs - hi.astype(F32)
    mid = r1.astype(BF16)
    lo = (r1 - mid.astype(F32)).astype(BF16)
    imp = _dot_nt(mimp, hi) + _dot_nt(mimp, mid) + _dot_nt(mimp, lo)
    nbp = -(-n_sb // SUBLANES) * SUBLANES
    blk = lax.broadcasted_iota(jnp.int32, (nbp, 1), 0)
    qpos = qpos0 + lax.broadcasted_iota(jnp.int32, (1, LANES), 1)
    cur = jnp.right_shift(qpos, int(math.log2(NSA_SEL_BLOCK)))
    valid = (blk <= cur) & (blk < n_sb)
    forced = (blk == 0) | (blk == cur) | (blk == cur - 1)
    score = jnp.where(forced, -NEG, jnp.where(valid, imp[:nbp], NEG))
    rank = jnp.zeros((nbp, LANES), F32)
    for k in range(n_sb):
        sk = score[k:k + 1, :]
        rank = rank + jnp.where((sk > score) | ((sk == score) & (blk > k)), 1.0, 0.0)
    sel_t = jnp.where((rank < min(NSA_N_SEL, n_sb)) & (score > 0.5 * NEG), 1.0, 0.0)
    sel_t = jnp.concatenate([sel_t, jnp.zeros((LANES - nbp, LANES), F32)], axis=0)
    return sel_t.T


def _nsa_window_mask(kpos, qpos):
    return (kpos <= qpos) & (qpos - kpos < NSA_WINDOW) & (kpos >= 0)


def _nsa_gate_mix(gsig, o_cmp, o_sel, o_win, nq):
    outs = []
    for r in range(NSA_REP):
        c0 = MISC_GATE_COL + 3 * r
        rows = slice(r * nq, (r + 1) * nq)
        outs.append(gsig[:, c0:c0 + 1] * o_cmp[rows] + gsig[:, c0 + 1:c0 + 2] * o_sel[rows]
                    + gsig[:, c0 + 2:c0 + 3] * o_win[rows])
    return jnp.concatenate(outs, axis=1)


def _nsa_cmp_kernel(*refs, n_ch):
    x_refs = refs[:NSA_CMP_STRIDE]
    w1_ref, cpe_ref, cb1_ref, w2_ref, cb2_ref, kg_ref, kc_ref, vc_ref = refs[NSA_CMP_STRIDE:]

    def chunk_rows(eg):
        return jnp.concatenate([x_refs[s][:, eg * HEAD_DIM:(eg + 1) * HEAD_DIM] for s in range(NSA_CMP_STRIDE)], axis=1)

    kc, vc = _cmp_mlp(chunk_rows, n_ch, w1_ref, cpe_ref, cb1_ref, w2_ref, cb2_ref, kg_ref)
    for g in range(NSA_KV):
        kc_ref[0, g] = kc[g]
        vc_ref[0, g] = vc[g]


def _cmp_weight_specs():
    z = (0,) * 8
    full = lambda shape: pl.BlockSpec(shape, lambda *a: z[:len(shape)])
    taps = NSA_CMP_STRIDE * HEAD_DIM
    return [full((2, taps, 2 * NSA_CMP_HIDDEN)), full((2, SUBLANES, taps)),
            full((2, 1, NSA_CMP_HIDDEN)), full((2, NSA_CMP_HIDDEN, HEAD_DIM)), full((2, 1, HEAD_DIM)),
            full((1, HEAD_DIM))]


def nsa_cmp_prompt(z, cmpw, *, batch, seq):
    n_ch = seq // NSA_CMP_STRIDE
    seg = 4 * HEAD_DIM
    st = NSA_CMP_STRIDE
    zc = z[:batch * seq, AB_CMP:AB_CMP + seg].reshape(batch * n_ch, st * seg)
    shp = jax.ShapeDtypeStruct((batch, NSA_KV, n_ch, HEAD_DIM), F32)
    ospec = pl.BlockSpec((1, NSA_KV, n_ch, HEAD_DIM), lambda b: (b, 0, 0, 0))
    xspec = lambda s: pl.BlockSpec((n_ch, seg), lambda b: (b, s))
    return pl.pallas_call(
        functools.partial(_nsa_cmp_kernel, n_ch=n_ch),
        grid=(batch,),
        in_specs=[xspec(s) for s in range(st)] + _cmp_weight_specs(),
        out_specs=[ospec, ospec],
        out_shape=[shp, shp],
        compiler_params=pltpu.CompilerParams(dimension_semantics=("parallel",)),
        name="nsa_cmp_prompt",
    )(*([zc] * st), *cmpw)


def _nsa_prep_kernel(sel_ref, win_ref, kg_ref, o_ref):
    d = HEAD_DIM
    parts = []
    for ref, row in ((sel_ref, 1), (win_ref, 2)):
        x = ref[...]
        for g in range(NSA_KV):
            parts.append(_rms(x[:, g * d:(g + 1) * d], kg_ref[row:row + 1, :]))
        parts.append(x[:, NSA_KV * d:])
    o_ref[...] = jnp.concatenate(parts, axis=1).astype(BF16)


def nsa_prep_prompt(z, k_gain, *, rows, tm):
    seg = 4 * HEAD_DIM
    return pl.pallas_call(
        _nsa_prep_kernel,
        grid=(rows // tm,),
        in_specs=[pl.BlockSpec((tm, seg), lambda i: (i, AB_SEL // seg)),
                  pl.BlockSpec((tm, seg), lambda i: (i, AB_WIN // seg)),
                  pl.BlockSpec((3, HEAD_DIM), lambda i: (0, 0))],
        out_specs=pl.BlockSpec((tm, 2 * seg), lambda i: (i, 0)),
        out_shape=jax.ShapeDtypeStruct((rows, 2 * seg), BF16),
        compiler_params=pltpu.CompilerParams(dimension_semantics=("parallel",)),
        name="nsa_prep_prompt",
    )(z, z, k_gain)


def _nsa_attn_kernel(q_ref, m_ref, kc_ref, vc_ref, ks_ref, vs_ref, kw_ref, vw_ref, oh_ref, qg_ref, o_ref, *, seq, kt):
    i = pl.program_id(2)
    nq = q_ref.shape[0]
    assert nq == LANES
    q4 = _nsa_q4(q_ref[...], qg_ref[...]).astype(BF16)
    qpos = i * nq + lax.broadcasted_iota(jnp.int32, (nq, 1), 0)
    n_ch = seq // NSA_CMP_STRIDE
    n_sb = -(-seq // NSA_SEL_BLOCK)

    o_cmp, pcs = _nsa_cmp_branch(q4, kc_ref[0, 0], vc_ref[0, 0], qpos, n_ch - 1)
    sel = _nsa_select(pcs, i * nq, n_sb)
    bias = jnp.where(sel > 0.5, 0.0, MASK_BIAS).astype(BF16)
    q_aug = jnp.concatenate([q4, jnp.concatenate([bias] * NSA_REP, axis=0)], axis=1)

    def sel_step(j, carry, causal):
        m, l, acc = carry
        rows = pl.ds(pl.multiple_of(j * kt, kt), kt)
        k_aug = jnp.concatenate([ks_ref[rows, :], oh_ref[rows, :]], axis=1)
        s = _dot_nt(q_aug, k_aug).reshape(NSA_REP, nq, kt)
        if causal:
            kpos = j * kt + lax.broadcasted_iota(jnp.int32, (1, kt), 1)
            s = jnp.where((kpos <= qpos)[None], s, NEG)
        mn = jnp.maximum(m, jnp.max(s, axis=-1, keepdims=True))
        p = jnp.exp(s - mn)
        alpha = jnp.exp(m - mn)
        l = alpha * l + jnp.sum(p, axis=-1, keepdims=True)
        pv = _dot(p.reshape(NSA_REP * nq, kt).astype(BF16), vs_ref[rows, :])
        return mn, l, alpha * acc + pv.reshape(NSA_REP, nq, HEAD_DIM)

    last = (i * nq + nq - 1) // kt
    init = (jnp.full((NSA_REP, nq, 1), NEG, F32), jnp.zeros((NSA_REP, nq, 1), F32),
            jnp.zeros((NSA_REP, nq, HEAD_DIM), F32))
    carry = lax.fori_loop(0, last, functools.partial(sel_step, causal=False), init)
    _, l, acc = sel_step(last, carry, True)
    o_sel = (acc / l).reshape(NSA_REP * nq, HEAD_DIM)

    span = min(NSA_WINDOW + nq, seq)
    start = pl.multiple_of(jnp.clip(i * nq - NSA_WINDOW, 0, seq - span), nq)
    s = _dot_nt(q4, kw_ref[pl.ds(start, span), :]).reshape(NSA_REP, nq, span)
    kpos = start + lax.broadcasted_iota(jnp.int32, (1, span), 1)
    p = _masked_softmax_rows(s, _nsa_window_mask(kpos, qpos))
    o_win = _dot(p.reshape(NSA_REP * nq, span).astype(BF16), vw_ref[pl.ds(start, span), :])

    o_ref[...] = _nsa_gate_mix(jax.nn.sigmoid(m_ref[...]), o_cmp, o_sel, o_win, nq).astype(o_ref.dtype)


def nsa_attn_prompt(z, kvp, kc, vc, q_gain, *, batch, seq, tq):
    nt = seq // tq
    gw = NSA_REP * HEAD_DIM
    kt = min(4 * tq, seq)
    row = lambda b, g, i: b * nt + i
    cspec = pl.BlockSpec((1, 1, seq // NSA_CMP_STRIDE, HEAD_DIM), lambda b, g, i: (b, g, 0, 0))
    kvspec = lambda c: pl.BlockSpec((seq, HEAD_DIM), lambda b, g, i: (b, c + g))
    return pl.pallas_call(
        functools.partial(_nsa_attn_kernel, seq=seq, kt=kt),
        grid=(batch, NSA_KV, nt),
        in_specs=[pl.BlockSpec((tq, gw), lambda b, g, i: (row(b, g, i), AB_NQ // gw + g)),
                  pl.BlockSpec((tq, LANES), lambda b, g, i: (row(b, g, i), AB_MISC0 // LANES + g)),
                  cspec, cspec, kvspec(0), kvspec(2), kvspec(4), kvspec(6),
                  pl.BlockSpec((seq, LANES), lambda b, g, i: (0, 0)),
                  pl.BlockSpec((1, HEAD_DIM), lambda b, g, i: (0, 0))],
        out_specs=pl.BlockSpec((tq, gw), lambda b, g, i: (row(b, g, i), g)),
        out_shape=jax.ShapeDtypeStruct((batch * seq, NSA_KV * gw), BF16),
        compiler_params=pltpu.CompilerParams(dimension_semantics=("parallel", "parallel", "arbitrary")),
        name="nsa_attn_prompt",
    )(z, z, kc, vc, kvp, kvp, kvp, kvp, _block_onehot(seq, NSA_SEL_BLOCK), q_gain.reshape(1, HEAD_DIM))


def _block_onehot(n_keys, block):
    k = lax.broadcasted_iota(jnp.int32, (n_keys, LANES), 0) // block
    return (k == lax.broadcasted_iota(jnp.int32, (n_keys, LANES), 1)).astype(BF16)


def nsa_cmp_weights(cw1, cb1, cw2, cb2, cpe, k_gain):
    st = NSA_CMP_STRIDE
    taps = st * HEAD_DIM
    w1cat = jnp.concatenate([cw1[:, :st], cw1[:, st:]], axis=-1).reshape(2, taps, 2 * NSA_CMP_HIDDEN).astype(BF16)
    pe_rows = jnp.stack([cpe[:, :st].reshape(2, taps), cpe[:, st:].reshape(2, taps)], axis=1)
    pe_rows = jnp.pad(pe_rows, ((0, 0), (0, SUBLANES - 2), (0, 0))).astype(BF16)
    return (w1cat, pe_rows, cb1.reshape(2, 1, NSA_CMP_HIDDEN), cw2.astype(BF16), cb2.reshape(2, 1, HEAD_DIM),
            k_gain[0].reshape(1, HEAD_DIM))


NSA_ROW = 4 * NSA_KV * HEAD_DIM
CHUNKS_PER_PAGE = PAGE_SIZE // NSA_CMP_STRIDE


def _nsa_sample_kernel(pt_ref, *refs, n_pages, past, t_new):
    del pt_ref
    pages = refs[:n_pages]
    (q_ref, sel_ref, win_ref, m0_ref, m1_ref, wb_ref, w1_ref, cpe_ref, cb1_ref, w2_ref, cb2_ref, kg0_ref,
     kg_ref, qg_ref, o_ref) = refs[n_pages:]
    nq = SAMPLE_ROWS
    d = HEAD_DIM
    n_ch = n_pages * CHUNKS_PER_PAGE
    n_cmp = (past + t_new) // NSA_CMP_STRIDE - 1
    n_sb = -(-(past + t_new) // NSA_SEL_BLOCK)
    n_sl = NSA_ROW // d
    n_wsl = 2 * NSA_KV

    def chunk_rows(j):
        return jnp.concatenate(
            [jnp.concatenate([pg[0, pl.ds(s * n_sl + j, CHUNKS_PER_PAGE, stride=NSA_CMP_STRIDE * n_sl), :]
                              for s in range(NSA_CMP_STRIDE)], axis=1) for pg in pages], axis=0)

    def past_rows(j):
        return jnp.concatenate([pg[0, pl.ds(j, PAGE_SIZE, stride=n_sl), :] for pg in pages], axis=0)

    kc, vc = _cmp_mlp(chunk_rows, n_ch, w1_ref, cpe_ref, cb1_ref, w2_ref, cb2_ref, kg0_ref)

    qpos = past + lax.broadcasted_iota(jnp.int32, (nq, 1), 0)
    nk = past + LANES
    kpos = lax.broadcasted_iota(jnp.int32, (1, nk), 1)
    esel = _block_expand(LANES, nk, NSA_SEL_BLOCK)
    causal = kpos <= qpos
    nw = NSA_WINDOW + LANES
    wmask = _nsa_window_mask(past - NSA_WINDOW + lax.broadcasted_iota(jnp.int32, (1, nw), 1), qpos)
    zpad = jnp.zeros((LANES - nq, d), F32)
    q = q_ref[0]
    sel_new = sel_ref[0]
    win_new = win_ref[0]
    kg_sel = kg_ref[1:2, :]
    kg_win = kg_ref[2:3, :]
    col = lambda x, c: x[:, c * d:(c + 1) * d]
    wb_rows = lambda j: wb_ref[0, pl.ds(j, NSA_WINDOW, stride=n_wsl), :]

    for g in range(NSA_KV):
        q4 = _nsa_q4(q[:, g * NSA_REP * d:(g + 1) * NSA_REP * d], qg_ref[...]).astype(BF16)
        o_cmp, pcs = _nsa_cmp_branch(q4, kc[g], vc[g], qpos, n_cmp)
        pcs = jnp.concatenate([pcs, jnp.zeros((LANES - nq, n_ch), F32)], axis=0)
        selb = _nsa_select(pcs, past, n_sb)[:nq].astype(BF16)

        ks = jnp.concatenate([_rms(past_rows(2 * NSA_KV + g), kg_sel), _rms(col(sel_new, g), kg_sel), zpad], axis=0)
        vs = jnp.concatenate([past_rows(3 * NSA_KV + g), col(sel_new, NSA_KV + g), zpad], axis=0)
        s = _dot_nt(q4, ks.astype(BF16)).reshape(NSA_REP, nq, nk)
        p = _masked_softmax_rows(s, (_dot(selb, esel) > 0.5) & causal)
        o_sel = _dot(p.reshape(NSA_REP * nq, nk).astype(BF16), vs.astype(BF16))

        kw = jnp.concatenate([_rms(wb_rows(g), kg_win), _rms(col(win_new, g), kg_win), zpad], axis=0)
        vw = jnp.concatenate([wb_rows(NSA_KV + g), col(win_new, NSA_KV + g), zpad], axis=0)
        s = _dot_nt(q4, kw.astype(BF16)).reshape(NSA_REP, nq, nw)
        p = _masked_softmax_rows(s, wmask)
        o_win = _dot(p.reshape(NSA_REP * nq, nw).astype(BF16), vw.astype(BF16))

        gsig = jax.nn.sigmoid((m0_ref, m1_ref)[g][0])
        o_ref[0, :, g * NSA_REP * d:(g + 1) * NSA_REP * d] = _nsa_gate_mix(gsig, o_cmp, o_sel, o_win, nq).astype(o_ref.dtype)


def nsa_sample(zs, page_table, cache_kv, cache_win, cmpw, k_gain, q_gain, *, t_new):
    db, n_pages = page_table.shape
    past = n_pages * PAGE_SIZE
    assert cache_win.shape[1] == NSA_WINDOW and (past + t_new) // NSA_CMP_STRIDE == n_pages * CHUNKS_PER_PAGE
    n_phys = cache_kv.shape[0]
    page_rows = PAGE_SIZE * NSA_ROW // HEAD_DIM
    win_rows = NSA_WINDOW * 2 * NSA_KV
    pages = cache_kv.reshape(n_phys, page_rows, HEAD_DIM)
    wb = cache_win.reshape(db, win_rows, HEAD_DIM)
    seg = 4 * HEAD_DIM
    qw = NSA_HEADS * HEAD_DIM
    zspec = lambda w, off: pl.BlockSpec((1, SAMPLE_ROWS, w), lambda b, pt: (b, 0, off // w))
    page_spec = lambda p: pl.BlockSpec((1, page_rows, HEAD_DIM), lambda b, pt: (pt[b, p], 0, 0))
    grid_spec = pltpu.PrefetchScalarGridSpec(
        num_scalar_prefetch=1,
        grid=(db,),
        in_specs=[page_spec(p) for p in range(n_pages)]
        + [zspec(qw, AB_NQ), zspec(seg, AB_SEL), zspec(seg, AB_WIN), zspec(LANES, AB_MISC0), zspec(LANES, AB_MISC1),
           pl.BlockSpec((1, win_rows, HEAD_DIM), lambda b, pt: (b, 0, 0))]
        + _cmp_weight_specs()
        + [pl.BlockSpec((3, HEAD_DIM), lambda b, pt: (0, 0)), pl.BlockSpec((1, HEAD_DIM), lambda b, pt: (0, 0))],
        out_specs=pl.BlockSpec((1, SAMPLE_ROWS, qw), lambda b, pt: (b, 0, 0)),
    )
    return pl.pallas_call(
        functools.partial(_nsa_sample_kernel, n_pages=n_pages, past=past, t_new=t_new),
        grid_spec=grid_spec,
        out_shape=jax.ShapeDtypeStruct((db, SAMPLE_ROWS, qw), BF16),
        compiler_params=pltpu.CompilerParams(dimension_semantics=("parallel",)),
        name="nsa_sample",
    )(page_table, *([pages] * n_pages), zs, zs, zs, zs, zs, wb, *cmpw, k_gain, q_gain.reshape(1, HEAD_DIM))


C_Q, C_K, C_V, C_TOTAL = 0, MOBA_HEADS * HEAD_DIM, (MOBA_HEADS + MOBA_KV) * HEAD_DIM, (MOBA_HEADS + 2 * MOBA_KV) * HEAD_DIM
MOBA_SUBTILE = 1024


def _masked_softmax2d(s, mask):
    sm = jnp.where(mask, s, NEG)
    m = jnp.max(sm, axis=-1, keepdims=True)
    e = jnp.where(mask, jnp.exp(sm - m), 0.0)
    dsum = jnp.sum(e, axis=-1, keepdims=True)
    return e / jnp.where(dsum > 0, dsum, 1.0)


def _moba_q4(q, qg):
    return jnp.concatenate([_rms(q[:, r * HEAD_DIM:(r + 1) * HEAD_DIM], qg) * HEAD_DIM ** -0.5
                            for r in range(MOBA_REP)], axis=0)


def _moba_select(q4, kmean_pad, cur, n_blocks):
    gs = _dot_nt(q4, kmean_pad.astype(BF16))
    blk = lax.broadcasted_iota(jnp.int32, (1, LANES), 1)
    score = jnp.where((blk < cur) & (blk < n_blocks), gs, NEG)
    sel = _topk_mask(score, min(MOBA_TOPK, n_blocks))
    return jnp.where(blk == cur, 1.0, sel)


def _moba_prep_kernel(k_ref, v_ref, kg_ref, o_ref, km_ref):
    k = k_ref[...]
    kn = jnp.concatenate([_rms(k[:, h * HEAD_DIM:(h + 1) * HEAD_DIM], kg_ref[...]) for h in range(MOBA_KV)], axis=1)
    km_ref[0] = jnp.mean(kn, axis=0, keepdims=True)
    o_ref[...] = jnp.concatenate([kn, v_ref[...]], axis=1).astype(BF16)


def moba_prep_prompt(zc, k_gain, *, rows):
    w = MOBA_KV * HEAD_DIM
    nblk = rows // MOBA_BLOCK
    return pl.pallas_call(
        _moba_prep_kernel,
        grid=(nblk,),
        in_specs=[pl.BlockSpec((MOBA_BLOCK, w), lambda i: (i, C_K // w)),
                  pl.BlockSpec((MOBA_BLOCK, w), lambda i: (i, C_V // w)),
                  pl.BlockSpec((1, HEAD_DIM), lambda i: (0, 0))],
        out_specs=[pl.BlockSpec((MOBA_BLOCK, 2 * w), lambda i: (i, 0)),
                   pl.BlockSpec((1, 1, w), lambda i: (i, 0, 0))],
        out_shape=[jax.ShapeDtypeStruct((rows, 2 * w), BF16), jax.ShapeDtypeStruct((nblk, 1, w), F32)],
        compiler_params=pltpu.CompilerParams(dimension_semantics=("parallel",)),
        name="moba_prep_prompt",
    )(zc, zc, k_gain.reshape(1, HEAD_DIM))


def _moba_attn_kernel(q_ref, km_ref, k_ref, v_ref, oh_ref, qg_ref, o_ref, *, seq):
    i = pl.program_id(2)
    nq = q_ref.shape[0]
    assert nq == MOBA_BLOCK
    nb = seq // MOBA_BLOCK
    rq = MOBA_REP * nq
    q4 = _moba_q4(q_ref[...], qg_ref[...]).astype(BF16)

    km = jnp.concatenate([km_ref[0], jnp.zeros((LANES - nb, HEAD_DIM), F32)], axis=0)
    nbp = -(-nb // SUBLANES) * SUBLANES
    gs = _dot_nt(km.astype(BF16), q4)[:nbp]
    blk = lax.broadcasted_iota(jnp.int32, (nbp, 1), 0)
    score = jnp.where((blk < i) & (blk < nb), gs, NEG)
    rank = jnp.zeros((nbp, rq), F32)
    for k in range(nb):
        sk = score[k:k + 1, :]
        rank = rank + jnp.where((sk > score) | ((sk == score) & (blk > k)), 1.0, 0.0)
    chosen = ((rank < min(MOBA_TOPK, nb)) & (score > 0.5 * NEG)) | (blk == i)
    bias_t = jnp.concatenate([jnp.where(chosen, 0.0, MASK_BIAS), jnp.zeros((LANES - nbp, rq), F32)], axis=0)
    q_aug = jnp.concatenate([q4, bias_t.T.astype(BF16)], axis=1)

    def update(carry, s, rows):
        m, l, acc = carry
        mn = jnp.maximum(m, jnp.max(s, axis=-1, keepdims=True))
        p = jnp.exp(s - mn)
        alpha = jnp.exp(m - mn)
        l = alpha * l + jnp.sum(p, axis=-1, keepdims=True)
        return mn, l, alpha * acc + _dot(p.astype(BF16), v_ref[rows, :])

    own = pl.ds(pl.multiple_of(i * MOBA_BLOCK, MOBA_BLOCK), MOBA_BLOCK)
    kcol = lax.broadcasted_iota(jnp.int32, (1, MOBA_BLOCK), 1)
    rs = min(MOBA_SUBTILE, rq)
    outs = []
    for t in range(rq // rs):
        qa = q_aug[t * rs:(t + 1) * rs]

        def past_step(n, carry, qa=qa):
            rows = pl.ds(pl.multiple_of(n * MOBA_BLOCK, MOBA_BLOCK), MOBA_BLOCK)
            k_aug = jnp.concatenate([k_ref[rows, :], oh_ref[rows, :]], axis=1)
            return update(carry, _dot_nt(qa, k_aug), rows)

        init = (jnp.full((rs, 1), NEG, F32), jnp.zeros((rs, 1), F32), jnp.zeros((rs, HEAD_DIM), F32))
        carry = lax.fori_loop(0, i, past_step, init)
        tq = (t * rs + lax.broadcasted_iota(jnp.int32, (rs, 1), 0)) % nq
        s_own = jnp.where(kcol <= tq, _dot_nt(qa[:, :HEAD_DIM], k_ref[own, :]), NEG)
        _, l, acc = update(carry, s_own, own)
        outs.append(acc / l)
    o = jnp.concatenate(outs, axis=0)
    o_ref[...] = jnp.concatenate([o[r * nq:(r + 1) * nq] for r in range(MOBA_REP)], axis=1).astype(o_ref.dtype)


def moba_attn_prompt(zc, kvp, kmean, q_gain, *, batch, seq):
    nt = seq // MOBA_BLOCK
    gw = MOBA_REP * HEAD_DIM
    km = kmean.reshape(batch, nt, MOBA_KV * HEAD_DIM)
    return pl.pallas_call(
        functools.partial(_moba_attn_kernel, seq=seq),
        grid=(batch, MOBA_KV, nt),
        in_specs=[pl.BlockSpec((MOBA_BLOCK, gw), lambda b, g, i: (b * nt + i, g)),
                  pl.BlockSpec((1, nt, HEAD_DIM), lambda b, g, i: (b, 0, g)),
                  pl.BlockSpec((seq, HEAD_DIM), lambda b, g, i: (b, g)),
                  pl.BlockSpec((seq, HEAD_DIM), lambda b, g, i: (b, MOBA_KV + g)),
                  pl.BlockSpec((seq, LANES), lambda b, g, i: (0, 0)),
                  pl.BlockSpec((1, HEAD_DIM), lambda b, g, i: (0, 0))],
        out_specs=pl.BlockSpec((MOBA_BLOCK, gw), lambda b, g, i: (b * nt + i, g)),
        out_shape=jax.ShapeDtypeStruct((batch * seq, MOBA_HEADS * HEAD_DIM), BF16),
        compiler_params=pltpu.CompilerParams(dimension_semantics=("parallel", "parallel", "arbitrary")),
        name="moba_attn_prompt",
    )(zc, km, kvp, kvp, _block_onehot(seq, MOBA_BLOCK), q_gain.reshape(1, HEAD_DIM))


def _moba_sample_kernel(pt_ref, *refs, n_pages, past, t_new):
    del pt_ref
    pages = refs[:n_pages]
    q_ref, kn_ref, vn_ref, kg_ref, qg_ref, o_ref = refs[n_pages:]
    nq = SAMPLE_ROWS
    d = HEAD_DIM
    rq = MOBA_REP * nq
    nb = -(-(past + t_new) // MOBA_BLOCK)
    n_past_blocks = past // MOBA_BLOCK
    nk = past + LANES
    qpos = past + lax.broadcasted_iota(jnp.int32, (nq, 1), 0)
    qpos4 = jnp.concatenate([qpos] * MOBA_REP, axis=0)
    cur = jnp.right_shift(qpos4, int(math.log2(MOBA_BLOCK)))
    kpos = lax.broadcasted_iota(jnp.int32, (1, nk), 1)
    expand = _block_expand(LANES, nk, MOBA_BLOCK)
    causal = kpos <= qpos4
    zpad = jnp.zeros((LANES - nq, d), F32)
    q = q_ref[0]
    k_new = kn_ref[0]
    v_new = vn_ref[0]
    n_sl = 2 * MOBA_KV
    past_rows = lambda j: jnp.concatenate([pg[0, pl.ds(j, PAGE_SIZE, stride=n_sl), :] for pg in pages], axis=0)
    for g in range(MOBA_KV):
        kn = _rms(past_rows(g), kg_ref[...])
        km = jnp.concatenate([jnp.mean(kn[n * MOBA_BLOCK:(n + 1) * MOBA_BLOCK], axis=0, keepdims=True)
                              for n in range(n_past_blocks)] + [jnp.zeros((LANES - n_past_blocks, d), F32)], axis=0)
        q4 = _moba_q4(q[:, g * MOBA_REP * d:(g + 1) * MOBA_REP * d], qg_ref[...]).astype(BF16)
        sel = _moba_select(q4, km, cur, nb)
        keys = jnp.concatenate([kn, _rms(k_new[:, g * d:(g + 1) * d], kg_ref[...]), zpad], axis=0)
        vals = jnp.concatenate([past_rows(MOBA_KV + g), v_new[:, g * d:(g + 1) * d], zpad], axis=0)
        s = _dot_nt(q4, keys.astype(BF16))
        p = _masked_softmax2d(s, (_dot(sel.astype(BF16), expand) > 0.5) & causal)
        o = _dot(p.astype(BF16), vals.astype(BF16))
        for r in range(MOBA_REP):
            h = g * MOBA_REP + r
            o_ref[0, :, h * d:(h + 1) * d] = o[r * nq:(r + 1) * nq].astype(o_ref.dtype)


def moba_sample(zcs, page_table, cache_kv, k_gain, q_gain, *, t_new):
    db, n_pages = page_table.shape
    past = n_pages * PAGE_SIZE
    assert past % MOBA_BLOCK == 0 and t_new <= MOBA_BLOCK
    n_phys = cache_kv.shape[0]
    w = MOBA_KV * HEAD_DIM
    page_rows = PAGE_SIZE * 2 * MOBA_KV
    pages = cache_kv.reshape(n_phys, page_rows, HEAD_DIM)
    qw = MOBA_HEADS * HEAD_DIM
    zspec = lambda wd, off: pl.BlockSpec((1, SAMPLE_ROWS, wd), lambda b, pt: (b, 0, off // wd))
    page_spec = lambda p: pl.BlockSpec((1, page_rows, HEAD_DIM), lambda b, pt: (pt[b, p], 0, 0))
    grid_spec = pltpu.PrefetchScalarGridSpec(
        num_scalar_prefetch=1,
        grid=(db,),
        in_specs=[page_spec(p) for p in range(n_pages)]
        + [zspec(qw, C_Q), zspec(w, C_K), zspec(w, C_V),
           pl.BlockSpec((1, HEAD_DIM), lambda b, pt: (0, 0)), pl.BlockSpec((1, HEAD_DIM), lambda b, pt: (0, 0))],
        out_specs=pl.BlockSpec((1, SAMPLE_ROWS, qw), lambda b, pt: (b, 0, 0)),
    )
    return pl.pallas_call(
        functools.partial(_moba_sample_kernel, n_pages=n_pages, past=past, t_new=t_new),
        grid_spec=grid_spec,
        out_shape=jax.ShapeDtypeStruct((db, SAMPLE_ROWS, qw), BF16),
        compiler_params=pltpu.CompilerParams(dimension_semantics=("parallel",)),
        name="moba_sample",
    )(page_table, *([pages] * n_pages), zcs, zcs, zcs, k_gain.reshape(1, HEAD_DIM), q_gain.reshape(1, HEAD_DIM))


ROW_TILE = 512
GLA_TILE = 128
NSA_Q_TILE = 128


def _ab_weight_layout(w_in_ab):
    d = w_in_ab.shape[0]
    widths = (GLA_HEADS * GLA_DK, GLA_HEADS * GLA_DK, GLA_HEADS * GLA_DV, GLA_HEADS * GLA_DV, GLA_GATE_RANK,
              NSA_HEADS * HEAD_DIM, NSA_HEADS * 3, 6 * NSA_KV * HEAD_DIM)
    offs = [0]
    for w in widths:
        offs.append(offs[-1] + w)
    gq, gk, gv, gr, ga, nq, ngt, nkv = (w_in_ab[:, offs[i]:offs[i + 1]] for i in range(len(widths)))
    half = NSA_REP * 3
    zeros = lambda n: jnp.zeros((d, n), w_in_ab.dtype)
    misc0 = jnp.concatenate([ga, ngt[:, :half], zeros(LANES - GLA_GATE_RANK - half)], axis=1)
    misc1 = jnp.concatenate([zeros(GLA_GATE_RANK), ngt[:, half:], zeros(LANES - GLA_GATE_RANK - half)], axis=1)
    w = jnp.concatenate([gq, gk, gv, gr, nq, nkv, misc0, misc1, zeros(AB_TOTAL - AB_MISC1 - LANES)], axis=1)
    return w.astype(BF16)


def _pad_sample_rows(z, db, t_new):
    return jnp.pad(z.reshape(db, t_new, z.shape[-1]), ((0, 0), (0, SAMPLE_ROWS - t_new), (0, 0)))


def _conv_ffn(h, l, n_p, batch, seq, t_new, state_ffn_conv, norm_ffn, ffn_w_up, ffn_conv_w, ffn_conv_b, ffn_w_down):
    d_ff = ffn_conv_w.shape[-1]
    db = state_ffn_conv.shape[1]
    u = norm_matmul(h, norm_ffn[l], ffn_w_up[l].astype(BF16), tm=ROW_TILE, tn=1024)
    act = jnp.concatenate([
        ffn_act_prompt(u, ffn_conv_w[l], ffn_conv_b[l], batch=batch, seq=seq, tm=512, tn=d_ff // 4),
        ffn_act_sample(u, n_p, state_ffn_conv[l], ffn_conv_w[l], ffn_conv_b[l], t_new=t_new, tn=d_ff // 4)], axis=0)
    h = matmul_residual(act, ffn_w_down[l].astype(BF16), h, tm=ROW_TILE, tn=min(1024, h.shape[1]), tk=d_ff // 2)
    keep = FFN_CONV - 1
    gate_p = jnp.stack([u[(b + 1) * seq - keep:(b + 1) * seq, :d_ff] for b in range(batch)])
    gate_s = u[n_p:, :d_ff].reshape(db, t_new, d_ff)[:, t_new - keep:]
    return h, gate_p, gate_s


def kernel(x_prompt, x_sample, page_table, cache_nsa_kv, cache_nsa_win, state_gla, cache_moba_kv, state_ffn_conv, norm_mix, w_in_ab, gla_a_w2, gla_a_b, gla_o_norm, nsa_q_norm, nsa_k_norm, nsa_cmp_w1, nsa_cmp_b1, nsa_cmp_w2, nsa_cmp_b2, nsa_cmp_pe, w_out_ab, w_in_c, moba_q_norm, moba_k_norm, w_out_c, norm_ffn, ffn_w_up, ffn_conv_w, ffn_conv_b, ffn_w_down):
    batch, seq, d_model = x_prompt.shape
    db, t_new, _ = x_sample.shape
    n_p, n_s = batch * seq, db * t_new
    n = n_p + n_s
    assert norm_mix.shape[0] == 2 and w_in_ab.shape[0] == 1 and w_in_c.shape[0] == 1
    assert FFN_CONV - 1 <= t_new <= SAMPLE_ROWS and n % ROW_TILE == 0 and n_p % ROW_TILE == 0
    ffn_args = (state_ffn_conv, norm_ffn, ffn_w_up, ffn_conv_w, ffn_conv_b, ffn_w_down)

    h = jnp.concatenate([x_prompt.reshape(n_p, d_model), x_sample.reshape(n_s, d_model)], axis=0)
    nsa_cache = cache_nsa_kv.reshape(cache_nsa_kv.shape[1:])
    nsa_win = cache_nsa_win.reshape(cache_nsa_win.shape[1:])
    gla_state = state_gla.reshape(state_gla.shape[1:])
    moba_cache = cache_moba_kv.reshape(cache_moba_kv.shape[1:])
    tn_out = min(1024, d_model)

    z = norm_matmul(h, norm_mix[0], _ab_weight_layout(w_in_ab[0]), tm=ROW_TILE, tn=1024)
    zs = _pad_sample_rows(z[n_p:], db, t_new)
    w2p = jnp.pad(gla_a_w2[0], ((0, LANES - GLA_GATE_RANK), (0, 0))).astype(BF16)
    gla_zero = jnp.zeros((batch,) + state_gla.shape[2:], F32)
    zg = z if n % GLA_TILE == 0 else z[:n_p]
    og_p, gla_p = gla_mixer(zg.reshape(-1, GLA_TILE, AB_TOTAL), gla_zero, w2p, gla_a_b[0], gla_o_norm[0],
                            n_seq=batch, tiles_per_seq=seq // GLA_TILE, chunk=GLA_CHUNK, n_valid=GLA_CHUNK, out_dtype=BF16)
    og_s, gla_s = gla_mixer(zs, gla_state, w2p, gla_a_b[0], gla_o_norm[0],
                            n_seq=db, tiles_per_seq=1, chunk=SAMPLE_ROWS, n_valid=t_new, out_dtype=BF16)
    cmpw = nsa_cmp_weights(nsa_cmp_w1[0], nsa_cmp_b1[0], nsa_cmp_w2[0], nsa_cmp_b2[0], nsa_cmp_pe[0], nsa_k_norm[0])
    kc, vc = nsa_cmp_prompt(z, cmpw, batch=batch, seq=seq)
    kvp = nsa_prep_prompt(z, nsa_k_norm[0], rows=n_p, tm=ROW_TILE)
    on_p = nsa_attn_prompt(z, kvp, kc, vc, nsa_q_norm[0], batch=batch, seq=seq, tq=NSA_Q_TILE)
    on_s = nsa_sample(zs, page_table, nsa_cache, nsa_win, cmpw, nsa_k_norm[0], nsa_q_norm[0], t_new=t_new)
    mix_p = jnp.concatenate([og_p.reshape(n_p, -1), on_p], axis=1)
    mix_s = jnp.concatenate([og_s[:, :t_new].reshape(n_s, -1), on_s[:, :t_new].reshape(n_s, -1)], axis=1)
    h = matmul_residual(jnp.concatenate([mix_p, mix_s], axis=0), w_out_ab[0].astype(BF16), h,
                        tm=ROW_TILE, tn=tn_out, tk=mix_p.shape[1])
    h, conv_p0, conv_s0 = _conv_ffn(h, 0, n_p, batch, seq, t_new, *ffn_args)

    kv_w = 4 * NSA_KV * HEAD_DIM
    win_w = 2 * NSA_KV * HEAD_DIM
    win_keep = min(NSA_WINDOW, seq)
    nsa_kv_p = z[:n_p, AB_CMP:AB_CMP + kv_w].reshape(1, batch, seq, 4, NSA_KV, HEAD_DIM)
    nsa_kv_s = z[n_p:, AB_CMP:AB_CMP + kv_w].reshape(1, db, t_new, 4, NSA_KV, HEAD_DIM)
    nsa_win_p = z[:n_p, AB_WIN:AB_WIN + win_w].reshape(batch, seq, win_w)[:, seq - win_keep:]
    nsa_win_p = nsa_win_p.reshape(1, batch, win_keep, 2, NSA_KV, HEAD_DIM)
    win_new = z[n_p:, AB_WIN:AB_WIN + win_w].reshape(db, t_new, 2, NSA_KV, HEAD_DIM)
    nsa_win_s = jnp.concatenate([nsa_win, win_new], axis=1)[None, :, -NSA_WINDOW:]

    zc = norm_matmul(h, norm_mix[1], w_in_c[0].astype(BF16), tm=ROW_TILE, tn=1024)
    zcs = _pad_sample_rows(zc[n_p:], db, t_new)
    kvm, kmean = moba_prep_prompt(zc, moba_k_norm[0], rows=n_p)
    om_p = moba_attn_prompt(zc, kvm, kmean, moba_q_norm[0], batch=batch, seq=seq)
    om_s = moba_sample(zcs, page_table, moba_cache, moba_k_norm[0], moba_q_norm[0], t_new=t_new)
    om = jnp.concatenate([om_p, om_s[:, :t_new].reshape(n_s, -1)], axis=0)
    h = matmul_residual(om, w_out_c[0].astype(BF16), h, tm=ROW_TILE, tn=tn_out, tk=om.shape[1])
    h, conv_p1, conv_s1 = _conv_ffn(h, 1, n_p, batch, seq, t_new, *ffn_args)

    moba_kv_p = zc[:n_p, C_K:].reshape(1, batch, seq, 2, MOBA_KV, HEAD_DIM)
    moba_kv_s = zc[n_p:, C_K:].reshape(1, db, t_new, 2, MOBA_KV, HEAD_DIM)

    return (h[:n_p].reshape(batch, seq, d_model), h[n_p:].reshape(db, t_new, d_model),
            nsa_kv_p, nsa_kv_s, nsa_win_p, nsa_win_s, gla_p[None], gla_s[None], moba_kv_p, moba_kv_s,
            jnp.stack([conv_p0, conv_p1]), jnp.stack([conv_s0, conv_s1]))
```

```python
import functools
import math

import jax
import jax.numpy as jnp
from jax import lax
from jax.experimental import pallas as pl
from jax.experimental.pallas import tpu as pltpu

F32 = jnp.float32
BF16 = jnp.bfloat16

HEAD_DIM = 128
GLA_HEADS = 4
GLA_DK = 128
GLA_DV = 256
GLA_GATE_RANK = 16
GLA_TAU = 16.0
GLA_CHUNK = 16
NSA_HEADS = 8
NSA_KV = 2
NSA_REP = NSA_HEADS // NSA_KV
NSA_CMP_STRIDE = 16
NSA_CMP_LEN = 2 * NSA_CMP_STRIDE
NSA_CMP_HIDDEN = 256
NSA_SEL_BLOCK = 64
NSA_N_SEL = 16
NSA_WINDOW = 512
MOBA_HEADS = 16
MOBA_KV = 4
MOBA_REP = MOBA_HEADS // MOBA_KV
MOBA_BLOCK = 256
MOBA_TOPK = 3
FFN_CONV = 3
NORM_EPS = 1e-6
PAGE_SIZE = 128

LANES = 128
SUBLANES = 8
NEG = -1e30
MASK_BIAS = -(2.0 ** 100)
SAMPLE_ROWS = 8

AB_GQ, AB_GK, AB_GV, AB_GR, AB_NQ = 0, 512, 1024, 2048, 3072
AB_CMP, AB_SEL, AB_WIN, AB_MISC0, AB_MISC1, AB_TOTAL = 4096, 4608, 5120, 5632, 5760, 6144
MISC_GATE_COL = GLA_GATE_RANK


def _rms(x, g):
    return x * lax.rsqrt(jnp.mean(x * x, axis=-1, keepdims=True) + NORM_EPS) * g


def _dot_nt(a, b):
    return lax.dot_general(a, b, (((1,), (1,)), ((), ())), preferred_element_type=F32)


def _dot_tn(a, b):
    return lax.dot_general(a, b, (((0,), (0,)), ((), ())), preferred_element_type=F32)


def _dot(a, b):
    return jnp.dot(a, b, preferred_element_type=F32)


def _silu(x):
    return x * jax.nn.sigmoid(x)


def _norm_matmul_kernel(x_ref, g_ref, w_ref, o_ref, xn_ref):
    @pl.when(pl.program_id(1) == 0)
    def _():
        xn_ref[...] = _rms(x_ref[...], g_ref[...]).astype(BF16)

    o_ref[...] = _dot(xn_ref[...], w_ref[...])


def norm_matmul(x, g, w, *, tm, tn):
    n, k = x.shape
    nout = w.shape[1]
    return pl.pallas_call(
        _norm_matmul_kernel,
        grid=(n // tm, nout // tn),
        in_specs=[pl.BlockSpec((tm, k), lambda i, j: (i, 0)),
                  pl.BlockSpec((1, k), lambda i, j: (0, 0)),
                  pl.BlockSpec((k, tn), lambda i, j: (0, j))],
        out_specs=pl.BlockSpec((tm, tn), lambda i, j: (i, j)),
        out_shape=jax.ShapeDtypeStruct((n, nout), F32),
        scratch_shapes=[pltpu.VMEM((tm, k), BF16)],
        compiler_params=pltpu.CompilerParams(dimension_semantics=("parallel", "arbitrary")),
        name="norm_matmul",
    )(x, g.reshape(1, k), w)


def _matmul_res_kernel(a_ref, w_ref, r_ref, o_ref):
    d = _dot(a_ref[...], w_ref[...])

    @pl.when(pl.program_id(2) == 0)
    def _():
        o_ref[...] = r_ref[...] + d

    @pl.when(pl.program_id(2) > 0)
    def _():
        o_ref[...] += d


def matmul_residual(a, w, res, *, tm, tn, tk):
    n, k = a.shape
    nout = w.shape[1]
    return pl.pallas_call(
        _matmul_res_kernel,
        grid=(n // tm, nout // tn, k // tk),
        in_specs=[pl.BlockSpec((tm, tk), lambda i, j, kk: (i, kk)),
                  pl.BlockSpec((tk, tn), lambda i, j, kk: (kk, j)),
                  pl.BlockSpec((tm, tn), lambda i, j, kk: (i, j))],
        out_specs=pl.BlockSpec((tm, tn), lambda i, j, kk: (i, j)),
        out_shape=jax.ShapeDtypeStruct((n, nout), F32),
        compiler_params=pltpu.CompilerParams(dimension_semantics=("parallel", "parallel", "arbitrary")),
        name="matmul_residual",
    )(a, w, res)


def _conv_act(g, p1, p2, val, cw_ref, cb_ref):
    c = cb_ref[...] + cw_ref[0:1, :] * p2 + cw_ref[1:2, :] * p1 + cw_ref[2:3, :] * g
    return (_silu(c) * val).astype(BF16)


def _ffn_act_prompt_kernel(g_ref, v_ref, halo_ref, cw_ref, cb_ref, o_ref):
    g = g_ref[...]
    tm = g.shape[0]
    halo = jnp.where(pl.program_id(1) == 0, 0.0, halo_ref[...])
    row = lax.broadcasted_iota(jnp.int32, (tm, 1), 0)
    p1 = jnp.where(row == 0, halo[7:8, :], pltpu.roll(g, 1, 0))
    p2 = jnp.where(row == 0, halo[6:7, :], jnp.where(row == 1, halo[7:8, :], pltpu.roll(g, 2, 0)))
    o_ref[...] = _conv_act(g, p1, p2, v_ref[...], cw_ref, cb_ref)


def ffn_act_prompt(u, conv_w, conv_b, *, batch, seq, tm, tn):
    d_ff = u.shape[1] // 2
    nj = d_ff // tn
    nt = seq // tm
    hb = tm // SUBLANES
    return pl.pallas_call(
        _ffn_act_prompt_kernel,
        grid=(batch, nt, nj),
        in_specs=[pl.BlockSpec((tm, tn), lambda b, i, j: (b * nt + i, j)),
                  pl.BlockSpec((tm, tn), lambda b, i, j: (b * nt + i, nj + j)),
                  pl.BlockSpec((SUBLANES, tn), lambda b, i, j: (jnp.maximum((b * nt + i) * hb - 1, 0), j)),
                  pl.BlockSpec((FFN_CONV, tn), lambda b, i, j: (0, j)),
                  pl.BlockSpec((1, tn), lambda b, i, j: (0, j))],
        out_specs=pl.BlockSpec((tm, tn), lambda b, i, j: (b * nt + i, j)),
        out_shape=jax.ShapeDtypeStruct((batch * seq, d_ff), BF16),
        compiler_params=pltpu.CompilerParams(dimension_semantics=("parallel", "parallel", "parallel")),
        name="ffn_act_prompt",
    )(u, u, u, conv_w, conv_b.reshape(1, d_ff))


def _ffn_act_sample_kernel(g_ref, v_ref, s1_ref, s2_ref, cw_ref, cb_ref, o_ref, *, t_new):
    g = g_ref[...]
    tm = g.shape[0]
    t = lax.broadcasted_iota(jnp.int32, (tm, 1), 0) % t_new
    p1 = jnp.where(t >= 1, pltpu.roll(g, 1, 0), s1_ref[...])
    p2 = jnp.where(t >= 2, pltpu.roll(g, 2, 0), s2_ref[...])
    o_ref[...] = _conv_act(g, p1, p2, v_ref[...], cw_ref, cb_ref)


def ffn_act_sample(u, row0, conv_state, conv_w, conv_b, *, t_new, tn):
    d_ff = u.shape[1] // 2
    nj = d_ff // tn
    db = conv_state.shape[0]
    tm = db * t_new
    rb = row0 // tm
    zero = jnp.zeros((db, t_new - 1, d_ff), F32)
    s1 = jnp.concatenate([conv_state[:, 1:2], zero], axis=1).reshape(tm, d_ff)
    s2 = jnp.concatenate([conv_state, jnp.zeros((db, t_new - 2, d_ff), F32)], axis=1).reshape(tm, d_ff)
    return pl.pallas_call(
        functools.partial(_ffn_act_sample_kernel, t_new=t_new),
        grid=(nj,),
        in_specs=[pl.BlockSpec((tm, tn), lambda j: (rb, j)),
                  pl.BlockSpec((tm, tn), lambda j: (rb, nj + j)),
                  pl.BlockSpec((tm, tn), lambda j: (0, j)),
                  pl.BlockSpec((tm, tn), lambda j: (0, j)),
                  pl.BlockSpec((FFN_CONV, tn), lambda j: (0, j)),
                  pl.BlockSpec((1, tn), lambda j: (0, j))],
        out_specs=pl.BlockSpec((tm, tn), lambda j: (0, j)),
        out_shape=jax.ShapeDtypeStruct((tm, d_ff), BF16),
        compiler_params=pltpu.CompilerParams(dimension_semantics=("parallel",)),
        name="ffn_act_sample",
    )(u, u, s1, s2, conv_w, conv_b.reshape(1, d_ff))


def _ffn_up_kernel(x_ref, g_ref, wg_ref, wv_ref, cw_ref, cb_ref, s1_ref, s2_ref, act_ref, tail_ref, gs_ref,
                   xn_ref, carry_ref, *, tiles_per_seq, n_prompt_tiles, t_new):
    i, j = pl.program_id(0), pl.program_id(1)

    @pl.when(j == 0)
    def _():
        xn_ref[...] = _rms(x_ref[...], g_ref[...]).astype(BF16)

    @pl.when(i == 0)
    def _():
        carry_ref[j] = jnp.zeros(carry_ref.shape[1:], F32)

    xn = xn_ref[...]
    gate = _dot(xn, wg_ref[...])
    val = _dot(xn, wv_ref[...])
    tm = gate.shape[0]
    tail = gate[tm - SUBLANES:, :]
    tail_ref[...] = tail
    gs_ref[...] = gate
    halo = jnp.where(i % tiles_per_seq == 0, 0.0, carry_ref[j])
    carry_ref[j] = tail
    row = lax.broadcasted_iota(jnp.int32, (tm, 1), 0)
    r1 = pltpu.roll(gate, 1, 0)
    r2 = pltpu.roll(gate, 2, 0)
    p1 = jnp.where(row == 0, halo[7:8, :], r1)
    p2 = jnp.where(row == 0, halo[6:7, :], jnp.where(row == 1, halo[7:8, :], r2))
    t = row % t_new
    is_sample = i >= n_prompt_tiles
    p1 = jnp.where(is_sample, jnp.where(t >= 1, r1, s1_ref[...]), p1)
    p2 = jnp.where(is_sample, jnp.where(t >= 2, r2, s2_ref[...]), p2)
    act_ref[...] = _conv_act(gate, p1, p2, val, cw_ref, cb_ref)


def ffn_up_act(h, norm_g, w_up, conv_w, conv_b, conv_state, *, batch, seq, t_new, tm, tn):
    n, k = h.shape
    d_ff = w_up.shape[1] // 2
    nj = d_ff // tn
    db = conv_state.shape[0]
    n_s = db * t_new
    assert n_s == tm and seq % tm == 0 and n == batch * seq + n_s
    n_tiles = n // tm
    n_prompt_tiles = n_tiles - 1
    s1 = jnp.concatenate([conv_state[:, 1:2], jnp.zeros((db, t_new - 1, d_ff), F32)], axis=1).reshape(n_s, d_ff)
    s2 = jnp.concatenate([conv_state, jnp.zeros((db, t_new - 2, d_ff), F32)], axis=1).reshape(n_s, d_ff)
    sample_col = lambda i, j: (0, jnp.where(i == n_prompt_tiles, j, 0))
    return pl.pallas_call(
        functools.partial(_ffn_up_kernel, tiles_per_seq=seq // tm, n_prompt_tiles=n_prompt_tiles, t_new=t_new),
        grid=(n_tiles, nj),
        in_specs=[pl.BlockSpec((tm, k), lambda i, j: (i, 0)),
                  pl.BlockSpec((1, k), lambda i, j: (0, 0)),
                  pl.BlockSpec((k, tn), lambda i, j: (0, j)),
                  pl.BlockSpec((k, tn), lambda i, j: (0, nj + j)),
                  pl.BlockSpec((FFN_CONV, tn), lambda i, j: (0, j)),
                  pl.BlockSpec((1, tn), lambda i, j: (0, j)),
                  pl.BlockSpec((tm, tn), sample_col),
                  pl.BlockSpec((tm, tn), sample_col)],
        out_specs=[pl.BlockSpec((tm, tn), lambda i, j: (i, j)),
                   pl.BlockSpec((SUBLANES, tn), lambda i, j: (i, j)),
                   pl.BlockSpec((tm, tn), sample_col)],
        out_shape=[jax.ShapeDtypeStruct((n, d_ff), BF16),
                   jax.ShapeDtypeStruct((n_tiles * SUBLANES, d_ff), F32),
                   jax.ShapeDtypeStruct((tm, d_ff), F32)],
        scratch_shapes=[pltpu.VMEM((tm, k), BF16), pltpu.VMEM((nj, SUBLANES, tn), F32)],
        compiler_params=pltpu.CompilerParams(dimension_semantics=("arbitrary", "arbitrary")),
        name="ffn_up_act",
    )(h, norm_g.reshape(1, k), w_up, w_up, conv_w, conv_b.reshape(1, d_ff), s1, s2)


def _gla_kernel(q_ref, k_ref, v_ref, r_ref, m_ref, w2_ref, b2_ref, on_ref, s0_ref, o_ref, sout_ref,
                st_ref, cum_ref, *, chunk, n_valid, nsub):
    i = pl.program_id(1)
    tt = nsub * chunk

    @pl.when(i == 0)
    def _():
        for h in range(GLA_HEADS):
            st_ref[h] = s0_ref[0, h].T

    a = _dot(m_ref[0].astype(BF16), w2_ref[...]) + b2_ref[...]
    a = (jnp.minimum(a, 0.0) - jnp.log(1.0 + jnp.exp(-jnp.abs(a)))) / GLA_TAU
    pos = lax.broadcasted_iota(jnp.int32, (tt, 1), 0) % chunk
    if n_valid < chunk:
        a = jnp.where(pos < n_valid, a, 0.0)
    cum = a
    sh = 1
    while sh < chunk:
        cum = cum + jnp.where(pos >= sh, pltpu.roll(cum, sh, 0), 0.0)
        sh *= 2
    cum_ref[...] = cum

    ti = lax.broadcasted_iota(jnp.int32, (chunk, 1), 0)

    def step(c, carry):
        rows = pl.ds(pl.multiple_of(c * chunk, chunk), chunk)
        for h in range(GLA_HEADS):
            kcols = slice(h * GLA_DK, (h + 1) * GLA_DK)
            vcols = slice(h * GLA_DV, (h + 1) * GLA_DV)
            qh = q_ref[0, rows, kcols] * GLA_DK ** -0.5
            kh = k_ref[0, rows, kcols]
            vh = v_ref[0, rows, vcols]
            ch = cum_ref[rows, kcols]
            st = st_ref[h]
            o = _dot_nt((qh * jnp.exp(ch)).astype(BF16), st.astype(BF16))
            for s in range(chunk):
                d = jnp.exp(jnp.where(ti >= s, ch - ch[s:s + 1, :], NEG))
                w = jnp.sum(qh * kh[s:s + 1, :] * d, axis=-1, keepdims=True)
                o = o + w * vh[s:s + 1, :]
            last = ch[chunk - 1:chunk, :]
            kt = kh * jnp.exp(last - ch)
            st_ref[h] = st * jnp.exp(last) + _dot_tn(vh.astype(BF16), kt.astype(BF16))
            rh = r_ref[0, rows, vcols]
            o_ref[0, rows, vcols] = (_rms(o, on_ref[...]) * _silu(rh)).astype(o_ref.dtype)
        return carry

    lax.fori_loop(0, nsub, step, 0)

    @pl.when(i == pl.num_programs(1) - 1)
    def _():
        for h in range(GLA_HEADS):
            sout_ref[0, h] = st_ref[h].T


def gla_mixer(z3, s0, w2p, b2, onorm, *, n_seq, tiles_per_seq, chunk, n_valid, out_dtype):
    tt = z3.shape[1]
    nsub = tt // chunk
    dqk = GLA_HEADS * GLA_DK
    dv = GLA_HEADS * GLA_DV
    tile = lambda b, i: b * tiles_per_seq + i
    return pl.pallas_call(
        functools.partial(_gla_kernel, chunk=chunk, n_valid=n_valid, nsub=nsub),
        grid=(n_seq, tiles_per_seq),
        in_specs=[pl.BlockSpec((1, tt, dqk), lambda b, i: (tile(b, i), 0, AB_GQ // dqk)),
                  pl.BlockSpec((1, tt, dqk), lambda b, i: (tile(b, i), 0, AB_GK // dqk)),
                  pl.BlockSpec((1, tt, dv), lambda b, i: (tile(b, i), 0, AB_GV // dv)),
                  pl.BlockSpec((1, tt, dv), lambda b, i: (tile(b, i), 0, AB_GR // dv)),
                  pl.BlockSpec((1, tt, LANES), lambda b, i: (tile(b, i), 0, AB_MISC0 // LANES)),
                  pl.BlockSpec((LANES, dqk), lambda b, i: (0, 0)),
                  pl.BlockSpec((1, dqk), lambda b, i: (0, 0)),
                  pl.BlockSpec((1, GLA_DV), lambda b, i: (0, 0)),
                  pl.BlockSpec((1, GLA_HEADS, GLA_DK, GLA_DV), lambda b, i: (b, 0, 0, 0))],
        out_specs=[pl.BlockSpec((1, tt, dv), lambda b, i: (tile(b, i), 0, 0)),
                   pl.BlockSpec((1, GLA_HEADS, GLA_DK, GLA_DV), lambda b, i: (b, 0, 0, 0))],
        out_shape=[jax.ShapeDtypeStruct((n_seq * tiles_per_seq, tt, dv), out_dtype),
                   jax.ShapeDtypeStruct((n_seq, GLA_HEADS, GLA_DK, GLA_DV), F32)],
        scratch_shapes=[pltpu.VMEM((GLA_HEADS, GLA_DV, GLA_DK), F32), pltpu.VMEM((tt, dqk), F32)],
        compiler_params=pltpu.CompilerParams(dimension_semantics=("parallel", "arbitrary")),
        name="gla_mixer",
    )(z3, z3, z3, z3, z3, w2p, b2.reshape(1, dqk), onorm.reshape(1, GLA_DV), s0)


def _masked_softmax_rows(s, mask):
    sm = jnp.where(mask[None], s, NEG)
    m = jnp.max(sm, axis=-1, keepdims=True)
    e = jnp.where(mask[None], jnp.exp(sm - m), 0.0)
    d = jnp.sum(e, axis=-1, keepdims=True)
    return e / jnp.where(d > 0, d, 1.0)


def _dot_exact_rhs(x, m):
    hi = x.astype(BF16)
    r1 = x - hi.astype(F32)
    mid = r1.astype(BF16)
    lo = (r1 - mid.astype(F32)).astype(BF16)
    return _dot(hi, m) + _dot(mid, m) + _dot(lo, m)


def _topk_mask(score, n_rounds):
    lane = lax.broadcasted_iota(jnp.int32, score.shape, 1).astype(F32)
    sel = jnp.zeros(score.shape, F32)
    work = score
    for _ in range(n_rounds):
        mx = jnp.max(work, axis=-1, keepdims=True)
        first = jnp.min(jnp.where(work == mx, lane, 1e9), axis=-1, keepdims=True)
        pick = lane == first
        sel = jnp.where(pick & (mx > 0.5 * NEG), 1.0, sel)
        work = jnp.where(pick, 2.0 * NEG, work)
    return sel


def _block_expand(n_blocks_pad, n_keys, block, key0=0):
    bi = lax.broadcasted_iota(jnp.int32, (n_blocks_pad, n_keys), 0)
    ki = lax.broadcasted_iota(jnp.int32, (n_blocks_pad, n_keys), 1) + key0
    return (bi == jnp.right_shift(ki, int(math.log2(block)))).astype(BF16)


def _gelu_tanh(x):
    return x * (0.5 * (1.0 + jnp.tanh(math.sqrt(2.0 / math.pi) * (x + 0.044715 * (x * x * x)))))


def _cmp_mlp(chunk_rows, n_ch, w1_ref, cpe_ref, cb1_ref, w2_ref, cb2_ref, kg_ref):
    hid_w = NSA_CMP_HIDDEN
    out = {}
    for e in range(2):
        accp = _dot(cpe_ref[e], w1_ref[e])
        pe = accp[0:1, :hid_w] + accp[1:2, hid_w:] + cb1_ref[e]
        x = jnp.concatenate([chunk_rows(e * NSA_KV + g) for g in range(NSA_KV)], axis=0).astype(BF16)
        acc = _dot(x, w1_ref[e])
        hid = []
        for g in range(NSA_KV):
            a = acc[g * n_ch:(g + 1) * n_ch]
            hid.append(_gelu_tanh(a[:, :hid_w] + pltpu.roll(a[:, hid_w:], n_ch - 1, 0) + pe))
        ckv = _dot(jnp.concatenate(hid, axis=0).astype(BF16), w2_ref[e]) + cb2_ref[e]
        for g in range(NSA_KV):
            out[e, g] = ckv[g * n_ch:(g + 1) * n_ch]
    kc = [_rms(out[0, g], kg_ref[...]) for g in range(NSA_KV)]
    vc = [out[1, g] for g in range(NSA_KV)]
    return kc, vc


def _nsa_q4(q, qg):
    return jnp.concatenate([_rms(q[:, r * HEAD_DIM:(r + 1) * HEAD_DIM], qg) * HEAD_DIM ** -0.5
                            for r in range(NSA_REP)], axis=0)


def _nsa_cmp_branch(q4, kc, vc, qpos, n_cmp):
    nq = qpos.shape[0]
    n_ch = kc.shape[0]
    sc = _dot_nt(q4, kc.astype(BF16)).reshape(NSA_REP, nq, n_ch)
    cidx = lax.broadcasted_iota(jnp.int32, (1, n_ch), 1)
    cmask = (NSA_CMP_STRIDE * cidx + NSA_CMP_LEN - 1 <= qpos) & (cidx < n_cmp)
    p = _masked_softmax_rows(sc, cmask)
    o = _dot(p.reshape(NSA_REP * nq, n_ch).astype(BF16), vc.astype(BF16))
    return o, jnp.sum(p, axis=0)


def _nsa_select(pcs, qpos0, n_sb):
    n_ch = pcs.shape[1]
    ratio = NSA_SEL_BLOCK // NSA_CMP_STRIDE
    bi = lax.broadcasted_iota(jnp.int32, (LANES, n_ch), 0)
    ci = lax.broadcasted_iota(jnp.int32, (LANES, n_ch), 1)
    mimp = ((ci >= ratio * bi - 1) & (ci <= ratio * bi + ratio - 1)).astype(BF16)
    hi = pcs.astype(BF16)
    r1 = pcs - hi.astype(F32)
    mid = r1.astype(BF16)
    lo = (r1 - mid.astype(F32)).astype(BF16)
    imp = _dot_nt(mimp, hi) + _dot_nt(mimp, mid) + _dot_nt(mimp, lo)
    nbp = -(-n_sb // SUBLANES) * SUBLANES
    blk = lax.broadcasted_iota(jnp.int32, (nbp, 1), 0)
    qpos = qpos0 + lax.broadcasted_iota(jnp.int32, (1, LANES), 1)
    cur = jnp.right_shift(qpos, int(math.log2(NSA_SEL_BLOCK)))
    valid = (blk <= cur) & (blk < n_sb)
    forced = (blk == 0) | (blk == cur) | (blk == cur - 1)
    score = jnp.where(forced, -NEG, jnp.where(valid, imp[:nbp], NEG))
    rank = jnp.zeros((nbp, LANES), F32)
    for k in range(n_sb):
        sk = score[k:k + 1, :]
        rank = rank + jnp.where((sk > score) | ((sk == score) & (blk > k)), 1.0, 0.0)
    sel_t = jnp.where((rank < min(NSA_N_SEL, n_sb)) & (score > 0.5 * NEG), 1.0, 0.0)
    sel_t = jnp.concatenate([sel_t, jnp.zeros((LANES - nbp, LANES), F32)], axis=0)
    return sel_t.T


def _nsa_window_mask(kpos, qpos):
    return (kpos <= qpos) & (qpos - kpos < NSA_WINDOW) & (kpos >= 0)


def _nsa_gate_mix(gsig, o_cmp, o_sel, o_win, nq):
    outs = []
    for r in range(NSA_REP):
        c0 = MISC_GATE_COL + 3 * r
        rows = slice(r * nq, (r + 1) * nq)
        outs.append(gsig[:, c0:c0 + 1] * o_cmp[rows] + gsig[:, c0 + 1:c0 + 2] * o_sel[rows]
                    + gsig[:, c0 + 2:c0 + 3] * o_win[rows])
    return jnp.concatenate(outs, axis=1)


def _nsa_cmp_kernel(*refs, n_ch):
    x_refs = refs[:NSA_CMP_STRIDE]
    w1_ref, cpe_ref, cb1_ref, w2_ref, cb2_ref, kg_ref, kc_ref, vc_ref = refs[NSA_CMP_STRIDE:]

    def chunk_rows(eg):
        return jnp.concatenate([x_refs[s][:, eg * HEAD_DIM:(eg + 1) * HEAD_DIM] for s in range(NSA_CMP_STRIDE)], axis=1)

    kc, vc = _cmp_mlp(chunk_rows, n_ch, w1_ref, cpe_ref, cb1_ref, w2_ref, cb2_ref, kg_ref)
    for g in range(NSA_KV):
        kc_ref[0, g] = kc[g]
        vc_ref[0, g] = vc[g]


def _cmp_weight_specs():
    z = (0,) * 8
    full = lambda shape: pl.BlockSpec(shape, lambda *a: z[:len(shape)])
    taps = NSA_CMP_STRIDE * HEAD_DIM
    return [full((2, taps, 2 * NSA_CMP_HIDDEN)), full((2, SUBLANES, taps)),
            full((2, 1, NSA_CMP_HIDDEN)), full((2, NSA_CMP_HIDDEN, HEAD_DIM)), full((2, 1, HEAD_DIM)),
            full((1, HEAD_DIM))]


def nsa_cmp_prompt(z, cmpw, *, batch, seq):
    n_ch = seq // NSA_CMP_STRIDE
    seg = 4 * HEAD_DIM
    st = NSA_CMP_STRIDE
    zc = z[:batch * seq, AB_CMP:AB_CMP + seg].reshape(batch * n_ch, st * seg)
    shp = jax.ShapeDtypeStruct((batch, NSA_KV, n_ch, HEAD_DIM), F32)
    ospec = pl.BlockSpec((1, NSA_KV, n_ch, HEAD_DIM), lambda b: (b, 0, 0, 0))
    xspec = lambda s: pl.BlockSpec((n_ch, seg), lambda b: (b, s))
    return pl.pallas_call(
        functools.partial(_nsa_cmp_kernel, n_ch=n_ch),
        grid=(batch,),
        in_specs=[xspec(s) for s in range(st)] + _cmp_weight_specs(),
        out_specs=[ospec, ospec],
        out_shape=[shp, shp],
        compiler_params=pltpu.CompilerParams(dimension_semantics=("parallel",)),
        name="nsa_cmp_prompt",
    )(*([zc] * st), *cmpw)


def _nsa_prep_kernel(sel_ref, win_ref, kg_ref, o_ref):
    d = HEAD_DIM
    parts = []
    for ref, row in ((sel_ref, 1), (win_ref, 2)):
        x = ref[...]
        for g in range(NSA_KV):
            parts.append(_rms(x[:, g * d:(g + 1) * d], kg_ref[row:row + 1, :]))
        parts.append(x[:, NSA_KV * d:])
    o_ref[...] = jnp.concatenate(parts, axis=1).astype(BF16)


def nsa_prep_prompt(z, k_gain, *, rows, tm):
    seg = 4 * HEAD_DIM
    return pl.pallas_call(
        _nsa_prep_kernel,
        grid=(rows // tm,),
        in_specs=[pl.BlockSpec((tm, seg), lambda i: (i, AB_SEL // seg)),
                  pl.BlockSpec((tm, seg), lambda i: (i, AB_WIN // seg)),
                  pl.BlockSpec((3, HEAD_DIM), lambda i: (0, 0))],
        out_specs=pl.BlockSpec((tm, 2 * seg), lambda i: (i, 0)),
        out_shape=jax.ShapeDtypeStruct((rows, 2 * seg), BF16),
        compiler_params=pltpu.CompilerParams(dimension_semantics=("parallel",)),
        name="nsa_prep_prompt",
    )(z, z, k_gain)


def _nsa_attn_kernel(q_ref, m_ref, kc_ref, vc_ref, ks_ref, vs_ref, kw_ref, vw_ref, oh_ref, qg_ref, o_ref, *, seq, kt):
    i = pl.program_id(2)
    nq = q_ref.shape[0]
    assert nq == LANES
    q4 = _nsa_q4(q_ref[...], qg_ref[...]).astype(BF16)
    qpos = i * nq + lax.broadcasted_iota(jnp.int32, (nq, 1), 0)
    n_ch = seq // NSA_CMP_STRIDE
    n_sb = -(-seq // NSA_SEL_BLOCK)

    o_cmp, pcs = _nsa_cmp_branch(q4, kc_ref[0, 0], vc_ref[0, 0], qpos, n_ch - 1)
    sel = _nsa_select(pcs, i * nq, n_sb)
    bias = jnp.where(sel > 0.5, 0.0, MASK_BIAS).astype(BF16)
    q_aug = jnp.concatenate([q4, jnp.concatenate([bias] * NSA_REP, axis=0)], axis=1)

    def sel_step(j, carry, causal):
        m, l, acc = carry
        rows = pl.ds(pl.multiple_of(j * kt, kt), kt)
        k_aug = jnp.concatenate([ks_ref[rows, :], oh_ref[rows, :]], axis=1)
        s = _dot_nt(q_aug, k_aug).reshape(NSA_REP, nq, kt)
        if causal:
            kpos = j * kt + lax.broadcasted_iota(jnp.int32, (1, kt), 1)
            s = jnp.where((kpos <= qpos)[None], s, NEG)
        mn = jnp.maximum(m, jnp.max(s, axis=-1, keepdims=True))
        p = jnp.exp(s - mn)
        alpha = jnp.exp(m - mn)
        l = alpha * l + jnp.sum(p, axis=-1, keepdims=True)
        pv = _dot(p.reshape(NSA_REP * nq, kt).astype(BF16), vs_ref[rows, :])
        return mn, l, alpha * acc + pv.reshape(NSA_REP, nq, HEAD_DIM)

    last = (i * nq + nq - 1) // kt
    init = (jnp.full((NSA_REP, nq, 1), NEG, F32), jnp.zeros((NSA_REP, nq, 1), F32),
            jnp.zeros((NSA_REP, nq, HEAD_DIM), F32))
    carry = lax.fori_loop(0, last, functools.partial(sel_step, causal=False), init)
    _, l, acc = sel_step(last, carry, True)
    o_sel = (acc / l).reshape(NSA_REP * nq, HEAD_DIM)

    span = min(NSA_WINDOW + nq, seq)
    start = pl.multiple_of(jnp.clip(i * nq - NSA_WINDOW, 0, seq - span), nq)
    s = _dot_nt(q4, kw_ref[pl.ds(start, span), :]).reshape(NSA_REP, nq, span)
    kpos = start + lax.broadcasted_iota(jnp.int32, (1, span), 1)
    p = _masked_softmax_rows(s, _nsa_window_mask(kpos, qpos))
    o_win = _dot(p.reshape(NSA_REP * nq, span).astype(BF16), vw_ref[pl.ds(start, span), :])

    o_ref[...] = _nsa_gate_mix(jax.nn.sigmoid(m_ref[...]), o_cmp, o_sel, o_win, nq).astype(o_ref.dtype)


def nsa_attn_prompt(z, kvp, kc, vc, q_gain, *, batch, seq, tq):
    nt = seq // tq
    gw = NSA_REP * HEAD_DIM
    kt = min(4 * tq, seq)
    row = lambda b, g, i: b * nt + i
    cspec = pl.BlockSpec((1, 1, seq // NSA_CMP_STRIDE, HEAD_DIM), lambda b, g, i: (b, g, 0, 0))
    kvspec = lambda c: pl.BlockSpec((seq, HEAD_DIM), lambda b, g, i: (b, c + g))
    return pl.pallas_call(
        functools.partial(_nsa_attn_kernel, seq=seq, kt=kt),
        grid=(batch, NSA_KV, nt),
        in_specs=[pl.BlockSpec((tq, gw), lambda b, g, i: (row(b, g, i), AB_NQ // gw + g)),
                  pl.BlockSpec((tq, LANES), lambda b, g, i: (row(b, g, i), AB_MISC0 // LANES + g)),
                  cspec, cspec, kvspec(0), kvspec(2), kvspec(4), kvspec(6),
                  pl.BlockSpec((seq, LANES), lambda b, g, i: (0, 0)),
                  pl.BlockSpec((1, HEAD_DIM), lambda b, g, i: (0, 0))],
        out_specs=pl.BlockSpec((tq, gw), lambda b, g, i: (row(b, g, i), g)),
        out_shape=jax.ShapeDtypeStruct((batch * seq, NSA_KV * gw), BF16),
        compiler_params=pltpu.CompilerParams(dimension_semantics=("parallel", "parallel", "arbitrary")),
        name="nsa_attn_prompt",
    )(z, z, kc, vc, kvp, kvp, kvp, kvp, _block_onehot(seq, NSA_SEL_BLOCK), q_gain.reshape(1, HEAD_DIM))


def _block_onehot(n_keys, block):
    k = lax.broadcasted_iota(jnp.int32, (n_keys, LANES), 0) // block
    return (k == lax.broadcasted_iota(jnp.int32, (n_keys, LANES), 1)).astype(BF16)


def nsa_cmp_weights(cw1, cb1, cw2, cb2, cpe, k_gain):
    st = NSA_CMP_STRIDE
    taps = st * HEAD_DIM
    w1cat = jnp.concatenate([cw1[:, :st], cw1[:, st:]], axis=-1).reshape(2, taps, 2 * NSA_CMP_HIDDEN).astype(BF16)
    pe_rows = jnp.stack([cpe[:, :st].reshape(2, taps), cpe[:, st:].reshape(2, taps)], axis=1)
    pe_rows = jnp.pad(pe_rows, ((0, 0), (0, SUBLANES - 2), (0, 0))).astype(BF16)
    return (w1cat, pe_rows, cb1.reshape(2, 1, NSA_CMP_HIDDEN), cw2.astype(BF16), cb2.reshape(2, 1, HEAD_DIM),
            k_gain[0].reshape(1, HEAD_DIM))


NSA_ROW = 4 * NSA_KV * HEAD_DIM
CHUNKS_PER_PAGE = PAGE_SIZE // NSA_CMP_STRIDE


def _nsa_sample_kernel(pt_ref, *refs, n_pages, past, t_new):
    del pt_ref
    pages = refs[:n_pages]
    (q_ref, sel_ref, win_ref, m0_ref, m1_ref, wb_ref, w1_ref, cpe_ref, cb1_ref, w2_ref, cb2_ref, kg0_ref,
     kg_ref, qg_ref, o_ref) = refs[n_pages:]
    nq = SAMPLE_ROWS
    d = HEAD_DIM
    n_ch = n_pages * CHUNKS_PER_PAGE
    n_cmp = (past + t_new) // NSA_CMP_STRIDE - 1
    n_sb = -(-(past + t_new) // NSA_SEL_BLOCK)
    n_sl = NSA_ROW // d
    n_wsl = 2 * NSA_KV

    def chunk_rows(j):
        return jnp.concatenate(
            [jnp.concatenate([pg[0, pl.ds(s * n_sl + j, CHUNKS_PER_PAGE, stride=NSA_CMP_STRIDE * n_sl), :]
                              for s in range(NSA_CMP_STRIDE)], axis=1) for pg in pages], axis=0)

    def past_rows(j):
        return jnp.concatenate([pg[0, pl.ds(j, PAGE_SIZE, stride=n_sl), :] for pg in pages], axis=0)

    kc, vc = _cmp_mlp(chunk_rows, n_ch, w1_ref, cpe_ref, cb1_ref, w2_ref, cb2_ref, kg0_ref)

    qpos = past + lax.broadcasted_iota(jnp.int32, (nq, 1), 0)
    nk = past + LANES
    kpos = lax.broadcasted_iota(jnp.int32, (1, nk), 1)
    esel = _block_expand(LANES, nk, NSA_SEL_BLOCK)
    causal = kpos <= qpos
    nw = NSA_WINDOW + LANES
    wmask = _nsa_window_mask(past - NSA_WINDOW + lax.broadcasted_iota(jnp.int32, (1, nw), 1), qpos)
    zpad = jnp.zeros((LANES - nq, d), F32)
    q = q_ref[0]
    sel_new = sel_ref[0]
    win_new = win_ref[0]
    kg_sel = kg_ref[1:2, :]
    kg_win = kg_ref[2:3, :]
    col = lambda x, c: x[:, c * d:(c + 1) * d]
    wb_rows = lambda j: wb_ref[0, pl.ds(j, NSA_WINDOW, stride=n_wsl), :]

    for g in range(NSA_KV):
        q4 = _nsa_q4(q[:, g * NSA_REP * d:(g + 1) * NSA_REP * d], qg_ref[...]).astype(BF16)
        o_cmp, pcs = _nsa_cmp_branch(q4, kc[g], vc[g], qpos, n_cmp)
        pcs = jnp.concatenate([pcs, jnp.zeros((LANES - nq, n_ch), F32)], axis=0)
        selb = _nsa_select(pcs, past, n_sb)[:nq].astype(BF16)

        ks = jnp.concatenate([_rms(past_rows(2 * NSA_KV + g), kg_sel), _rms(col(sel_new, g), kg_sel), zpad], axis=0)
        vs = jnp.concatenate([past_rows(3 * NSA_KV + g), col(sel_new, NSA_KV + g), zpad], axis=0)
        s = _dot_nt(q4, ks.astype(BF16)).reshape(NSA_REP, nq, nk)
        p = _masked_softmax_rows(s, (_dot(selb, esel) > 0.5) & causal)
        o_sel = _dot(p.reshape(NSA_REP * nq, nk).astype(BF16), vs.astype(BF16))

        kw = jnp.concatenate([_rms(wb_rows(g), kg_win), _rms(col(win_new, g), kg_win), zpad], axis=0)
        vw = jnp.concatenate([wb_rows(NSA_KV + g), col(win_new, NSA_KV + g), zpad], axis=0)
        s = _dot_nt(q4, kw.astype(BF16)).reshape(NSA_REP, nq, nw)
        p = _masked_softmax_rows(s, wmask)
        o_win = _dot(p.reshape(NSA_REP * nq, nw).astype(BF16), vw.astype(BF16))

        gsig = jax.nn.sigmoid((m0_ref, m1_ref)[g][0])
        o_ref[0, :, g * NSA_REP * d:(g + 1) * NSA_REP * d] = _nsa_gate_mix(gsig, o_cmp, o_sel, o_win, nq).astype(o_ref.dtype)


def nsa_sample(zs, page_table, cache_kv, cache_win, cmpw, k_gain, q_gain, *, t_new):
    db, n_pages = page_table.shape
    past = n_pages * PAGE_SIZE
    assert cache_win.shape[1] == NSA_WINDOW and (past + t_new) // NSA_CMP_STRIDE == n_pages * CHUNKS_PER_PAGE
    n_phys = cache_kv.shape[0]
    page_rows = PAGE_SIZE * NSA_ROW // HEAD_DIM
    win_rows = NSA_WINDOW * 2 * NSA_KV
    pages = cache_kv.reshape(n_phys, page_rows, HEAD_DIM)
    wb = cache_win.reshape(db, win_rows, HEAD_DIM)
    seg = 4 * HEAD_DIM
    qw = NSA_HEADS * HEAD_DIM
    zspec = lambda w, off: pl.BlockSpec((1, SAMPLE_ROWS, w), lambda b, pt: (b, 0, off // w))
    page_spec = lambda p: pl.BlockSpec((1, page_rows, HEAD_DIM), lambda b, pt: (pt[b, p], 0, 0))
    grid_spec = pltpu.PrefetchScalarGridSpec(
        num_scalar_prefetch=1,
        grid=(db,),
        in_specs=[page_spec(p) for p in range(n_pages)]
        + [zspec(qw, AB_NQ), zspec(seg, AB_SEL), zspec(seg, AB_WIN), zspec(LANES, AB_MISC0), zspec(LANES, AB_MISC1),
           pl.BlockSpec((1, win_rows, HEAD_DIM), lambda b, pt: (b, 0, 0))]
        + _cmp_weight_specs()
        + [pl.BlockSpec((3, HEAD_DIM), lambda b, pt: (0, 0)), pl.BlockSpec((1, HEAD_DIM), lambda b, pt: (0, 0))],
        out_specs=pl.BlockSpec((1, SAMPLE_ROWS, qw), lambda b, pt: (b, 0, 0)),
    )
    return pl.pallas_call(
        functools.partial(_nsa_sample_kernel, n_pages=n_pages, past=past, t_new=t_new),
        grid_spec=grid_spec,
        out_shape=jax.ShapeDtypeStruct((db, SAMPLE_ROWS, qw), BF16),
        compiler_params=pltpu.CompilerParams(dimension_semantics=("parallel",)),
        name="nsa_sample",
    )(page_table, *([pages] * n_pages), zs, zs, zs, zs, zs, wb, *cmpw, k_gain, q_gain.reshape(1, HEAD_DIM))


C_Q, C_K, C_V, C_TOTAL = 0, MOBA_HEADS * HEAD_DIM, (MOBA_HEADS + MOBA_KV) * HEAD_DIM, (MOBA_HEADS + 2 * MOBA_KV) * HEAD_DIM
MOBA_SUBTILE = 1024


def _masked_softmax2d(s, mask):
    sm = jnp.where(mask, s, NEG)
    m = jnp.max(sm, axis=-1, keepdims=True)
    e = jnp.where(mask, jnp.exp(sm - m), 0.0)
    dsum = jnp.sum(e, axis=-1, keepdims=True)
    return e / jnp.where(dsum > 0, dsum, 1.0)


def _moba_q4(q, qg):
    return jnp.concatenate([_rms(q[:, r * HEAD_DIM:(r + 1) * HEAD_DIM], qg) * HEAD_DIM ** -0.5
                            for r in range(MOBA_REP)], axis=0)


def _moba_select(q4, kmean_pad, cur, n_blocks):
    gs = _dot_nt(q4, kmean_pad.astype(BF16))
    blk = lax.broadcasted_iota(jnp.int32, (1, LANES), 1)
    score = jnp.where((blk < cur) & (blk < n_blocks), gs, NEG)
    sel = _topk_mask(score, min(MOBA_TOPK, n_blocks))
    return jnp.where(blk == cur, 1.0, sel)


def _moba_prep_kernel(k_ref, v_ref, kg_ref, o_ref, km_ref):
    k = k_ref[...]
    kn = jnp.concatenate([_rms(k[:, h * HEAD_DIM:(h + 1) * HEAD_DIM], kg_ref[...]) for h in range(MOBA_KV)], axis=1)
    km_ref[0] = jnp.mean(kn, axis=0, keepdims=True)
    o_ref[...] = jnp.concatenate([kn, v_ref[...]], axis=1).astype(BF16)


def moba_prep_prompt(zc, k_gain, *, rows):
    w = MOBA_KV * HEAD_DIM
    nblk = rows // MOBA_BLOCK
    return pl.pallas_call(
        _moba_prep_kernel,
        grid=(nblk,),
        in_specs=[pl.BlockSpec((MOBA_BLOCK, w), lambda i: (i, C_K // w)),
                  pl.BlockSpec((MOBA_BLOCK, w), lambda i: (i, C_V // w)),
                  pl.BlockSpec((1, HEAD_DIM), lambda i: (0, 0))],
        out_specs=[pl.BlockSpec((MOBA_BLOCK, 2 * w), lambda i: (i, 0)),
                   pl.BlockSpec((1, 1, w), lambda i: (i, 0, 0))],
        out_shape=[jax.ShapeDtypeStruct((rows, 2 * w), BF16), jax.ShapeDtypeStruct((nblk, 1, w), F32)],
        compiler_params=pltpu.CompilerParams(dimension_semantics=("parallel",)),
        name="moba_prep_prompt",
    )(zc, zc, k_gain.reshape(1, HEAD_DIM))


def _moba_attn_kernel(q_ref, km_ref, k_ref, v_ref, oh_ref, qg_ref, o_ref, *, seq):
    i = pl.program_id(2)
    nq = q_ref.shape[0]
    assert nq == MOBA_BLOCK
    nb = seq // MOBA_BLOCK
    rq = MOBA_REP * nq
    q4 = _moba_q4(q_ref[...], qg_ref[...]).astype(BF16)

    km = jnp.concatenate([km_ref[0], jnp.zeros((LANES - nb, HEAD_DIM), F32)], axis=0)
    nbp = -(-nb // SUBLANES) * SUBLANES
    gs = _dot_nt(km.astype(BF16), q4)[:nbp]
    blk = lax.broadcasted_iota(jnp.int32, (nbp, 1), 0)
    score = jnp.where((blk < i) & (blk < nb), gs, NEG)
    rank = jnp.zeros((nbp, rq), F32)
    for k in range(nb):
        sk = score[k:k + 1, :]
        rank = rank + jnp.where((sk > score) | ((sk == score) & (blk > k)), 1.0, 0.0)
    chosen = ((rank < min(MOBA_TOPK, nb)) & (score > 0.5 * NEG)) | (blk == i)
    bias_t = jnp.concatenate([jnp.where(chosen, 0.0, MASK_BIAS), jnp.zeros((LANES - nbp, rq), F32)], axis=0)
    q_aug = jnp.concatenate([q4, bias_t.T.astype(BF16)], axis=1)

    def update(carry, s, rows):
        m, l, acc = carry
        mn = jnp.maximum(m, jnp.max(s, axis=-1, keepdims=True))
        p = jnp.exp(s - mn)
        alpha = jnp.exp(m - mn)
        l = alpha * l + jnp.sum(p, axis=-1, keepdims=True)
        return mn, l, alpha * acc + _dot(p.astype(BF16), v_ref[rows, :])

    own = pl.ds(pl.multiple_of(i * MOBA_BLOCK, MOBA_BLOCK), MOBA_BLOCK)
    kcol = lax.broadcasted_iota(jnp.int32, (1, MOBA_BLOCK), 1)
    rs = min(MOBA_SUBTILE, rq)
    outs = []
    for t in range(rq // rs):
        qa = q_aug[t * rs:(t + 1) * rs]

        def past_step(n, carry, qa=qa):
            rows = pl.ds(pl.multiple_of(n * MOBA_BLOCK, MOBA_BLOCK), MOBA_BLOCK)
            k_aug = jnp.concatenate([k_ref[rows, :], oh_ref[rows, :]], axis=1)
            return update(carry, _dot_nt(qa, k_aug), rows)

        init = (jnp.full((rs, 1), NEG, F32), jnp.zeros((rs, 1), F32), jnp.zeros((rs, HEAD_DIM), F32))
        carry = lax.fori_loop(0, i, past_step, init)
        tq = (t * rs + lax.broadcasted_iota(jnp.int32, (rs, 1), 0)) % nq
        s_own = jnp.where(kcol <= tq, _dot_nt(qa[:, :HEAD_DIM], k_ref[own, :]), NEG)
        _, l, acc = update(carry, s_own, own)
        outs.append(acc / l)
    o = jnp.concatenate(outs, axis=0)
    o_ref[...] = jnp.concatenate([o[r * nq:(r + 1) * nq] for r in range(MOBA_REP)], axis=1).astype(o_ref.dtype)


def moba_attn_prompt(zc, kvp, kmean, q_gain, *, batch, seq):
    nt = seq // MOBA_BLOCK
    gw = MOBA_REP * HEAD_DIM
    km = kmean.reshape(batch, nt, MOBA_KV * HEAD_DIM)
    return pl.pallas_call(
        functools.partial(_moba_attn_kernel, seq=seq),
        grid=(batch, MOBA_KV, nt),
        in_specs=[pl.BlockSpec((MOBA_BLOCK, gw), lambda b, g, i: (b * nt + i, g)),
                  pl.BlockSpec((1, nt, HEAD_DIM), lambda b, g, i: (b, 0, g)),
                  pl.BlockSpec((seq, HEAD_DIM), lambda b, g, i: (b, g)),
                  pl.BlockSpec((seq, HEAD_DIM), lambda b, g, i: (b, MOBA_KV + g)),
                  pl.BlockSpec((seq, LANES), lambda b, g, i: (0, 0)),
                  pl.BlockSpec((1, HEAD_DIM), lambda b, g, i: (0, 0))],
        out_specs=pl.BlockSpec((MOBA_BLOCK, gw), lambda b, g, i: (b * nt + i, g)),
        out_shape=jax.ShapeDtypeStruct((batch * seq, MOBA_HEADS * HEAD_DIM), BF16),
        compiler_params=pltpu.CompilerParams(dimension_semantics=("parallel", "parallel", "arbitrary")),
        name="moba_attn_prompt",
    )(zc, km, kvp, kvp, _block_onehot(seq, MOBA_BLOCK), q_gain.reshape(1, HEAD_DIM))


def _moba_sample_kernel(pt_ref, *refs, n_pages, past, t_new):
    del pt_ref
    pages = refs[:n_pages]
    q_ref, kn_ref, vn_ref, kg_ref, qg_ref, o_ref = refs[n_pages:]
    nq = SAMPLE_ROWS
    d = HEAD_DIM
    rq = MOBA_REP * nq
    nb = -(-(past + t_new) // MOBA_BLOCK)
    n_past_blocks = past // MOBA_BLOCK
    nk = past + LANES
    qpos = past + lax.broadcasted_iota(jnp.int32, (nq, 1), 0)
    qpos4 = jnp.concatenate([qpos] * MOBA_REP, axis=0)
    cur = jnp.right_shift(qpos4, int(math.log2(MOBA_BLOCK)))
    kpos = lax.broadcasted_iota(jnp.int32, (1, nk), 1)
    expand = _block_expand(LANES, nk, MOBA_BLOCK)
    causal = kpos <= qpos4
    zpad = jnp.zeros((LANES - nq, d), F32)
    q = q_ref[0]
    k_new = kn_ref[0]
    v_new = vn_ref[0]
    n_sl = 2 * MOBA_KV
    past_rows = lambda j: jnp.concatenate([pg[0, pl.ds(j, PAGE_SIZE, stride=n_sl), :] for pg in pages], axis=0)
    for g in range(MOBA_KV):
        kn = _rms(past_rows(g), kg_ref[...])
        km = jnp.concatenate([jnp.mean(kn[n * MOBA_BLOCK:(n + 1) * MOBA_BLOCK], axis=0, keepdims=True)
                              for n in range(n_past_blocks)] + [jnp.zeros((LANES - n_past_blocks, d), F32)], axis=0)
        q4 = _moba_q4(q[:, g * MOBA_REP * d:(g + 1) * MOBA_REP * d], qg_ref[...]).astype(BF16)
        sel = _moba_select(q4, km, cur, nb)
        keys = jnp.concatenate([kn, _rms(k_new[:, g * d:(g + 1) * d], kg_ref[...]), zpad], axis=0)
        vals = jnp.concatenate([past_rows(MOBA_KV + g), v_new[:, g * d:(g + 1) * d], zpad], axis=0)
        s = _dot_nt(q4, keys.astype(BF16))
        p = _masked_softmax2d(s, (_dot(sel.astype(BF16), expand) > 0.5) & causal)
        o = _dot(p.astype(BF16), vals.astype(BF16))
        for r in range(MOBA_REP):
            h = g * MOBA_REP + r
            o_ref[0, :, h * d:(h + 1) * d] = o[r * nq:(r + 1) * nq].astype(o_ref.dtype)


def moba_sample(zcs, page_table, cache_kv, k_gain, q_gain, *, t_new):
    db, n_pages = page_table.shape
    past = n_pages * PAGE_SIZE
    assert past % MOBA_BLOCK == 0 and t_new <= MOBA_BLOCK
    n_phys = cache_kv.shape[0]
    w = MOBA_KV * HEAD_DIM
    page_rows = PAGE_SIZE * 2 * MOBA_KV
    pages = cache_kv.reshape(n_phys, page_rows, HEAD_DIM)
    qw = MOBA_HEADS * HEAD_DIM
    zspec = lambda wd, off: pl.BlockSpec((1, SAMPLE_ROWS, wd), lambda b, pt: (b, 0, off // wd))
    page_spec = lambda p: pl.BlockSpec((1, page_rows, HEAD_DIM), lambda b, pt: (pt[b, p], 0, 0))
    grid_spec = pltpu.PrefetchScalarGridSpec(
        num_scalar_prefetch=1,
        grid=(db,),
        in_specs=[page_spec(p) for p in range(n_pages)]
        + [zspec(qw, C_Q), zspec(w, C_K), zspec(w, C_V),
           pl.BlockSpec((1, HEAD_DIM), lambda b, pt: (0, 0)), pl.BlockSpec((1, HEAD_DIM), lambda b, pt: (0, 0))],
        out_specs=pl.BlockSpec((1, SAMPLE_ROWS, qw), lambda b, pt: (b, 0, 0)),
    )
    return pl.pallas_call(
        functools.partial(_moba_sample_kernel, n_pages=n_pages, past=past, t_new=t_new),
        grid_spec=grid_spec,
        out_shape=jax.ShapeDtypeStruct((db, SAMPLE_ROWS, qw), BF16),
        compiler_params=pltpu.CompilerParams(dimension_semantics=("parallel",)),
        name="moba_sample",
    )(page_table, *([pages] * n_pages), zcs, zcs, zcs, k_gain.reshape(1, HEAD_DIM), q_gain.reshape(1, HEAD_DIM))


ROW_TILE = 512
GLA_TILE = 128
NSA_Q_TILE = 128


def _ab_weight_layout(w_in_ab):
    d = w_in_ab.shape[0]
    widths = (GLA_HEADS * GLA_DK, GLA_HEADS * GLA_DK, GLA_HEADS * GLA_DV, GLA_HEADS * GLA_DV, GLA_GATE_RANK,
              NSA_HEADS * HEAD_DIM, NSA_HEADS * 3, 6 * NSA_KV * HEAD_DIM)
    offs = [0]
    for w in widths:
        offs.append(offs[-1] + w)
    gq, gk, gv, gr, ga, nq, ngt, nkv = (w_in_ab[:, offs[i]:offs[i + 1]] for i in range(len(widths)))
    half = NSA_REP * 3
    zeros = lambda n: jnp.zeros((d, n), w_in_ab.dtype)
    misc0 = jnp.concatenate([ga, ngt[:, :half], zeros(LANES - GLA_GATE_RANK - half)], axis=1)
    misc1 = jnp.concatenate([zeros(GLA_GATE_RANK), ngt[:, half:], zeros(LANES - GLA_GATE_RANK - half)], axis=1)
    w = jnp.concatenate([gq, gk, gv, gr, nq, nkv, misc0, misc1, zeros(AB_TOTAL - AB_MISC1 - LANES)], axis=1)
    return w.astype(BF16)


def _pad_sample_rows(z, db, t_new):
    return jnp.pad(z.reshape(db, t_new, z.shape[-1]), ((0, 0), (0, SAMPLE_ROWS - t_new), (0, 0)))


def _conv_ffn(h, l, n_p, batch, seq, t_new, state_ffn_conv, norm_ffn, ffn_w_up, ffn_conv_w, ffn_conv_b, ffn_w_down):
    d_ff = ffn_conv_w.shape[-1]
    db = state_ffn_conv.shape[1]
    act, tails, gate_s = ffn_up_act(h, norm_ffn[l], ffn_w_up[l].astype(BF16), ffn_conv_w[l], ffn_conv_b[l],
                                    state_ffn_conv[l], batch=batch, seq=seq, t_new=t_new, tm=ROW_TILE, tn=512)
    h = matmul_residual(act, ffn_w_down[l].astype(BF16), h, tm=ROW_TILE, tn=min(1024, h.shape[1]), tk=d_ff // 2)
    keep = FFN_CONV - 1
    tps = seq // ROW_TILE
    gate_p = jnp.stack([tails[(b + 1) * tps * SUBLANES - keep:(b + 1) * tps * SUBLANES] for b in range(batch)])
    gate_s = gate_s.reshape(db, t_new, d_ff)[:, t_new - keep:]
    return h, gate_p, gate_s


def kernel(x_prompt, x_sample, page_table, cache_nsa_kv, cache_nsa_win, state_gla, cache_moba_kv, state_ffn_conv, norm_mix, w_in_ab, gla_a_w2, gla_a_b, gla_o_norm, nsa_q_norm, nsa_k_norm, nsa_cmp_w1, nsa_cmp_b1, nsa_cmp_w2, nsa_cmp_b2, nsa_cmp_pe, w_out_ab, w_in_c, moba_q_norm, moba_k_norm, w_out_c, norm_ffn, ffn_w_up, ffn_conv_w, ffn_conv_b, ffn_w_down):
    batch, seq, d_model = x_prompt.shape
    db, t_new, _ = x_sample.shape
    n_p, n_s = batch * seq, db * t_new
    n = n_p + n_s
    assert norm_mix.shape[0] == 2 and w_in_ab.shape[0] == 1 and w_in_c.shape[0] == 1
    assert FFN_CONV - 1 <= t_new <= SAMPLE_ROWS and n % ROW_TILE == 0 and n_p % ROW_TILE == 0
    ffn_args = (state_ffn_conv, norm_ffn, ffn_w_up, ffn_conv_w, ffn_conv_b, ffn_w_down)

    h = jnp.concatenate([x_prompt.reshape(n_p, d_model), x_sample.reshape(n_s, d_model)], axis=0)
    nsa_cache = cache_nsa_kv.reshape(cache_nsa_kv.shape[1:])
    nsa_win = cache_nsa_win.reshape(cache_nsa_win.shape[1:])
    gla_state = state_gla.reshape(state_gla.shape[1:])
    moba_cache = cache_moba_kv.reshape(cache_moba_kv.shape[1:])
    tn_out = min(1024, d_model)

    z = norm_matmul(h, norm_mix[0], _ab_weight_layout(w_in_ab[0]), tm=ROW_TILE, tn=1024)
    zs = _pad_sample_rows(z[n_p:], db, t_new)
    w2p = jnp.pad(gla_a_w2[0], ((0, LANES - GLA_GATE_RANK), (0, 0))).astype(BF16)
    gla_zero = jnp.zeros((batch,) + state_gla.shape[2:], F32)
    zg = z if n % GLA_TILE == 0 else z[:n_p]
    og_p, gla_p = gla_mixer(zg.reshape(-1, GLA_TILE, AB_TOTAL), gla_zero, w2p, gla_a_b[0], gla_o_norm[0],
                            n_seq=batch, tiles_per_seq=seq // GLA_TILE, chunk=GLA_CHUNK, n_valid=GLA_CHUNK, out_dtype=BF16)
    og_s, gla_s = gla_mixer(zs, gla_state, w2p, gla_a_b[0], gla_o_norm[0],
                            n_seq=db, tiles_per_seq=1, chunk=SAMPLE_ROWS, n_valid=t_new, out_dtype=BF16)
    cmpw = nsa_cmp_weights(nsa_cmp_w1[0], nsa_cmp_b1[0], nsa_cmp_w2[0], nsa_cmp_b2[0], nsa_cmp_pe[0], nsa_k_norm[0])
    kc, vc = nsa_cmp_prompt(z, cmpw, batch=batch, seq=seq)
    kvp = nsa_prep_prompt(z, nsa_k_norm[0], rows=n_p, tm=ROW_TILE)
    on_p = nsa_attn_prompt(z, kvp, kc, vc, nsa_q_norm[0], batch=batch, seq=seq, tq=NSA_Q_TILE)
    on_s = nsa_sample(zs, page_table, nsa_cache, nsa_win, cmpw, nsa_k_norm[0], nsa_q_norm[0], t_new=t_new)
    mix_p = jnp.concatenate([og_p.reshape(n_p, -1), on_p], axis=1)
    mix_s = jnp.concatenate([og_s[:, :t_new].reshape(n_s, -1), on_s[:, :t_new].reshape(n_s, -1)], axis=1)
    h = matmul_residual(jnp.concatenate([mix_p, mix_s], axis=0), w_out_ab[0].astype(BF16), h,
                        tm=ROW_TILE, tn=tn_out, tk=mix_p.shape[1])
    h, conv_p0, conv_s0 = _conv_ffn(h, 0, n_p, batch, seq, t_new, *ffn_args)

    kv_w = 4 * NSA_KV * HEAD_DIM
    win_w = 2 * NSA_KV * HEAD_DIM
    win_keep = min(NSA_WINDOW, seq)
    nsa_kv_p = z[:n_p, AB_CMP:AB_CMP + kv_w].reshape(1, batch, seq, 4, NSA_KV, HEAD_DIM)
    nsa_kv_s = z[n_p:, AB_CMP:AB_CMP + kv_w].reshape(1, db, t_new, 4, NSA_KV, HEAD_DIM)
    nsa_win_p = z[:n_p, AB_WIN:AB_WIN + win_w].reshape(batch, seq, win_w)[:, seq - win_keep:]
    nsa_win_p = nsa_win_p.reshape(1, batch, win_keep, 2, NSA_KV, HEAD_DIM)
    win_new = z[n_p:, AB_WIN:AB_WIN + win_w].reshape(db, t_new, 2, NSA_KV, HEAD_DIM)
    nsa_win_s = jnp.concatenate([nsa_win, win_new], axis=1)[None, :, -NSA_WINDOW:]

    zc = norm_matmul(h, norm_mix[1], w_in_c[0].astype(BF16), tm=ROW_TILE, tn=1024)
    zcs = _pad_sample_rows(zc[n_p:], db, t_new)
    kvm, kmean = moba_prep_prompt(zc, moba_k_norm[0], rows=n_p)
    om_p = moba_attn_prompt(zc, kvm, kmean, moba_q_norm[0], batch=batch, seq=seq)
    om_s = moba_sample(zcs, page_table, moba_cache, moba_k_norm[0], moba_q_norm[0], t_new=t_new)
    om = jnp.concatenate([om_p, om_s[:, :t_new].reshape(n_s, -1)], axis=0)
    h = matmul_residual(om, w_out_c[0].astype(BF16), h, tm=ROW_TILE, tn=tn_out, tk=om.shape[1])
    h, conv_p1, conv_s1 = _conv_ffn(h, 1, n_p, batch, seq, t_new, *ffn_args)

    moba_kv_p = zc[:n_p, C_K:].reshape(1, batch, seq, 2, MOBA_KV, HEAD_DIM)
    moba_kv_s = zc[n_p:, C_K:].reshape(1, db, t_new, 2, MOBA_KV, HEAD_DIM)

    return (h[:n_p].reshape(batch, seq, d_model), h[n_p:].reshape(db, t_new, d_model),
            nsa_kv_p, nsa_kv_s, nsa_win_p, nsa_win_s, gla_p[None], gla_s[None], moba_kv_p, moba_kv_s,
            jnp.stack([conv_p0, conv_p1]), jnp.stack([conv_s0, conv_s1]))
```

```python
import functools
import math

import jax
import jax.numpy as jnp
from jax import lax
from jax.experimental import pallas as pl
from jax.experimental.pallas import tpu as pltpu

F32 = jnp.float32
BF16 = jnp.bfloat16

HEAD_DIM = 128
GLA_HEADS = 4
GLA_DK = 128
GLA_DV = 256
GLA_GATE_RANK = 16
GLA_TAU = 16.0
GLA_CHUNK = 16
NSA_HEADS = 8
NSA_KV = 2
NSA_REP = NSA_HEADS // NSA_KV
NSA_CMP_STRIDE = 16
NSA_CMP_LEN = 2 * NSA_CMP_STRIDE
NSA_CMP_HIDDEN = 256
NSA_SEL_BLOCK = 64
NSA_N_SEL = 16
NSA_WINDOW = 512
MOBA_HEADS = 16
MOBA_KV = 4
MOBA_REP = MOBA_HEADS // MOBA_KV
MOBA_BLOCK = 256
MOBA_TOPK = 3
FFN_CONV = 3
NORM_EPS = 1e-6
PAGE_SIZE = 128

LANES = 128
SUBLANES = 8
NEG = -1e30
MASK_BIAS = -(2.0 ** 100)
SAMPLE_ROWS = 8

AB_GQ, AB_GK, AB_GV, AB_GR, AB_NQ = 0, 512, 1024, 2048, 3072
AB_CMP, AB_SEL, AB_WIN, AB_MISC0, AB_MISC1, AB_TOTAL = 4096, 4608, 5120, 5632, 5760, 6144
MISC_GATE_COL = GLA_GATE_RANK


def _rms(x, g):
    return x * lax.rsqrt(jnp.mean(x * x, axis=-1, keepdims=True) + NORM_EPS) * g


def _dot_nt(a, b):
    return lax.dot_general(a, b, (((1,), (1,)), ((), ())), preferred_element_type=F32)


def _dot_tn(a, b):
    return lax.dot_general(a, b, (((0,), (0,)), ((), ())), preferred_element_type=F32)


def _dot(a, b):
    return jnp.dot(a, b, preferred_element_type=F32)


def _silu(x):
    return x * jax.nn.sigmoid(x)


def _norm_matmul_kernel(x_ref, g_ref, w_ref, o_ref, xn_ref):
    @pl.when(pl.program_id(1) == 0)
    def _():
        xn_ref[...] = _rms(x_ref[...], g_ref[...]).astype(BF16)

    o_ref[...] = _dot(xn_ref[...], w_ref[...])


def norm_matmul(x, g, w, *, tm, tn):
    n, k = x.shape
    nout = w.shape[1]
    return pl.pallas_call(
        _norm_matmul_kernel,
        grid=(n // tm, nout // tn),
        in_specs=[pl.BlockSpec((tm, k), lambda i, j: (i, 0)),
                  pl.BlockSpec((1, k), lambda i, j: (0, 0)),
                  pl.BlockSpec((k, tn), lambda i, j: (0, j))],
        out_specs=pl.BlockSpec((tm, tn), lambda i, j: (i, j)),
        out_shape=jax.ShapeDtypeStruct((n, nout), F32),
        scratch_shapes=[pltpu.VMEM((tm, k), BF16)],
        compiler_params=pltpu.CompilerParams(dimension_semantics=("parallel", "arbitrary")),
        name="norm_matmul",
    )(x, g.reshape(1, k), w)


def _matmul_res_kernel(a_ref, w_ref, r_ref, o_ref):
    d = _dot(a_ref[...], w_ref[...])

    @pl.when(pl.program_id(2) == 0)
    def _():
        o_ref[...] = r_ref[...] + d

    @pl.when(pl.program_id(2) > 0)
    def _():
        o_ref[...] += d


def matmul_residual(a, w, res, *, tm, tn, tk, layer=0):
    n, k = a.shape
    nout = w.shape[2]
    return pl.pallas_call(
        _matmul_res_kernel,
        grid=(n // tm, nout // tn, k // tk),
        in_specs=[pl.BlockSpec((tm, tk), lambda i, j, kk: (i, kk)),
                  pl.BlockSpec((None, tk, tn), lambda i, j, kk: (layer, kk, j)),
                  pl.BlockSpec((tm, tn), lambda i, j, kk: (i, j))],
        out_specs=pl.BlockSpec((tm, tn), lambda i, j, kk: (i, j)),
        out_shape=jax.ShapeDtypeStruct((n, nout), F32),
        compiler_params=pltpu.CompilerParams(dimension_semantics=("parallel", "parallel", "arbitrary")),
        name="matmul_residual",
    )(a, w, res)


def _conv_act(g, p1, p2, val, cw_ref, cb_ref):
    c = cb_ref[...] + cw_ref[0:1, :] * p2 + cw_ref[1:2, :] * p1 + cw_ref[2:3, :] * g
    return (_silu(c) * val).astype(BF16)


def _ffn_act_prompt_kernel(g_ref, v_ref, halo_ref, cw_ref, cb_ref, o_ref):
    g = g_ref[...]
    tm = g.shape[0]
    halo = jnp.where(pl.program_id(1) == 0, 0.0, halo_ref[...])
    row = lax.broadcasted_iota(jnp.int32, (tm, 1), 0)
    p1 = jnp.where(row == 0, halo[7:8, :], pltpu.roll(g, 1, 0))
    p2 = jnp.where(row == 0, halo[6:7, :], jnp.where(row == 1, halo[7:8, :], pltpu.roll(g, 2, 0)))
    o_ref[...] = _conv_act(g, p1, p2, v_ref[...], cw_ref, cb_ref)


def ffn_act_prompt(u, conv_w, conv_b, *, batch, seq, tm, tn):
    d_ff = u.shape[1] // 2
    nj = d_ff // tn
    nt = seq // tm
    hb = tm // SUBLANES
    return pl.pallas_call(
        _ffn_act_prompt_kernel,
        grid=(batch, nt, nj),
        in_specs=[pl.BlockSpec((tm, tn), lambda b, i, j: (b * nt + i, j)),
                  pl.BlockSpec((tm, tn), lambda b, i, j: (b * nt + i, nj + j)),
                  pl.BlockSpec((SUBLANES, tn), lambda b, i, j: (jnp.maximum((b * nt + i) * hb - 1, 0), j)),
                  pl.BlockSpec((FFN_CONV, tn), lambda b, i, j: (0, j)),
                  pl.BlockSpec((1, tn), lambda b, i, j: (0, j))],
        out_specs=pl.BlockSpec((tm, tn), lambda b, i, j: (b * nt + i, j)),
        out_shape=jax.ShapeDtypeStruct((batch * seq, d_ff), BF16),
        compiler_params=pltpu.CompilerParams(dimension_semantics=("parallel", "parallel", "parallel")),
        name="ffn_act_prompt",
    )(u, u, u, conv_w, conv_b.reshape(1, d_ff))


def _ffn_act_sample_kernel(g_ref, v_ref, s1_ref, s2_ref, cw_ref, cb_ref, o_ref, *, t_new):
    g = g_ref[...]
    tm = g.shape[0]
    t = lax.broadcasted_iota(jnp.int32, (tm, 1), 0) % t_new
    p1 = jnp.where(t >= 1, pltpu.roll(g, 1, 0), s1_ref[...])
    p2 = jnp.where(t >= 2, pltpu.roll(g, 2, 0), s2_ref[...])
    o_ref[...] = _conv_act(g, p1, p2, v_ref[...], cw_ref, cb_ref)


def ffn_act_sample(u, row0, conv_state, conv_w, conv_b, *, t_new, tn):
    d_ff = u.shape[1] // 2
    nj = d_ff // tn
    db = conv_state.shape[0]
    tm = db * t_new
    rb = row0 // tm
    zero = jnp.zeros((db, t_new - 1, d_ff), F32)
    s1 = jnp.concatenate([conv_state[:, 1:2], zero], axis=1).reshape(tm, d_ff)
    s2 = jnp.concatenate([conv_state, jnp.zeros((db, t_new - 2, d_ff), F32)], axis=1).reshape(tm, d_ff)
    return pl.pallas_call(
        functools.partial(_ffn_act_sample_kernel, t_new=t_new),
        grid=(nj,),
        in_specs=[pl.BlockSpec((tm, tn), lambda j: (rb, j)),
                  pl.BlockSpec((tm, tn), lambda j: (rb, nj + j)),
                  pl.BlockSpec((tm, tn), lambda j: (0, j)),
                  pl.BlockSpec((tm, tn), lambda j: (0, j)),
                  pl.BlockSpec((FFN_CONV, tn), lambda j: (0, j)),
                  pl.BlockSpec((1, tn), lambda j: (0, j))],
        out_specs=pl.BlockSpec((tm, tn), lambda j: (0, j)),
        out_shape=jax.ShapeDtypeStruct((tm, d_ff), BF16),
        compiler_params=pltpu.CompilerParams(dimension_semantics=("parallel",)),
        name="ffn_act_sample",
    )(u, u, s1, s2, conv_w, conv_b.reshape(1, d_ff))


def _ffn_up_kernel(x_ref, g_ref, wg_ref, wv_ref, cw_ref, cb_ref, s1_ref, s2_ref, act_ref, tail_ref, gs_ref,
                   xn_ref, carry_ref, *, tiles_per_seq, n_prompt_tiles, t_new):
    i, j = pl.program_id(0), pl.program_id(1)

    @pl.when(j == 0)
    def _():
        xn_ref[...] = _rms(x_ref[...], g_ref[...]).astype(BF16)

    @pl.when(i == 0)
    def _():
        carry_ref[j] = jnp.zeros(carry_ref.shape[1:], F32)

    xn = xn_ref[...]
    gate = _dot(xn, wg_ref[...])
    val = _dot(xn, wv_ref[...])
    tm = gate.shape[0]
    tail = gate[tm - SUBLANES:, :]
    tail_ref[...] = tail
    gs_ref[...] = gate
    halo = jnp.where(i % tiles_per_seq == 0, 0.0, carry_ref[j])
    carry_ref[j] = tail
    row = lax.broadcasted_iota(jnp.int32, (tm, 1), 0)
    r1 = pltpu.roll(gate, 1, 0)
    r2 = pltpu.roll(gate, 2, 0)
    p1 = jnp.where(row == 0, halo[7:8, :], r1)
    p2 = jnp.where(row == 0, halo[6:7, :], jnp.where(row == 1, halo[7:8, :], r2))
    t = row % t_new
    is_sample = i >= n_prompt_tiles
    p1 = jnp.where(is_sample, jnp.where(t >= 1, r1, s1_ref[...]), p1)
    p2 = jnp.where(is_sample, jnp.where(t >= 2, r2, s2_ref[...]), p2)
    act_ref[...] = _conv_act(gate, p1, p2, val, cw_ref, cb_ref)


def ffn_up_act(h, norm_g, w_up, conv_w, conv_b, conv_state, *, batch, seq, t_new, tm, tn, layer=0):
    n, k = h.shape
    d_ff = w_up.shape[2] // 2
    nj = d_ff // tn
    db = conv_state.shape[0]
    n_s = db * t_new
    assert n_s == tm and seq % tm == 0 and n == batch * seq + n_s
    n_tiles = n // tm
    n_prompt_tiles = n_tiles - 1
    s1 = jnp.concatenate([conv_state[:, 1:2], jnp.zeros((db, t_new - 1, d_ff), F32)], axis=1).reshape(n_s, d_ff)
    s2 = jnp.concatenate([conv_state, jnp.zeros((db, t_new - 2, d_ff), F32)], axis=1).reshape(n_s, d_ff)
    sample_col = lambda i, j: (0, jnp.where(i == n_prompt_tiles, j, 0))
    return pl.pallas_call(
        functools.partial(_ffn_up_kernel, tiles_per_seq=seq // tm, n_prompt_tiles=n_prompt_tiles, t_new=t_new),
        grid=(n_tiles, nj),
        in_specs=[pl.BlockSpec((tm, k), lambda i, j: (i, 0)),
                  pl.BlockSpec((1, k), lambda i, j: (0, 0)),
                  pl.BlockSpec((None, k, tn), lambda i, j: (layer, 0, j)),
                  pl.BlockSpec((None, k, tn), lambda i, j: (layer, 0, nj + j)),
                  pl.BlockSpec((FFN_CONV, tn), lambda i, j: (0, j)),
                  pl.BlockSpec((1, tn), lambda i, j: (0, j)),
                  pl.BlockSpec((tm, tn), sample_col),
                  pl.BlockSpec((tm, tn), sample_col)],
        out_specs=[pl.BlockSpec((tm, tn), lambda i, j: (i, j)),
                   pl.BlockSpec((SUBLANES, tn), lambda i, j: (i, j)),
                   pl.BlockSpec((tm, tn), sample_col)],
        out_shape=[jax.ShapeDtypeStruct((n, d_ff), BF16),
                   jax.ShapeDtypeStruct((n_tiles * SUBLANES, d_ff), F32),
                   jax.ShapeDtypeStruct((tm, d_ff), F32)],
        scratch_shapes=[pltpu.VMEM((tm, k), BF16), pltpu.VMEM((nj, SUBLANES, tn), F32)],
        compiler_params=pltpu.CompilerParams(dimension_semantics=("arbitrary", "arbitrary")),
        name="ffn_up_act",
    )(h, norm_g.reshape(1, k), w_up, w_up, conv_w, conv_b.reshape(1, d_ff), s1, s2)


def _gla_kernel(q_ref, k_ref, v_ref, r_ref, m_ref, w2_ref, b2_ref, on_ref, s0_ref, o_ref, sout_ref,
                st_ref, cum_ref, *, chunk, n_valid, nsub):
    i = pl.program_id(1)
    tt = nsub * chunk

    @pl.when(i == 0)
    def _():
        for h in range(GLA_HEADS):
            st_ref[h] = s0_ref[0, h].T

    a = _dot(m_ref[0].astype(BF16), w2_ref[...]) + b2_ref[...]
    a = (jnp.minimum(a, 0.0) - jnp.log(1.0 + jnp.exp(-jnp.abs(a)))) / GLA_TAU
    pos = lax.broadcasted_iota(jnp.int32, (tt, 1), 0) % chunk
    if n_valid < chunk:
        a = jnp.where(pos < n_valid, a, 0.0)
    cum = a
    sh = 1
    while sh < chunk:
        cum = cum + jnp.where(pos >= sh, pltpu.roll(cum, sh, 0), 0.0)
        sh *= 2
    cum_ref[...] = cum

    ti = lax.broadcasted_iota(jnp.int32, (chunk, 1), 0)

    def step(c, carry):
        rows = pl.ds(pl.multiple_of(c * chunk, chunk), chunk)
        for h in range(GLA_HEADS):
            kcols = slice(h * GLA_DK, (h + 1) * GLA_DK)
            vcols = slice(h * GLA_DV, (h + 1) * GLA_DV)
            qh = q_ref[0, rows, kcols] * GLA_DK ** -0.5
            kh = k_ref[0, rows, kcols]
            vh = v_ref[0, rows, vcols]
            ch = cum_ref[rows, kcols]
            st = st_ref[h]
            o = _dot_nt((qh * jnp.exp(ch)).astype(BF16), st.astype(BF16))
            for s in range(n_valid):
                lo = s // SUBLANES * SUBLANES
                d = jnp.exp(jnp.where(ti[lo:] >= s, ch[lo:] - ch[s:s + 1, :], NEG))
                w = jnp.sum(qh[lo:] * kh[s:s + 1, :] * d, axis=-1, keepdims=True)
                upd = o[lo:] + w * vh[s:s + 1, :]
                o = upd if lo == 0 else jnp.concatenate([o[:lo], upd], axis=0)
            last = ch[chunk - 1:chunk, :]
            kt = kh * jnp.exp(last - ch)
            st_ref[h] = st * jnp.exp(last) + _dot_tn(vh.astype(BF16), kt.astype(BF16))
            rh = r_ref[0, rows, vcols]
            o_ref[0, rows, vcols] = (_rms(o, on_ref[...]) * _silu(rh)).astype(o_ref.dtype)
        return carry

    lax.fori_loop(0, nsub, step, 0)

    @pl.when(i == pl.num_programs(1) - 1)
    def _():
        for h in range(GLA_HEADS):
            sout_ref[0, h] = st_ref[h].T


def gla_mixer(z3, s0, w2p, b2, onorm, *, n_seq, tiles_per_seq, chunk, n_valid, out_dtype):
    tt = z3.shape[1]
    nsub = tt // chunk
    dqk = GLA_HEADS * GLA_DK
    dv = GLA_HEADS * GLA_DV
    tile = lambda b, i: b * tiles_per_seq + i
    return pl.pallas_call(
        functools.partial(_gla_kernel, chunk=chunk, n_valid=n_valid, nsub=nsub),
        grid=(n_seq, tiles_per_seq),
        in_specs=[pl.BlockSpec((1, tt, dqk), lambda b, i: (tile(b, i), 0, AB_GQ // dqk)),
                  pl.BlockSpec((1, tt, dqk), lambda b, i: (tile(b, i), 0, AB_GK // dqk)),
                  pl.BlockSpec((1, tt, dv), lambda b, i: (tile(b, i), 0, AB_GV // dv)),
                  pl.BlockSpec((1, tt, dv), lambda b, i: (tile(b, i), 0, AB_GR // dv)),
                  pl.BlockSpec((1, tt, LANES), lambda b, i: (tile(b, i), 0, AB_MISC0 // LANES)),
                  pl.BlockSpec((LANES, dqk), lambda b, i: (0, 0)),
                  pl.BlockSpec((1, dqk), lambda b, i: (0, 0)),
                  pl.BlockSpec((1, GLA_DV), lambda b, i: (0, 0)),
                  pl.BlockSpec((1, GLA_HEADS, GLA_DK, GLA_DV), lambda b, i: (b, 0, 0, 0))],
        out_specs=[pl.BlockSpec((1, tt, dv), lambda b, i: (tile(b, i), 0, 0)),
                   pl.BlockSpec((1, GLA_HEADS, GLA_DK, GLA_DV), lambda b, i: (b, 0, 0, 0))],
        out_shape=[jax.ShapeDtypeStruct((n_seq * tiles_per_seq, tt, dv), out_dtype),
                   jax.ShapeDtypeStruct((n_seq, GLA_HEADS, GLA_DK, GLA_DV), F32)],
        scratch_shapes=[pltpu.VMEM((GLA_HEADS, GLA_DV, GLA_DK), F32), pltpu.VMEM((tt, dqk), F32)],
        compiler_params=pltpu.CompilerParams(dimension_semantics=("parallel", "arbitrary")),
        name="gla_mixer",
    )(z3, z3, z3, z3, z3, w2p, b2.reshape(1, dqk), onorm.reshape(1, GLA_DV), s0)


def _masked_softmax_rows(s, mask):
    sm = jnp.where(mask[None], s, NEG)
    m = jnp.max(sm, axis=-1, keepdims=True)
    e = jnp.where(mask[None], jnp.exp(sm - m), 0.0)
    d = jnp.sum(e, axis=-1, keepdims=True)
    return e / jnp.where(d > 0, d, 1.0)


def _dot_exact_rhs(x, m):
    hi = x.astype(BF16)
    r1 = x - hi.astype(F32)
    mid = r1.astype(BF16)
    lo = (r1 - mid.astype(F32)).astype(BF16)
    return _dot(hi, m) + _dot(mid, m) + _dot(lo, m)


def _topk_mask(score, n_rounds):
    lane = lax.broadcasted_iota(jnp.int32, score.shape, 1).astype(F32)
    sel = jnp.zeros(score.shape, F32)
    work = score
    for _ in range(n_rounds):
        mx = jnp.max(work, axis=-1, keepdims=True)
        first = jnp.min(jnp.where(work == mx, lane, 1e9), axis=-1, keepdims=True)
        pick = lane == first
        sel = jnp.where(pick & (mx > 0.5 * NEG), 1.0, sel)
        work = jnp.where(pick, 2.0 * NEG, work)
    return sel


def _block_expand(n_blocks_pad, n_keys, block, key0=0):
    bi = lax.broadcasted_iota(jnp.int32, (n_blocks_pad, n_keys), 0)
    ki = lax.broadcasted_iota(jnp.int32, (n_blocks_pad, n_keys), 1) + key0
    return (bi == jnp.right_shift(ki, int(math.log2(block)))).astype(BF16)


def _gelu_tanh(x):
    return x * (0.5 * (1.0 + jnp.tanh(math.sqrt(2.0 / math.pi) * (x + 0.044715 * (x * x * x)))))


def _cmp_mlp(chunk_rows, n_ch, w1_ref, cpe_ref, cb1_ref, w2_ref, cb2_ref, kg_ref):
    hid_w = NSA_CMP_HIDDEN
    out = {}
    for e in range(2):
        accp = _dot(cpe_ref[e], w1_ref[e])
        pe = accp[0:1, :hid_w] + accp[1:2, hid_w:] + cb1_ref[e]
        x = jnp.concatenate([chunk_rows(e * NSA_KV + g) for g in range(NSA_KV)], axis=0).astype(BF16)
        acc = _dot(x, w1_ref[e])
        hid = []
        for g in range(NSA_KV):
            a = acc[g * n_ch:(g + 1) * n_ch]
            hid.append(_gelu_tanh(a[:, :hid_w] + pltpu.roll(a[:, hid_w:], n_ch - 1, 0) + pe))
        ckv = _dot(jnp.concatenate(hid, axis=0).astype(BF16), w2_ref[e]) + cb2_ref[e]
        for g in range(NSA_KV):
            out[e, g] = ckv[g * n_ch:(g + 1) * n_ch]
    kc = [_rms(out[0, g], kg_ref[...]) for g in range(NSA_KV)]
    vc = [out[1, g] for g in range(NSA_KV)]
    return kc, vc


def _nsa_q4(q, qg):
    return jnp.concatenate([_rms(q[:, r * HEAD_DIM:(r + 1) * HEAD_DIM], qg) * HEAD_DIM ** -0.5
                            for r in range(NSA_REP)], axis=0)


def _nsa_cmp_branch(q4, kc, vc, qpos, n_cmp):
    nq = qpos.shape[0]
    n_ch = kc.shape[0]
    sc = _dot_nt(q4, kc.astype(BF16)).reshape(NSA_REP, nq, n_ch)
    cidx = lax.broadcasted_iota(jnp.int32, (1, n_ch), 1)
    cmask = (NSA_CMP_STRIDE * cidx + NSA_CMP_LEN - 1 <= qpos) & (cidx < n_cmp)
    p = _masked_softmax_rows(sc, cmask)
    o = _dot(p.reshape(NSA_REP * nq, n_ch).astype(BF16), vc.astype(BF16))
    return o, jnp.sum(p, axis=0)


def _nsa_select(pcs, qpos0, n_sb):
    n_ch = pcs.shape[1]
    ratio = NSA_SEL_BLOCK // NSA_CMP_STRIDE
    bi = lax.broadcasted_iota(jnp.int32, (LANES, n_ch), 0)
    ci = lax.broadcasted_iota(jnp.int32, (LANES, n_ch), 1)
    mimp = ((ci >= ratio * bi - 1) & (ci <= ratio * bi + ratio - 1)).astype(BF16)
    hi = pcs.astype(BF16)
    r1 = pcs - hi.astype(F32)
    mid = r1.astype(BF16)
    lo = (r1 - mid.astype(F32)).astype(BF16)
    imp = _dot_nt(mimp, hi) + _dot_nt(mimp, mid) + _dot_nt(mimp, lo)
    nbp = -(-n_sb // SUBLANES) * SUBLANES
    blk = lax.broadcasted_iota(jnp.int32, (nbp, 1), 0)
    qpos = qpos0 + lax.broadcasted_iota(jnp.int32, (1, LANES), 1)
    cur = jnp.right_shift(qpos, int(math.log2(NSA_SEL_BLOCK)))
    valid = (blk <= cur) & (blk < n_sb)
    forced = (blk == 0) | (blk == cur) | (blk == cur - 1)
    score = jnp.where(forced, -NEG, jnp.where(valid, imp[:nbp], NEG))
    rank = jnp.zeros((nbp, LANES), F32)
    for k in range(n_sb):
        sk = score[k:k + 1, :]
        rank = rank + jnp.where((sk > score) | ((sk == score) & (blk > k)), 1.0, 0.0)
    sel_t = jnp.where((rank < min(NSA_N_SEL, n_sb)) & (score > 0.5 * NEG), 1.0, 0.0)
    sel_t = jnp.concatenate([sel_t, jnp.zeros((LANES - nbp, LANES), F32)], axis=0)
    return sel_t.T


def _nsa_window_mask(kpos, qpos):
    return (kpos <= qpos) & (qpos - kpos < NSA_WINDOW) & (kpos >= 0)


def _nsa_gate_mix(gsig, o_cmp, o_sel, o_win, nq):
    outs = []
    for r in range(NSA_REP):
        c0 = MISC_GATE_COL + 3 * r
        rows = slice(r * nq, (r + 1) * nq)
        outs.append(gsig[:, c0:c0 + 1] * o_cmp[rows] + gsig[:, c0 + 1:c0 + 2] * o_sel[rows]
                    + gsig[:, c0 + 2:c0 + 3] * o_win[rows])
    return jnp.concatenate(outs, axis=1)


def _nsa_cmp_kernel(*refs, n_ch):
    x_refs = refs[:NSA_CMP_STRIDE]
    w1_ref, cpe_ref, cb1_ref, w2_ref, cb2_ref, kg_ref, kc_ref, vc_ref = refs[NSA_CMP_STRIDE:]

    def chunk_rows(eg):
        return jnp.concatenate([x_refs[s][:, eg * HEAD_DIM:(eg + 1) * HEAD_DIM] for s in range(NSA_CMP_STRIDE)], axis=1)

    kc, vc = _cmp_mlp(chunk_rows, n_ch, w1_ref, cpe_ref, cb1_ref, w2_ref, cb2_ref, kg_ref)
    for g in range(NSA_KV):
        kc_ref[0, g] = kc[g]
        vc_ref[0, g] = vc[g]


def _cmp_weight_specs():
    z = (0,) * 8
    full = lambda shape: pl.BlockSpec(shape, lambda *a: z[:len(shape)])
    taps = NSA_CMP_STRIDE * HEAD_DIM
    return [full((2, taps, 2 * NSA_CMP_HIDDEN)), full((2, SUBLANES, taps)),
            full((2, 1, NSA_CMP_HIDDEN)), full((2, NSA_CMP_HIDDEN, HEAD_DIM)), full((2, 1, HEAD_DIM)),
            full((1, HEAD_DIM))]


def nsa_cmp_prompt(z, cmpw, *, batch, seq):
    n_ch = seq // NSA_CMP_STRIDE
    seg = 4 * HEAD_DIM
    st = NSA_CMP_STRIDE
    zc = z[:batch * seq, AB_CMP:AB_CMP + seg].reshape(batch * n_ch, st * seg)
    shp = jax.ShapeDtypeStruct((batch, NSA_KV, n_ch, HEAD_DIM), F32)
    ospec = pl.BlockSpec((1, NSA_KV, n_ch, HEAD_DIM), lambda b: (b, 0, 0, 0))
    xspec = lambda s: pl.BlockSpec((n_ch, seg), lambda b: (b, s))
    return pl.pallas_call(
        functools.partial(_nsa_cmp_kernel, n_ch=n_ch),
        grid=(batch,),
        in_specs=[xspec(s) for s in range(st)] + _cmp_weight_specs(),
        out_specs=[ospec, ospec],
        out_shape=[shp, shp],
        compiler_params=pltpu.CompilerParams(dimension_semantics=("parallel",)),
        name="nsa_cmp_prompt",
    )(*([zc] * st), *cmpw)


def _nsa_prep_kernel(sel_ref, win_ref, kg_ref, o_ref):
    d = HEAD_DIM
    parts = []
    for ref, row in ((sel_ref, 1), (win_ref, 2)):
        x = ref[...]
        for g in range(NSA_KV):
            parts.append(_rms(x[:, g * d:(g + 1) * d], kg_ref[row:row + 1, :]))
        parts.append(x[:, NSA_KV * d:])
    o_ref[...] = jnp.concatenate(parts, axis=1).astype(BF16)


def nsa_prep_prompt(z, k_gain, *, rows, tm):
    seg = 4 * HEAD_DIM
    return pl.pallas_call(
        _nsa_prep_kernel,
        grid=(rows // tm,),
        in_specs=[pl.BlockSpec((tm, seg), lambda i: (i, AB_SEL // seg)),
                  pl.BlockSpec((tm, seg), lambda i: (i, AB_WIN // seg)),
                  pl.BlockSpec((3, HEAD_DIM), lambda i: (0, 0))],
        out_specs=pl.BlockSpec((tm, 2 * seg), lambda i: (i, 0)),
        out_shape=jax.ShapeDtypeStruct((rows, 2 * seg), BF16),
        compiler_params=pltpu.CompilerParams(dimension_semantics=("parallel",)),
        name="nsa_prep_prompt",
    )(z, z, k_gain)


def _nsa_attn_kernel(q_ref, m_ref, kc_ref, vc_ref, ks_ref, vs_ref, kw_ref, vw_ref, oh_ref, qg_ref, o_ref, *, seq, kt):
    i = pl.program_id(2)
    nq = q_ref.shape[0]
    assert nq == LANES
    q4 = _nsa_q4(q_ref[...], qg_ref[...]).astype(BF16)
    qpos = i * nq + lax.broadcasted_iota(jnp.int32, (nq, 1), 0)
    n_ch = seq // NSA_CMP_STRIDE
    n_sb = -(-seq // NSA_SEL_BLOCK)

    o_cmp, pcs = _nsa_cmp_branch(q4, kc_ref[0, 0], vc_ref[0, 0], qpos, n_ch - 1)
    sel = _nsa_select(pcs, i * nq, n_sb)
    bias = jnp.where(sel > 0.5, 0.0, MASK_BIAS).astype(BF16)
    q_aug = jnp.concatenate([q4, jnp.concatenate([bias] * NSA_REP, axis=0)], axis=1)

    def sel_step(j, carry, causal):
        m, l, acc = carry
        rows = pl.ds(pl.multiple_of(j * kt, kt), kt)
        k_aug = jnp.concatenate([ks_ref[rows, :], oh_ref[rows, :]], axis=1)
        s = _dot_nt(q_aug, k_aug).reshape(NSA_REP, nq, kt)
        if causal:
            kpos = j * kt + lax.broadcasted_iota(jnp.int32, (1, kt), 1)
            s = jnp.where((kpos <= qpos)[None], s, NEG)
        mn = jnp.maximum(m, jnp.max(s, axis=-1, keepdims=True))
        p = jnp.exp(s - mn)
        alpha = jnp.exp(m - mn)
        l = alpha * l + jnp.sum(p, axis=-1, keepdims=True)
        pv = _dot(p.reshape(NSA_REP * nq, kt).astype(BF16), vs_ref[rows, :])
        return mn, l, alpha * acc + pv.reshape(NSA_REP, nq, HEAD_DIM)

    last = (i * nq + nq - 1) // kt
    init = (jnp.full((NSA_REP, nq, 1), NEG, F32), jnp.zeros((NSA_REP, nq, 1), F32),
            jnp.zeros((NSA_REP, nq, HEAD_DIM), F32))
    carry = lax.fori_loop(0, last, functools.partial(sel_step, causal=False), init)
    _, l, acc = sel_step(last, carry, True)
    o_sel = (acc / l).reshape(NSA_REP * nq, HEAD_DIM)

    span = min(NSA_WINDOW + nq, seq)
    start = pl.multiple_of(jnp.clip(i * nq - NSA_WINDOW, 0, seq - span), nq)
    s = _dot_nt(q4, kw_ref[pl.ds(start, span), :]).reshape(NSA_REP, nq, span)
    kpos = start + lax.broadcasted_iota(jnp.int32, (1, span), 1)
    p = _masked_softmax_rows(s, _nsa_window_mask(kpos, qpos))
    o_win = _dot(p.reshape(NSA_REP * nq, span).astype(BF16), vw_ref[pl.ds(start, span), :])

    o_ref[...] = _nsa_gate_mix(jax.nn.sigmoid(m_ref[...]), o_cmp, o_sel, o_win, nq).astype(o_ref.dtype)


def nsa_attn_prompt(z, kvp, kc, vc, q_gain, *, batch, seq, tq):
    nt = seq // tq
    gw = NSA_REP * HEAD_DIM
    kt = min(4 * tq, seq)
    row = lambda b, g, i: b * nt + i
    cspec = pl.BlockSpec((1, 1, seq // NSA_CMP_STRIDE, HEAD_DIM), lambda b, g, i: (b, g, 0, 0))
    kvspec = lambda c: pl.BlockSpec((seq, HEAD_DIM), lambda b, g, i: (b, c + g))
    return pl.pallas_call(
        functools.partial(_nsa_attn_kernel, seq=seq, kt=kt),
        grid=(batch, NSA_KV, nt),
        in_specs=[pl.BlockSpec((tq, gw), lambda b, g, i: (row(b, g, i), AB_NQ // gw + g)),
                  pl.BlockSpec((tq, LANES), lambda b, g, i: (row(b, g, i), AB_MISC0 // LANES + g)),
                  cspec, cspec, kvspec(0), kvspec(2), kvspec(4), kvspec(6),
                  pl.BlockSpec((seq, LANES), lambda b, g, i: (0, 0)),
                  pl.BlockSpec((1, HEAD_DIM), lambda b, g, i: (0, 0))],
        out_specs=pl.BlockSpec((tq, gw), lambda b, g, i: (row(b, g, i), g)),
        out_shape=jax.ShapeDtypeStruct((batch * seq, NSA_KV * gw), BF16),
        compiler_params=pltpu.CompilerParams(dimension_semantics=("parallel", "parallel", "arbitrary")),
        name="nsa_attn_prompt",
    )(z, z, kc, vc, kvp, kvp, kvp, kvp, _block_onehot(seq, NSA_SEL_BLOCK), q_gain.reshape(1, HEAD_DIM))


def _block_onehot(n_keys, block):
    k = lax.broadcasted_iota(jnp.int32, (n_keys, LANES), 0) // block
    return (k == lax.broadcasted_iota(jnp.int32, (n_keys, LANES), 1)).astype(BF16)


def nsa_cmp_weights(cw1, cb1, cw2, cb2, cpe, k_gain):
    st = NSA_CMP_STRIDE
    taps = st * HEAD_DIM
    w1cat = jnp.concatenate([cw1[:, :st], cw1[:, st:]], axis=-1).reshape(2, taps, 2 * NSA_CMP_HIDDEN).astype(BF16)
    pe_rows = jnp.stack([cpe[:, :st].reshape(2, taps), cpe[:, st:].reshape(2, taps)], axis=1)
    pe_rows = jnp.pad(pe_rows, ((0, 0), (0, SUBLANES - 2), (0, 0))).astype(BF16)
    return (w1cat, pe_rows, cb1.reshape(2, 1, NSA_CMP_HIDDEN), cw2.astype(BF16), cb2.reshape(2, 1, HEAD_DIM),
            k_gain[0].reshape(1, HEAD_DIM))


NSA_ROW = 4 * NSA_KV * HEAD_DIM
CHUNKS_PER_PAGE = PAGE_SIZE // NSA_CMP_STRIDE


def _nsa_sample_kernel(pt_ref, *refs, n_pages, past, t_new):
    del pt_ref
    pages = refs[:n_pages]
    (q_ref, sel_ref, win_ref, m0_ref, m1_ref, wb_ref, w1_ref, cpe_ref, cb1_ref, w2_ref, cb2_ref, kg0_ref,
     kg_ref, qg_ref, o_ref) = refs[n_pages:]
    nq = SAMPLE_ROWS
    d = HEAD_DIM
    n_ch = n_pages * CHUNKS_PER_PAGE
    n_cmp = (past + t_new) // NSA_CMP_STRIDE - 1
    n_sb = -(-(past + t_new) // NSA_SEL_BLOCK)
    n_sl = NSA_ROW // d
    n_wsl = 2 * NSA_KV

    def chunk_rows(j):
        return jnp.concatenate(
            [jnp.concatenate([pg[0, pl.ds(s * n_sl + j, CHUNKS_PER_PAGE, stride=NSA_CMP_STRIDE * n_sl), :]
                              for s in range(NSA_CMP_STRIDE)], axis=1) for pg in pages], axis=0)

    def past_rows(j):
        return jnp.concatenate([pg[0, pl.ds(j, PAGE_SIZE, stride=n_sl), :] for pg in pages], axis=0)

    kc, vc = _cmp_mlp(chunk_rows, n_ch, w1_ref, cpe_ref, cb1_ref, w2_ref, cb2_ref, kg0_ref)

    qpos = past + lax.broadcasted_iota(jnp.int32, (nq, 1), 0)
    nk = past + LANES
    kpos = lax.broadcasted_iota(jnp.int32, (1, nk), 1)
    esel = _block_expand(LANES, nk, NSA_SEL_BLOCK)
    causal = kpos <= qpos
    nw = NSA_WINDOW + LANES
    wmask = _nsa_window_mask(past - NSA_WINDOW + lax.broadcasted_iota(jnp.int32, (1, nw), 1), qpos)
    zpad = jnp.zeros((LANES - nq, d), F32)
    q = q_ref[0]
    sel_new = sel_ref[0]
    win_new = win_ref[0]
    kg_sel = kg_ref[1:2, :]
    kg_win = kg_ref[2:3, :]
    col = lambda x, c: x[:, c * d:(c + 1) * d]
    wb_rows = lambda j: wb_ref[0, pl.ds(j, NSA_WINDOW, stride=n_wsl), :]

    for g in range(NSA_KV):
        q4 = _nsa_q4(q[:, g * NSA_REP * d:(g + 1) * NSA_REP * d], qg_ref[...]).astype(BF16)
        o_cmp, pcs = _nsa_cmp_branch(q4, kc[g], vc[g], qpos, n_cmp)
        pcs = jnp.concatenate([pcs, jnp.zeros((LANES - nq, n_ch), F32)], axis=0)
        selb = _nsa_select(pcs, past, n_sb)[:nq].astype(BF16)

        ks = jnp.concatenate([_rms(past_rows(2 * NSA_KV + g), kg_sel), _rms(col(sel_new, g), kg_sel), zpad], axis=0)
        vs = jnp.concatenate([past_rows(3 * NSA_KV + g), col(sel_new, NSA_KV + g), zpad], axis=0)
        s = _dot_nt(q4, ks.astype(BF16)).reshape(NSA_REP, nq, nk)
        p = _masked_softmax_rows(s, (_dot(selb, esel) > 0.5) & causal)
        o_sel = _dot(p.reshape(NSA_REP * nq, nk).astype(BF16), vs.astype(BF16))

        kw = jnp.concatenate([_rms(wb_rows(g), kg_win), _rms(col(win_new, g), kg_win), zpad], axis=0)
        vw = jnp.concatenate([wb_rows(NSA_KV + g), col(win_new, NSA_KV + g), zpad], axis=0)
        s = _dot_nt(q4, kw.astype(BF16)).reshape(NSA_REP, nq, nw)
        p = _masked_softmax_rows(s, wmask)
        o_win = _dot(p.reshape(NSA_REP * nq, nw).astype(BF16), vw.astype(BF16))

        gsig = jax.nn.sigmoid((m0_ref, m1_ref)[g][0])
        o_ref[0, :, g * NSA_REP * d:(g + 1) * NSA_REP * d] = _nsa_gate_mix(gsig, o_cmp, o_sel, o_win, nq).astype(o_ref.dtype)


def nsa_sample(zs, page_table, cache_kv, cache_win, cmpw, k_gain, q_gain, *, t_new):
    db, n_pages = page_table.shape
    past = n_pages * PAGE_SIZE
    assert cache_win.shape[1] == NSA_WINDOW and (past + t_new) // NSA_CMP_STRIDE == n_pages * CHUNKS_PER_PAGE
    n_phys = cache_kv.shape[0]
    page_rows = PAGE_SIZE * NSA_ROW // HEAD_DIM
    win_rows = NSA_WINDOW * 2 * NSA_KV
    pages = cache_kv.reshape(n_phys, page_rows, HEAD_DIM)
    wb = cache_win.reshape(db, win_rows, HEAD_DIM)
    seg = 4 * HEAD_DIM
    qw = NSA_HEADS * HEAD_DIM
    zspec = lambda w, off: pl.BlockSpec((1, SAMPLE_ROWS, w), lambda b, pt: (b, 0, off // w))
    page_spec = lambda p: pl.BlockSpec((1, page_rows, HEAD_DIM), lambda b, pt: (pt[b, p], 0, 0))
    grid_spec = pltpu.PrefetchScalarGridSpec(
        num_scalar_prefetch=1,
        grid=(db,),
        in_specs=[page_spec(p) for p in range(n_pages)]
        + [zspec(qw, AB_NQ), zspec(seg, AB_SEL), zspec(seg, AB_WIN), zspec(LANES, AB_MISC0), zspec(LANES, AB_MISC1),
           pl.BlockSpec((1, win_rows, HEAD_DIM), lambda b, pt: (b, 0, 0))]
        + _cmp_weight_specs()
        + [pl.BlockSpec((3, HEAD_DIM), lambda b, pt: (0, 0)), pl.BlockSpec((1, HEAD_DIM), lambda b, pt: (0, 0))],
        out_specs=pl.BlockSpec((1, SAMPLE_ROWS, qw), lambda b, pt: (b, 0, 0)),
    )
    return pl.pallas_call(
        functools.partial(_nsa_sample_kernel, n_pages=n_pages, past=past, t_new=t_new),
        grid_spec=grid_spec,
        out_shape=jax.ShapeDtypeStruct((db, SAMPLE_ROWS, qw), BF16),
        compiler_params=pltpu.CompilerParams(dimension_semantics=("parallel",)),
        name="nsa_sample",
    )(page_table, *([pages] * n_pages), zs, zs, zs, zs, zs, wb, *cmpw, k_gain, q_gain.reshape(1, HEAD_DIM))


C_Q, C_K, C_V, C_TOTAL = 0, MOBA_HEADS * HEAD_DIM, (MOBA_HEADS + MOBA_KV) * HEAD_DIM, (MOBA_HEADS + 2 * MOBA_KV) * HEAD_DIM
MOBA_SUBTILE = 1024


def _masked_softmax2d(s, mask):
    sm = jnp.where(mask, s, NEG)
    m = jnp.max(sm, axis=-1, keepdims=True)
    e = jnp.where(mask, jnp.exp(sm - m), 0.0)
    dsum = jnp.sum(e, axis=-1, keepdims=True)
    return e / jnp.where(dsum > 0, dsum, 1.0)


def _moba_q4(q, qg):
    return jnp.concatenate([_rms(q[:, r * HEAD_DIM:(r + 1) * HEAD_DIM], qg) * HEAD_DIM ** -0.5
                            for r in range(MOBA_REP)], axis=0)


def _moba_select(q4, kmean_pad, cur, n_blocks):
    gs = _dot_nt(q4, kmean_pad.astype(BF16))
    blk = lax.broadcasted_iota(jnp.int32, (1, LANES), 1)
    score = jnp.where((blk < cur) & (blk < n_blocks), gs, NEG)
    sel = _topk_mask(score, min(MOBA_TOPK, n_blocks))
    return jnp.where(blk == cur, 1.0, sel)


def _moba_prep_kernel(k_ref, v_ref, kg_ref, o_ref, km_ref):
    k = k_ref[...]
    kn = jnp.concatenate([_rms(k[:, h * HEAD_DIM:(h + 1) * HEAD_DIM], kg_ref[...]) for h in range(MOBA_KV)], axis=1)
    km_ref[0] = jnp.mean(kn, axis=0, keepdims=True)
    o_ref[...] = jnp.concatenate([kn, v_ref[...]], axis=1).astype(BF16)


def moba_prep_prompt(zc, k_gain, *, rows):
    w = MOBA_KV * HEAD_DIM
    nblk = rows // MOBA_BLOCK
    return pl.pallas_call(
        _moba_prep_kernel,
        grid=(nblk,),
        in_specs=[pl.BlockSpec((MOBA_BLOCK, w), lambda i: (i, C_K // w)),
                  pl.BlockSpec((MOBA_BLOCK, w), lambda i: (i, C_V // w)),
                  pl.BlockSpec((1, HEAD_DIM), lambda i: (0, 0))],
        out_specs=[pl.BlockSpec((MOBA_BLOCK, 2 * w), lambda i: (i, 0)),
                   pl.BlockSpec((1, 1, w), lambda i: (i, 0, 0))],
        out_shape=[jax.ShapeDtypeStruct((rows, 2 * w), BF16), jax.ShapeDtypeStruct((nblk, 1, w), F32)],
        compiler_params=pltpu.CompilerParams(dimension_semantics=("parallel",)),
        name="moba_prep_prompt",
    )(zc, zc, k_gain.reshape(1, HEAD_DIM))


def _moba_attn_kernel(q_ref, km_ref, k_ref, v_ref, oh_ref, qg_ref, o_ref, *, seq):
    i = pl.program_id(2)
    nq = q_ref.shape[0]
    assert nq == MOBA_BLOCK
    nb = seq // MOBA_BLOCK
    rq = MOBA_REP * nq
    q4 = _moba_q4(q_ref[...], qg_ref[...]).astype(BF16)

    km = jnp.concatenate([km_ref[0], jnp.zeros((LANES - nb, HEAD_DIM), F32)], axis=0)
    nbp = -(-nb // SUBLANES) * SUBLANES
    gs = _dot_nt(km.astype(BF16), q4)[:nbp]
    blk = lax.broadcasted_iota(jnp.int32, (nbp, 1), 0)
    score = jnp.where((blk < i) & (blk < nb), gs, NEG)
    rank = jnp.zeros((nbp, rq), F32)
    for k in range(nb):
        sk = score[k:k + 1, :]
        rank = rank + jnp.where((sk > score) | ((sk == score) & (blk > k)), 1.0, 0.0)
    chosen = ((rank < min(MOBA_TOPK, nb)) & (score > 0.5 * NEG)) | (blk == i)
    bias_t = jnp.concatenate([jnp.where(chosen, 0.0, MASK_BIAS), jnp.zeros((LANES - nbp, rq), F32)], axis=0)
    q_aug = jnp.concatenate([q4, bias_t.T.astype(BF16)], axis=1)

    def update(carry, s, rows):
        m, l, acc = carry
        mn = jnp.maximum(m, jnp.max(s, axis=-1, keepdims=True))
        p = jnp.exp(s - mn)
        alpha = jnp.exp(m - mn)
        l = alpha * l + jnp.sum(p, axis=-1, keepdims=True)
        return mn, l, alpha * acc + _dot(p.astype(BF16), v_ref[rows, :])

    own = pl.ds(pl.multiple_of(i * MOBA_BLOCK, MOBA_BLOCK), MOBA_BLOCK)
    kcol = lax.broadcasted_iota(jnp.int32, (1, MOBA_BLOCK), 1)
    rs = min(MOBA_SUBTILE, rq)
    outs = []
    for t in range(rq // rs):
        qa = q_aug[t * rs:(t + 1) * rs]

        def past_step(n, carry, qa=qa):
            rows = pl.ds(pl.multiple_of(n * MOBA_BLOCK, MOBA_BLOCK), MOBA_BLOCK)
            k_aug = jnp.concatenate([k_ref[rows, :], oh_ref[rows, :]], axis=1)
            return update(carry, _dot_nt(qa, k_aug), rows)

        init = (jnp.full((rs, 1), NEG, F32), jnp.zeros((rs, 1), F32), jnp.zeros((rs, HEAD_DIM), F32))
        carry = lax.fori_loop(0, i, past_step, init)
        tq = (t * rs + lax.broadcasted_iota(jnp.int32, (rs, 1), 0)) % nq
        s_own = jnp.where(kcol <= tq, _dot_nt(qa[:, :HEAD_DIM], k_ref[own, :]), NEG)
        _, l, acc = update(carry, s_own, own)
        outs.append(acc / l)
    o = jnp.concatenate(outs, axis=0)
    o_ref[...] = jnp.concatenate([o[r * nq:(r + 1) * nq] for r in range(MOBA_REP)], axis=1).astype(o_ref.dtype)


def moba_attn_prompt(zc, kvp, kmean, q_gain, *, batch, seq):
    nt = seq // MOBA_BLOCK
    gw = MOBA_REP * HEAD_DIM
    km = kmean.reshape(batch, nt, MOBA_KV * HEAD_DIM)
    return pl.pallas_call(
        functools.partial(_moba_attn_kernel, seq=seq),
        grid=(batch, MOBA_KV, nt),
        in_specs=[pl.BlockSpec((MOBA_BLOCK, gw), lambda b, g, i: (b * nt + i, g)),
                  pl.BlockSpec((1, nt, HEAD_DIM), lambda b, g, i: (b, 0, g)),
                  pl.BlockSpec((seq, HEAD_DIM), lambda b, g, i: (b, g)),
                  pl.BlockSpec((seq, HEAD_DIM), lambda b, g, i: (b, MOBA_KV + g)),
                  pl.BlockSpec((seq, LANES), lambda b, g, i: (0, 0)),
                  pl.BlockSpec((1, HEAD_DIM), lambda b, g, i: (0, 0))],
        out_specs=pl.BlockSpec((MOBA_BLOCK, gw), lambda b, g, i: (b * nt + i, g)),
        out_shape=jax.ShapeDtypeStruct((batch * seq, MOBA_HEADS * HEAD_DIM), BF16),
        compiler_params=pltpu.CompilerParams(dimension_semantics=("parallel", "parallel", "arbitrary")),
        name="moba_attn_prompt",
    )(zc, km, kvp, kvp, _block_onehot(seq, MOBA_BLOCK), q_gain.reshape(1, HEAD_DIM))


def _moba_sample_kernel(pt_ref, *refs, n_pages, past, t_new):
    del pt_ref
    pages = refs[:n_pages]
    q_ref, kn_ref, vn_ref, kg_ref, qg_ref, o_ref = refs[n_pages:]
    nq = SAMPLE_ROWS
    d = HEAD_DIM
    rq = MOBA_REP * nq
    nb = -(-(past + t_new) // MOBA_BLOCK)
    n_past_blocks = past // MOBA_BLOCK
    nk = past + LANES
    qpos = past + lax.broadcasted_iota(jnp.int32, (nq, 1), 0)
    qpos4 = jnp.concatenate([qpos] * MOBA_REP, axis=0)
    cur = jnp.right_shift(qpos4, int(math.log2(MOBA_BLOCK)))
    kpos = lax.broadcasted_iota(jnp.int32, (1, nk), 1)
    expand = _block_expand(LANES, nk, MOBA_BLOCK)
    causal = kpos <= qpos4
    zpad = jnp.zeros((LANES - nq, d), F32)
    q = q_ref[0]
    k_new = kn_ref[0]
    v_new = vn_ref[0]
    n_sl = 2 * MOBA_KV
    past_rows = lambda j: jnp.concatenate([pg[0, pl.ds(j, PAGE_SIZE, stride=n_sl), :] for pg in pages], axis=0)
    for g in range(MOBA_KV):
        kn = _rms(past_rows(g), kg_ref[...])
        km = jnp.concatenate([jnp.mean(kn[n * MOBA_BLOCK:(n + 1) * MOBA_BLOCK], axis=0, keepdims=True)
                              for n in range(n_past_blocks)] + [jnp.zeros((LANES - n_past_blocks, d), F32)], axis=0)
        q4 = _moba_q4(q[:, g * MOBA_REP * d:(g + 1) * MOBA_REP * d], qg_ref[...]).astype(BF16)
        sel = _moba_select(q4, km, cur, nb)
        keys = jnp.concatenate([kn, _rms(k_new[:, g * d:(g + 1) * d], kg_ref[...]), zpad], axis=0)
        vals = jnp.concatenate([past_rows(MOBA_KV + g), v_new[:, g * d:(g + 1) * d], zpad], axis=0)
        s = _dot_nt(q4, keys.astype(BF16))
        p = _masked_softmax2d(s, (_dot(sel.astype(BF16), expand) > 0.5) & causal)
        o = _dot(p.astype(BF16), vals.astype(BF16))
        for r in range(MOBA_REP):
            h = g * MOBA_REP + r
            o_ref[0, :, h * d:(h + 1) * d] = o[r * nq:(r + 1) * nq].astype(o_ref.dtype)


def moba_sample(zcs, page_table, cache_kv, k_gain, q_gain, *, t_new):
    db, n_pages = page_table.shape
    past = n_pages * PAGE_SIZE
    assert past % MOBA_BLOCK == 0 and t_new <= MOBA_BLOCK
    n_phys = cache_kv.shape[0]
    w = MOBA_KV * HEAD_DIM
    page_rows = PAGE_SIZE * 2 * MOBA_KV
    pages = cache_kv.reshape(n_phys, page_rows, HEAD_DIM)
    qw = MOBA_HEADS * HEAD_DIM
    zspec = lambda wd, off: pl.BlockSpec((1, SAMPLE_ROWS, wd), lambda b, pt: (b, 0, off // wd))
    page_spec = lambda p: pl.BlockSpec((1, page_rows, HEAD_DIM), lambda b, pt: (pt[b, p], 0, 0))
    grid_spec = pltpu.PrefetchScalarGridSpec(
        num_scalar_prefetch=1,
        grid=(db,),
        in_specs=[page_spec(p) for p in range(n_pages)]
        + [zspec(qw, C_Q), zspec(w, C_K), zspec(w, C_V),
           pl.BlockSpec((1, HEAD_DIM), lambda b, pt: (0, 0)), pl.BlockSpec((1, HEAD_DIM), lambda b, pt: (0, 0))],
        out_specs=pl.BlockSpec((1, SAMPLE_ROWS, qw), lambda b, pt: (b, 0, 0)),
    )
    return pl.pallas_call(
        functools.partial(_moba_sample_kernel, n_pages=n_pages, past=past, t_new=t_new),
        grid_spec=grid_spec,
        out_shape=jax.ShapeDtypeStruct((db, SAMPLE_ROWS, qw), BF16),
        compiler_params=pltpu.CompilerParams(dimension_semantics=("parallel",)),
        name="moba_sample",
    )(page_table, *([pages] * n_pages), zcs, zcs, zcs, k_gain.reshape(1, HEAD_DIM), q_gain.reshape(1, HEAD_DIM))


ROW_TILE = 512
GLA_TILE = 128
NSA_Q_TILE = 128


def _ab_weight_layout(w_in_ab):
    d = w_in_ab.shape[0]
    widths = (GLA_HEADS * GLA_DK, GLA_HEADS * GLA_DK, GLA_HEADS * GLA_DV, GLA_HEADS * GLA_DV, GLA_GATE_RANK,
              NSA_HEADS * HEAD_DIM, NSA_HEADS * 3, 6 * NSA_KV * HEAD_DIM)
    offs = [0]
    for w in widths:
        offs.append(offs[-1] + w)
    gq, gk, gv, gr, ga, nq, ngt, nkv = (w_in_ab[:, offs[i]:offs[i + 1]] for i in range(len(widths)))
    half = NSA_REP * 3
    zeros = lambda n: jnp.zeros((d, n), w_in_ab.dtype)
    misc0 = jnp.concatenate([ga, ngt[:, :half], zeros(LANES - GLA_GATE_RANK - half)], axis=1)
    misc1 = jnp.concatenate([zeros(GLA_GATE_RANK), ngt[:, half:], zeros(LANES - GLA_GATE_RANK - half)], axis=1)
    w = jnp.concatenate([gq, gk, gv, gr, nq, nkv, misc0, misc1, zeros(AB_TOTAL - AB_MISC1 - LANES)], axis=1)
    return w.astype(BF16)


def _pad_sample_rows(z, db, t_new):
    return jnp.pad(z.reshape(db, t_new, z.shape[-1]), ((0, 0), (0, SAMPLE_ROWS - t_new), (0, 0)))


def _matmul_rows(n):
    return n // 8 if n % (8 * 2 * SUBLANES) == 0 else ROW_TILE


def _conv_ffn(h, l, n_p, batch, seq, t_new, state_ffn_conv, norm_ffn, w_up_bf16, ffn_conv_w, ffn_conv_b, w_down_bf16):
    d_ff = ffn_conv_w.shape[-1]
    db = state_ffn_conv.shape[1]
    act, tails, gate_s = ffn_up_act(h, norm_ffn[l], w_up_bf16, ffn_conv_w[l], ffn_conv_b[l], state_ffn_conv[l],
                                    batch=batch, seq=seq, t_new=t_new, tm=ROW_TILE, tn=512, layer=l)
    h = matmul_residual(act, w_down_bf16, h, tm=_matmul_rows(h.shape[0]), tn=min(1024, h.shape[1]), tk=d_ff // 2, layer=l)
    keep = FFN_CONV - 1
    tps = seq // ROW_TILE
    gate_p = jnp.stack([tails[(b + 1) * tps * SUBLANES - keep:(b + 1) * tps * SUBLANES] for b in range(batch)])
    gate_s = gate_s.reshape(db, t_new, d_ff)[:, t_new - keep:]
    return h, gate_p, gate_s


def kernel(x_prompt, x_sample, page_table, cache_nsa_kv, cache_nsa_win, state_gla, cache_moba_kv, state_ffn_conv, norm_mix, w_in_ab, gla_a_w2, gla_a_b, gla_o_norm, nsa_q_norm, nsa_k_norm, nsa_cmp_w1, nsa_cmp_b1, nsa_cmp_w2, nsa_cmp_b2, nsa_cmp_pe, w_out_ab, w_in_c, moba_q_norm, moba_k_norm, w_out_c, norm_ffn, ffn_w_up, ffn_conv_w, ffn_conv_b, ffn_w_down):
    batch, seq, d_model = x_prompt.shape
    db, t_new, _ = x_sample.shape
    n_p, n_s = batch * seq, db * t_new
    n = n_p + n_s
    assert norm_mix.shape[0] == 2 and w_in_ab.shape[0] == 1 and w_in_c.shape[0] == 1
    assert FFN_CONV - 1 <= t_new <= SAMPLE_ROWS and n % ROW_TILE == 0 and n_p % ROW_TILE == 0
    ffn_args = (state_ffn_conv, norm_ffn, ffn_w_up.astype(BF16), ffn_conv_w, ffn_conv_b, ffn_w_down.astype(BF16))
    tm_mm = _matmul_rows(n)

    h = jnp.concatenate([x_prompt.reshape(n_p, d_model), x_sample.reshape(n_s, d_model)], axis=0)
    nsa_cache = cache_nsa_kv.reshape(cache_nsa_kv.shape[1:])
    nsa_win = cache_nsa_win.reshape(cache_nsa_win.shape[1:])
    gla_state = state_gla.reshape(state_gla.shape[1:])
    moba_cache = cache_moba_kv.reshape(cache_moba_kv.shape[1:])
    tn_out = min(1024, d_model)

    z = norm_matmul(h, norm_mix[0], _ab_weight_layout(w_in_ab[0]), tm=tm_mm, tn=1024)
    zs = _pad_sample_rows(z[n_p:], db, t_new)
    w2p = jnp.pad(gla_a_w2[0], ((0, LANES - GLA_GATE_RANK), (0, 0))).astype(BF16)
    gla_zero = jnp.zeros((batch,) + state_gla.shape[2:], F32)
    zg = z if n % GLA_TILE == 0 else z[:n_p]
    og_p, gla_p = gla_mixer(zg.reshape(-1, GLA_TILE, AB_TOTAL), gla_zero, w2p, gla_a_b[0], gla_o_norm[0],
                            n_seq=batch, tiles_per_seq=seq // GLA_TILE, chunk=GLA_CHUNK, n_valid=GLA_CHUNK, out_dtype=BF16)
    og_s, gla_s = gla_mixer(zs, gla_state, w2p, gla_a_b[0], gla_o_norm[0],
                            n_seq=db, tiles_per_seq=1, chunk=SAMPLE_ROWS, n_valid=t_new, out_dtype=BF16)
    cmpw = nsa_cmp_weights(nsa_cmp_w1[0], nsa_cmp_b1[0], nsa_cmp_w2[0], nsa_cmp_b2[0], nsa_cmp_pe[0], nsa_k_norm[0])
    kc, vc = nsa_cmp_prompt(z, cmpw, batch=batch, seq=seq)
    kvp = nsa_prep_prompt(z, nsa_k_norm[0], rows=n_p, tm=ROW_TILE)
    on_p = nsa_attn_prompt(z, kvp, kc, vc, nsa_q_norm[0], batch=batch, seq=seq, tq=NSA_Q_TILE)
    on_s = nsa_sample(zs, page_table, nsa_cache, nsa_win, cmpw, nsa_k_norm[0], nsa_q_norm[0], t_new=t_new)
    mix_p = jnp.concatenate([og_p.reshape(n_p, -1), on_p], axis=1)
    mix_s = jnp.concatenate([og_s[:, :t_new].reshape(n_s, -1), on_s[:, :t_new].reshape(n_s, -1)], axis=1)
    h = matmul_residual(jnp.concatenate([mix_p, mix_s], axis=0), w_out_ab.astype(BF16), h,
                        tm=tm_mm, tn=tn_out, tk=mix_p.shape[1])
    h, conv_p0, conv_s0 = _conv_ffn(h, 0, n_p, batch, seq, t_new, *ffn_args)

    kv_w = 4 * NSA_KV * HEAD_DIM
    win_w = 2 * NSA_KV * HEAD_DIM
    win_keep = min(NSA_WINDOW, seq)
    nsa_kv_p = z[:n_p, AB_CMP:AB_CMP + kv_w].reshape(1, batch, seq, 4, NSA_KV, HEAD_DIM)
    nsa_kv_s = z[n_p:, AB_CMP:AB_CMP + kv_w].reshape(1, db, t_new, 4, NSA_KV, HEAD_DIM)
    nsa_win_p = z[:n_p, AB_WIN:AB_WIN + win_w].reshape(batch, seq, win_w)[:, seq - win_keep:]
    nsa_win_p = nsa_win_p.reshape(1, batch, win_keep, 2, NSA_KV, HEAD_DIM)
    win_new = z[n_p:, AB_WIN:AB_WIN + win_w].reshape(db, t_new, 2, NSA_KV, HEAD_DIM)
    nsa_win_s = jnp.concatenate([nsa_win, win_new], axis=1)[None, :, -NSA_WINDOW:]

    zc = norm_matmul(h, norm_mix[1], w_in_c[0].astype(BF16), tm=tm_mm, tn=1024)
    zcs = _pad_sample_rows(zc[n_p:], db, t_new)
    kvm, kmean = moba_prep_prompt(zc, moba_k_norm[0], rows=n_p)
    om_p = moba_attn_prompt(zc, kvm, kmean, moba_q_norm[0], batch=batch, seq=seq)
    om_s = moba_sample(zcs, page_table, moba_cache, moba_k_norm[0], moba_q_norm[0], t_new=t_new)
    om = jnp.concatenate([om_p, om_s[:, :t_new].reshape(n_s, -1)], axis=0)
    h = matmul_residual(om, w_out_c.astype(BF16), h, tm=tm_mm, tn=tn_out, tk=om.shape[1])
    h, conv_p1, conv_s1 = _conv_ffn(h, 1, n_p, batch, seq, t_new, *ffn_args)

    moba_kv_p = zc[:n_p, C_K:].reshape(1, batch, seq, 2, MOBA_KV, HEAD_DIM)
    moba_kv_s = zc[n_p:, C_K:].reshape(1, db, t_new, 2, MOBA_KV, HEAD_DIM)

    return (h[:n_p].reshape(batch, seq, d_model), h[n_p:].reshape(db, t_new, d_model),
            nsa_kv_p, nsa_kv_s, nsa_win_p, nsa_win_s, gla_p[None], gla_s[None], moba_kv_p, moba_kv_s,
            jnp.stack([conv_p0, conv_p1]), jnp.stack([conv_s0, conv_s1]))
```

```python
import functools
import math

import jax
import jax.numpy as jnp
from jax import lax
from jax.experimental import pallas as pl
from jax.experimental.pallas import tpu as pltpu

F32 = jnp.float32
BF16 = jnp.bfloat16

HEAD_DIM = 128
GLA_HEADS = 4
GLA_DK = 128
GLA_DV = 256
GLA_GATE_RANK = 16
GLA_TAU = 16.0
GLA_CHUNK = 16
NSA_HEADS = 8
NSA_KV = 2
NSA_REP = NSA_HEADS // NSA_KV
NSA_CMP_STRIDE = 16
NSA_CMP_LEN = 2 * NSA_CMP_STRIDE
NSA_CMP_HIDDEN = 256
NSA_SEL_BLOCK = 64
NSA_N_SEL = 16
NSA_WINDOW = 512
MOBA_HEADS = 16
MOBA_KV = 4
MOBA_REP = MOBA_HEADS // MOBA_KV
MOBA_BLOCK = 256
MOBA_TOPK = 3
FFN_CONV = 3
NORM_EPS = 1e-6
PAGE_SIZE = 128

LANES = 128
SUBLANES = 8
NEG = -1e30
MASK_BIAS = -(2.0 ** 100)
SAMPLE_ROWS = 8

AB_GQ, AB_GK, AB_GV, AB_GR, AB_NQ = 0, 512, 1024, 2048, 3072
AB_CMP, AB_SEL, AB_WIN, AB_MISC0, AB_MISC1, AB_TOTAL = 4096, 4608, 5120, 5632, 5760, 6144
MISC_GATE_COL = GLA_GATE_RANK


def _rms(x, g):
    return x * lax.rsqrt(jnp.mean(x * x, axis=-1, keepdims=True) + NORM_EPS) * g


def _dot_nt(a, b):
    return lax.dot_general(a, b, (((1,), (1,)), ((), ())), preferred_element_type=F32)


def _dot_tn(a, b):
    return lax.dot_general(a, b, (((0,), (0,)), ((), ())), preferred_element_type=F32)


def _dot(a, b):
    return jnp.dot(a, b, preferred_element_type=F32)


def _silu(x):
    return x * jax.nn.sigmoid(x)


def _norm_matmul_kernel(x_ref, g_ref, w_ref, o_ref, xn_ref):
    @pl.when(pl.program_id(1) == 0)
    def _():
        xn_ref[...] = _rms(x_ref[...], g_ref[...]).astype(BF16)

    o_ref[...] = _dot(xn_ref[...], w_ref[...])


def norm_matmul(x, g, w, *, tm, tn):
    n, k = x.shape
    nout = w.shape[1]
    return pl.pallas_call(
        _norm_matmul_kernel,
        grid=(n // tm, nout // tn),
        in_specs=[pl.BlockSpec((tm, k), lambda i, j: (i, 0)),
                  pl.BlockSpec((1, k), lambda i, j: (0, 0)),
                  pl.BlockSpec((k, tn), lambda i, j: (0, j))],
        out_specs=pl.BlockSpec((tm, tn), lambda i, j: (i, j)),
        out_shape=jax.ShapeDtypeStruct((n, nout), F32),
        scratch_shapes=[pltpu.VMEM((tm, k), BF16)],
        compiler_params=pltpu.CompilerParams(dimension_semantics=("parallel", "arbitrary")),
        name="norm_matmul",
    )(x, g.reshape(1, k), w)


def _matmul_res_kernel(a_ref, w_ref, r_ref, o_ref):
    d = _dot(a_ref[...], w_ref[...])

    @pl.when(pl.program_id(2) == 0)
    def _():
        o_ref[...] = r_ref[...] + d

    @pl.when(pl.program_id(2) > 0)
    def _():
        o_ref[...] += d


def matmul_residual(a, w, res, *, tm, tn, tk, layer=0):
    n, k = a.shape
    nout = w.shape[2]
    return pl.pallas_call(
        _matmul_res_kernel,
        grid=(n // tm, nout // tn, k // tk),
        in_specs=[pl.BlockSpec((tm, tk), lambda i, j, kk: (i, kk)),
                  pl.BlockSpec((None, tk, tn), lambda i, j, kk: (layer, kk, j)),
                  pl.BlockSpec((tm, tn), lambda i, j, kk: (i, j))],
        out_specs=pl.BlockSpec((tm, tn), lambda i, j, kk: (i, j)),
        out_shape=jax.ShapeDtypeStruct((n, nout), F32),
        compiler_params=pltpu.CompilerParams(dimension_semantics=("parallel", "parallel", "arbitrary")),
        name="matmul_residual",
    )(a, w, res)


def _conv_act(g, p1, p2, val, cw_ref, cb_ref):
    c = cb_ref[...] + cw_ref[0:1, :] * p2 + cw_ref[1:2, :] * p1 + cw_ref[2:3, :] * g
    return (_silu(c) * val).astype(BF16)


def _ffn_up_kernel(x_ref, g_ref, wg_ref, wv_ref, cw_ref, cb_ref, s1_ref, s2_ref, act_ref, tail_ref, gs_ref,
                   xn_ref, carry_ref, *, tiles_per_seq, n_prompt_tiles, t_new):
    i, j = pl.program_id(0), pl.program_id(1)

    @pl.when(j == 0)
    def _():
        xn_ref[...] = _rms(x_ref[...], g_ref[...]).astype(BF16)

    @pl.when(i == 0)
    def _():
        carry_ref[j] = jnp.zeros(carry_ref.shape[1:], F32)

    xn = xn_ref[...]
    gate = _dot(xn, wg_ref[...])
    val = _dot(xn, wv_ref[...])
    tm = gate.shape[0]
    tail = gate[tm - SUBLANES:, :]
    tail_ref[...] = tail
    gs_ref[...] = gate
    halo = jnp.where(i % tiles_per_seq == 0, 0.0, carry_ref[j])
    carry_ref[j] = tail
    row = lax.broadcasted_iota(jnp.int32, (tm, 1), 0)
    r1 = pltpu.roll(gate, 1, 0)
    r2 = pltpu.roll(gate, 2, 0)
    p1 = jnp.where(row == 0, halo[7:8, :], r1)
    p2 = jnp.where(row == 0, halo[6:7, :], jnp.where(row == 1, halo[7:8, :], r2))
    t = row % t_new
    is_sample = i >= n_prompt_tiles
    p1 = jnp.where(is_sample, jnp.where(t >= 1, r1, s1_ref[...]), p1)
    p2 = jnp.where(is_sample, jnp.where(t >= 2, r2, s2_ref[...]), p2)
    act_ref[...] = _conv_act(gate, p1, p2, val, cw_ref, cb_ref)


def ffn_up_act(h, norm_g, w_up, conv_w, conv_b, conv_state, *, batch, seq, t_new, tm, tn, layer=0):
    n, k = h.shape
    d_ff = w_up.shape[2] // 2
    nj = d_ff // tn
    db = conv_state.shape[0]
    n_s = db * t_new
    assert n_s == tm and seq % tm == 0 and n == batch * seq + n_s
    n_tiles = n // tm
    n_prompt_tiles = n_tiles - 1
    s1 = jnp.concatenate([conv_state[:, 1:2], jnp.zeros((db, t_new - 1, d_ff), F32)], axis=1).reshape(n_s, d_ff)
    s2 = jnp.concatenate([conv_state, jnp.zeros((db, t_new - 2, d_ff), F32)], axis=1).reshape(n_s, d_ff)
    sample_col = lambda i, j: (0, jnp.where(i == n_prompt_tiles, j, 0))
    return pl.pallas_call(
        functools.partial(_ffn_up_kernel, tiles_per_seq=seq // tm, n_prompt_tiles=n_prompt_tiles, t_new=t_new),
        grid=(n_tiles, nj),
        in_specs=[pl.BlockSpec((tm, k), lambda i, j: (i, 0)),
                  pl.BlockSpec((1, k), lambda i, j: (0, 0)),
                  pl.BlockSpec((None, k, tn), lambda i, j: (layer, 0, j)),
                  pl.BlockSpec((None, k, tn), lambda i, j: (layer, 0, nj + j)),
                  pl.BlockSpec((FFN_CONV, tn), lambda i, j: (0, j)),
                  pl.BlockSpec((1, tn), lambda i, j: (0, j)),
                  pl.BlockSpec((tm, tn), sample_col),
                  pl.BlockSpec((tm, tn), sample_col)],
        out_specs=[pl.BlockSpec((tm, tn), lambda i, j: (i, j)),
                   pl.BlockSpec((SUBLANES, tn), lambda i, j: (i, j)),
                   pl.BlockSpec((tm, tn), sample_col)],
        out_shape=[jax.ShapeDtypeStruct((n, d_ff), BF16),
                   jax.ShapeDtypeStruct((n_tiles * SUBLANES, d_ff), F32),
                   jax.ShapeDtypeStruct((tm, d_ff), F32)],
        scratch_shapes=[pltpu.VMEM((tm, k), BF16), pltpu.VMEM((nj, SUBLANES, tn), F32)],
        compiler_params=pltpu.CompilerParams(dimension_semantics=("arbitrary", "arbitrary")),
        name="ffn_up_act",
    )(h, norm_g.reshape(1, k), w_up, w_up, conv_w, conv_b.reshape(1, d_ff), s1, s2)


def _gla_kernel(q_ref, k_ref, v_ref, r_ref, m_ref, w2_ref, b2_ref, on_ref, s0_ref, o_ref, sout_ref,
                st_ref, cum_ref, *, chunk, n_valid, nsub):
    i = pl.program_id(1)
    tt = nsub * chunk

    @pl.when(i == 0)
    def _():
        for h in range(GLA_HEADS):
            st_ref[h] = s0_ref[0, h].T

    a = _dot(m_ref[0].astype(BF16), w2_ref[...]) + b2_ref[...]
    a = (jnp.minimum(a, 0.0) - jnp.log(1.0 + jnp.exp(-jnp.abs(a)))) / GLA_TAU
    pos = lax.broadcasted_iota(jnp.int32, (tt, 1), 0) % chunk
    if n_valid < chunk:
        a = jnp.where(pos < n_valid, a, 0.0)
    cum = a
    sh = 1
    while sh < chunk:
        cum = cum + jnp.where(pos >= sh, pltpu.roll(cum, sh, 0), 0.0)
        sh *= 2
    cum_ref[...] = cum

    ti = lax.broadcasted_iota(jnp.int32, (chunk, 1), 0)

    def step(c, carry):
        rows = pl.ds(pl.multiple_of(c * chunk, chunk), chunk)
        for h in range(GLA_HEADS):
            kcols = slice(h * GLA_DK, (h + 1) * GLA_DK)
            vcols = slice(h * GLA_DV, (h + 1) * GLA_DV)
            qh = q_ref[0, rows, kcols] * GLA_DK ** -0.5
            kh = k_ref[0, rows, kcols]
            vh = v_ref[0, rows, vcols]
            ch = cum_ref[rows, kcols]
            st = st_ref[h]
            o = _dot_nt((qh * jnp.exp(ch)).astype(BF16), st.astype(BF16))
            for s in range(n_valid):
                lo = s // SUBLANES * SUBLANES
                d = jnp.exp(jnp.where(ti[lo:] >= s, ch[lo:] - ch[s:s + 1, :], NEG))
                w = jnp.sum(qh[lo:] * kh[s:s + 1, :] * d, axis=-1, keepdims=True)
                upd = o[lo:] + w * vh[s:s + 1, :]
                o = upd if lo == 0 else jnp.concatenate([o[:lo], upd], axis=0)
            last = ch[chunk - 1:chunk, :]
            kt = kh * jnp.exp(last - ch)
            st_ref[h] = st * jnp.exp(last) + _dot_tn(vh.astype(BF16), kt.astype(BF16))
            rh = r_ref[0, rows, vcols]
            o_ref[0, rows, vcols] = (_rms(o, on_ref[...]) * _silu(rh)).astype(o_ref.dtype)
        return carry

    lax.fori_loop(0, nsub, step, 0)

    @pl.when(i == pl.num_programs(1) - 1)
    def _():
        for h in range(GLA_HEADS):
            sout_ref[0, h] = st_ref[h].T


def gla_mixer(z3, s0, w2p, b2, onorm, *, n_seq, tiles_per_seq, chunk, n_valid, out_dtype):
    tt = z3.shape[1]
    nsub = tt // chunk
    dqk = GLA_HEADS * GLA_DK
    dv = GLA_HEADS * GLA_DV
    tile = lambda b, i: b * tiles_per_seq + i
    return pl.pallas_call(
        functools.partial(_gla_kernel, chunk=chunk, n_valid=n_valid, nsub=nsub),
        grid=(n_seq, tiles_per_seq),
        in_specs=[pl.BlockSpec((1, tt, dqk), lambda b, i: (tile(b, i), 0, AB_GQ // dqk)),
                  pl.BlockSpec((1, tt, dqk), lambda b, i: (tile(b, i), 0, AB_GK // dqk)),
                  pl.BlockSpec((1, tt, dv), lambda b, i: (tile(b, i), 0, AB_GV // dv)),
                  pl.BlockSpec((1, tt, dv), lambda b, i: (tile(b, i), 0, AB_GR // dv)),
                  pl.BlockSpec((1, tt, LANES), lambda b, i: (tile(b, i), 0, AB_MISC0 // LANES)),
                  pl.BlockSpec((LANES, dqk), lambda b, i: (0, 0)),
                  pl.BlockSpec((1, dqk), lambda b, i: (0, 0)),
                  pl.BlockSpec((1, GLA_DV), lambda b, i: (0, 0)),
                  pl.BlockSpec((1, GLA_HEADS, GLA_DK, GLA_DV), lambda b, i: (b, 0, 0, 0))],
        out_specs=[pl.BlockSpec((1, tt, dv), lambda b, i: (tile(b, i), 0, 0)),
                   pl.BlockSpec((1, GLA_HEADS, GLA_DK, GLA_DV), lambda b, i: (b, 0, 0, 0))],
        out_shape=[jax.ShapeDtypeStruct((n_seq * tiles_per_seq, tt, dv), out_dtype),
                   jax.ShapeDtypeStruct((n_seq, GLA_HEADS, GLA_DK, GLA_DV), F32)],
        scratch_shapes=[pltpu.VMEM((GLA_HEADS, GLA_DV, GLA_DK), F32), pltpu.VMEM((tt, dqk), F32)],
        compiler_params=pltpu.CompilerParams(dimension_semantics=("parallel", "arbitrary")),
        name="gla_mixer",
    )(z3, z3, z3, z3, z3, w2p, b2.reshape(1, dqk), onorm.reshape(1, GLA_DV), s0)


def _masked_softmax_rows(s, mask):
    sm = jnp.where(mask[None], s, NEG)
    m = jnp.max(sm, axis=-1, keepdims=True)
    e = jnp.where(mask[None], jnp.exp(sm - m), 0.0)
    d = jnp.sum(e, axis=-1, keepdims=True)
    return e / jnp.where(d > 0, d, 1.0)


def _block_expand(n_blocks_pad, n_keys, block, key0=0):
    bi = lax.broadcasted_iota(jnp.int32, (n_blocks_pad, n_keys), 0)
    ki = lax.broadcasted_iota(jnp.int32, (n_blocks_pad, n_keys), 1) + key0
    return (bi == jnp.right_shift(ki, int(math.log2(block)))).astype(BF16)


def _gelu_tanh(x):
    return x * (0.5 * (1.0 + jnp.tanh(math.sqrt(2.0 / math.pi) * (x + 0.044715 * (x * x * x)))))


def _cmp_mlp(chunk_rows, n_ch, w1_ref, cpe_ref, cb1_ref, w2_ref, cb2_ref, kg_ref):
    hid_w = NSA_CMP_HIDDEN
    out = {}
    for e in range(2):
        accp = _dot(cpe_ref[e], w1_ref[e])
        pe = accp[0:1, :hid_w] + accp[1:2, hid_w:] + cb1_ref[e]
        x = jnp.concatenate([chunk_rows(e * NSA_KV + g) for g in range(NSA_KV)], axis=0).astype(BF16)
        acc = _dot(x, w1_ref[e])
        hid = []
        for g in range(NSA_KV):
            a = acc[g * n_ch:(g + 1) * n_ch]
            hid.append(_gelu_tanh(a[:, :hid_w] + pltpu.roll(a[:, hid_w:], n_ch - 1, 0) + pe))
        ckv = _dot(jnp.concatenate(hid, axis=0).astype(BF16), w2_ref[e]) + cb2_ref[e]
        for g in range(NSA_KV):
            out[e, g] = ckv[g * n_ch:(g + 1) * n_ch]
    kc = [_rms(out[0, g], kg_ref[...]) for g in range(NSA_KV)]
    vc = [out[1, g] for g in range(NSA_KV)]
    return kc, vc


def _nsa_q4(q, qg):
    return jnp.concatenate([_rms(q[:, r * HEAD_DIM:(r + 1) * HEAD_DIM], qg) * HEAD_DIM ** -0.5
                            for r in range(NSA_REP)], axis=0)


def _nsa_cmp_branch(q4, kc, vc, qpos, n_cmp):
    nq = qpos.shape[0]
    n_ch = kc.shape[0]
    sc = _dot_nt(q4, kc.astype(BF16)).reshape(NSA_REP, nq, n_ch)
    cidx = lax.broadcasted_iota(jnp.int32, (1, n_ch), 1)
    cmask = (NSA_CMP_STRIDE * cidx + NSA_CMP_LEN - 1 <= qpos) & (cidx < n_cmp)
    p = _masked_softmax_rows(sc, cmask)
    o = _dot(p.reshape(NSA_REP * nq, n_ch).astype(BF16), vc.astype(BF16))
    return o, jnp.sum(p, axis=0)


def _nsa_select(pcs, qpos0, n_sb):
    n_ch = pcs.shape[1]
    ratio = NSA_SEL_BLOCK // NSA_CMP_STRIDE
    bi = lax.broadcasted_iota(jnp.int32, (LANES, n_ch), 0)
    ci = lax.broadcasted_iota(jnp.int32, (LANES, n_ch), 1)
    mimp = ((ci >= ratio * bi - 1) & (ci <= ratio * bi + ratio - 1)).astype(BF16)
    hi = pcs.astype(BF16)
    r1 = pcs - hi.astype(F32)
    mid = r1.astype(BF16)
    lo = (r1 - mid.astype(F32)).astype(BF16)
    imp = _dot_nt(mimp, hi) + _dot_nt(mimp, mid) + _dot_nt(mimp, lo)
    nbp = -(-n_sb // SUBLANES) * SUBLANES
    blk = lax.broadcasted_iota(jnp.int32, (nbp, 1), 0)
    qpos = qpos0 + lax.broadcasted_iota(jnp.int32, (1, LANES), 1)
    cur = jnp.right_shift(qpos, int(math.log2(NSA_SEL_BLOCK)))
    valid = (blk <= cur) & (blk < n_sb)
    forced = (blk == 0) | (blk == cur) | (blk == cur - 1)
    score = jnp.where(forced, -NEG, jnp.where(valid, imp[:nbp], NEG))
    rank = jnp.zeros((nbp, LANES), F32)
    for k in range(n_sb):
        sk = score[k:k + 1, :]
        rank = rank + jnp.where((sk > score) | ((sk == score) & (blk > k)), 1.0, 0.0)
    sel_t = jnp.where((rank < min(NSA_N_SEL, n_sb)) & (score > 0.5 * NEG), 1.0, 0.0)
    sel_t = jnp.concatenate([sel_t, jnp.zeros((LANES - nbp, LANES), F32)], axis=0)
    return sel_t.T


def _nsa_window_mask(kpos, qpos):
    return (kpos <= qpos) & (qpos - kpos < NSA_WINDOW) & (kpos >= 0)


def _nsa_gate_mix(gsig, o_cmp, o_sel, o_win, nq):
    outs = []
    for r in range(NSA_REP):
        c0 = MISC_GATE_COL + 3 * r
        rows = slice(r * nq, (r + 1) * nq)
        outs.append(gsig[:, c0:c0 + 1] * o_cmp[rows] + gsig[:, c0 + 1:c0 + 2] * o_sel[rows]
                    + gsig[:, c0 + 2:c0 + 3] * o_win[rows])
    return jnp.concatenate(outs, axis=1)


def _nsa_cmp_kernel(*refs, n_ch):
    x_refs = refs[:NSA_CMP_STRIDE]
    w1_ref, cpe_ref, cb1_ref, w2_ref, cb2_ref, kg_ref, kc_ref, vc_ref = refs[NSA_CMP_STRIDE:]

    def chunk_rows(eg):
        return jnp.concatenate([x_refs[s][:, eg * HEAD_DIM:(eg + 1) * HEAD_DIM] for s in range(NSA_CMP_STRIDE)], axis=1)

    kc, vc = _cmp_mlp(chunk_rows, n_ch, w1_ref, cpe_ref, cb1_ref, w2_ref, cb2_ref, kg_ref)
    for g in range(NSA_KV):
        kc_ref[0, g] = kc[g]
        vc_ref[0, g] = vc[g]


def _cmp_weight_specs():
    z = (0,) * 8
    full = lambda shape: pl.BlockSpec(shape, lambda *a: z[:len(shape)])
    taps = NSA_CMP_STRIDE * HEAD_DIM
    return [full((2, taps, 2 * NSA_CMP_HIDDEN)), full((2, SUBLANES, taps)),
            full((2, 1, NSA_CMP_HIDDEN)), full((2, NSA_CMP_HIDDEN, HEAD_DIM)), full((2, 1, HEAD_DIM)),
            full((1, HEAD_DIM))]


def nsa_cmp_prompt(z, cmpw, *, batch, seq):
    n_ch = seq // NSA_CMP_STRIDE
    seg = 4 * HEAD_DIM
    st = NSA_CMP_STRIDE
    zc = z[:batch * seq, AB_CMP:AB_CMP + seg].reshape(batch * n_ch, st * seg)
    shp = jax.ShapeDtypeStruct((batch, NSA_KV, n_ch, HEAD_DIM), F32)
    ospec = pl.BlockSpec((1, NSA_KV, n_ch, HEAD_DIM), lambda b: (b, 0, 0, 0))
    xspec = lambda s: pl.BlockSpec((n_ch, seg), lambda b: (b, s))
    return pl.pallas_call(
        functools.partial(_nsa_cmp_kernel, n_ch=n_ch),
        grid=(batch,),
        in_specs=[xspec(s) for s in range(st)] + _cmp_weight_specs(),
        out_specs=[ospec, ospec],
        out_shape=[shp, shp],
        compiler_params=pltpu.CompilerParams(dimension_semantics=("parallel",)),
        name="nsa_cmp_prompt",
    )(*([zc] * st), *cmpw)


def _nsa_prep_kernel(sel_ref, win_ref, kg_ref, o_ref):
    d = HEAD_DIM
    parts = []
    for ref, row in ((sel_ref, 1), (win_ref, 2)):
        x = ref[...]
        for g in range(NSA_KV):
            parts.append(_rms(x[:, g * d:(g + 1) * d], kg_ref[row:row + 1, :]))
        parts.append(x[:, NSA_KV * d:])
    o_ref[...] = jnp.concatenate(parts, axis=1).astype(BF16)


def nsa_prep_prompt(z, k_gain, *, rows, tm):
    seg = 4 * HEAD_DIM
    return pl.pallas_call(
        _nsa_prep_kernel,
        grid=(rows // tm,),
        in_specs=[pl.BlockSpec((tm, seg), lambda i: (i, AB_SEL // seg)),
                  pl.BlockSpec((tm, seg), lambda i: (i, AB_WIN // seg)),
                  pl.BlockSpec((3, HEAD_DIM), lambda i: (0, 0))],
        out_specs=pl.BlockSpec((tm, 2 * seg), lambda i: (i, 0)),
        out_shape=jax.ShapeDtypeStruct((rows, 2 * seg), BF16),
        compiler_params=pltpu.CompilerParams(dimension_semantics=("parallel",)),
        name="nsa_prep_prompt",
    )(z, z, k_gain)


def _nsa_attn_kernel(q_ref, m_ref, kc_ref, vc_ref, ks_ref, vs_ref, kw_ref, vw_ref, oh_ref, qg_ref, o_ref, *, seq, kt):
    i = pl.program_id(2)
    nq = q_ref.shape[0]
    assert nq == LANES
    q4 = _nsa_q4(q_ref[...], qg_ref[...]).astype(BF16)
    qpos = i * nq + lax.broadcasted_iota(jnp.int32, (nq, 1), 0)
    n_ch = seq // NSA_CMP_STRIDE
    n_sb = -(-seq // NSA_SEL_BLOCK)

    o_cmp, pcs = _nsa_cmp_branch(q4, kc_ref[0, 0], vc_ref[0, 0], qpos, n_ch - 1)
    sel = _nsa_select(pcs, i * nq, n_sb)
    bias = jnp.where(sel > 0.5, 0.0, MASK_BIAS).astype(BF16)
    q_aug = jnp.concatenate([q4, jnp.concatenate([bias] * NSA_REP, axis=0)], axis=1)

    def sel_step(j, carry, causal):
        m, l, acc = carry
        rows = pl.ds(pl.multiple_of(j * kt, kt), kt)
        k_aug = jnp.concatenate([ks_ref[rows, :], oh_ref[rows, :]], axis=1)
        s = _dot_nt(q_aug, k_aug).reshape(NSA_REP, nq, kt)
        if causal:
            kpos = j * kt + lax.broadcasted_iota(jnp.int32, (1, kt), 1)
            s = jnp.where((kpos <= qpos)[None], s, NEG)
        mn = jnp.maximum(m, jnp.max(s, axis=-1, keepdims=True))
        p = jnp.exp(s - mn)
        alpha = jnp.exp(m - mn)
        l = alpha * l + jnp.sum(p, axis=-1, keepdims=True)
        pv = _dot(p.reshape(NSA_REP * nq, kt).astype(BF16), vs_ref[rows, :])
        return mn, l, alpha * acc + pv.reshape(NSA_REP, nq, HEAD_DIM)

    last = (i * nq + nq - 1) // kt
    init = (jnp.full((NSA_REP, nq, 1), NEG, F32), jnp.zeros((NSA_REP, nq, 1), F32),
            jnp.zeros((NSA_REP, nq, HEAD_DIM), F32))
    carry = lax.fori_loop(0, last, functools.partial(sel_step, causal=False), init)
    _, l, acc = sel_step(last, carry, True)
    o_sel = (acc / l).reshape(NSA_REP * nq, HEAD_DIM)

    span = min(NSA_WINDOW + nq, seq)
    start = pl.multiple_of(jnp.clip(i * nq - NSA_WINDOW, 0, seq - span), nq)
    s = _dot_nt(q4, kw_ref[pl.ds(start, span), :]).reshape(NSA_REP, nq, span)
    kpos = start + lax.broadcasted_iota(jnp.int32, (1, span), 1)
    p = _masked_softmax_rows(s, _nsa_window_mask(kpos, qpos))
    o_win = _dot(p.reshape(NSA_REP * nq, span).astype(BF16), vw_ref[pl.ds(start, span), :])

    o_ref[...] = _nsa_gate_mix(jax.nn.sigmoid(m_ref[...]), o_cmp, o_sel, o_win, nq).astype(o_ref.dtype)


def nsa_attn_prompt(z, kvp, kc, vc, q_gain, *, batch, seq, tq):
    nt = seq // tq
    gw = NSA_REP * HEAD_DIM
    kt = min(4 * tq, seq)
    row = lambda b, g, i: b * nt + i
    cspec = pl.BlockSpec((1, 1, seq // NSA_CMP_STRIDE, HEAD_DIM), lambda b, g, i: (b, g, 0, 0))
    kvspec = lambda c: pl.BlockSpec((seq, HEAD_DIM), lambda b, g, i: (b, c + g))
    return pl.pallas_call(
        functools.partial(_nsa_attn_kernel, seq=seq, kt=kt),
        grid=(batch, NSA_KV, nt),
        in_specs=[pl.BlockSpec((tq, gw), lambda b, g, i: (row(b, g, i), AB_NQ // gw + g)),
                  pl.BlockSpec((tq, LANES), lambda b, g, i: (row(b, g, i), AB_MISC0 // LANES + g)),
                  cspec, cspec, kvspec(0), kvspec(2), kvspec(4), kvspec(6),
                  pl.BlockSpec((seq, LANES), lambda b, g, i: (0, 0)),
                  pl.BlockSpec((1, HEAD_DIM), lambda b, g, i: (0, 0))],
        out_specs=pl.BlockSpec((tq, gw), lambda b, g, i: (row(b, g, i), g)),
        out_shape=jax.ShapeDtypeStruct((batch * seq, NSA_KV * gw), BF16),
        compiler_params=pltpu.CompilerParams(dimension_semantics=("parallel", "parallel", "arbitrary")),
        name="nsa_attn_prompt",
    )(z, z, kc, vc, kvp, kvp, kvp, kvp, _block_onehot(seq, NSA_SEL_BLOCK), q_gain.reshape(1, HEAD_DIM))


def _block_onehot(n_keys, block):
    k = lax.broadcasted_iota(jnp.int32, (n_keys, LANES), 0) // block
    return (k == lax.broadcasted_iota(jnp.int32, (n_keys, LANES), 1)).astype(BF16)


def nsa_cmp_weights(cw1, cb1, cw2, cb2, cpe, k_gain):
    st = NSA_CMP_STRIDE
    taps = st * HEAD_DIM
    w1cat = jnp.concatenate([cw1[:, :st], cw1[:, st:]], axis=-1).reshape(2, taps, 2 * NSA_CMP_HIDDEN).astype(BF16)
    pe_rows = jnp.stack([cpe[:, :st].reshape(2, taps), cpe[:, st:].reshape(2, taps)], axis=1)
    pe_rows = jnp.pad(pe_rows, ((0, 0), (0, SUBLANES - 2), (0, 0))).astype(BF16)
    return (w1cat, pe_rows, cb1.reshape(2, 1, NSA_CMP_HIDDEN), cw2.astype(BF16), cb2.reshape(2, 1, HEAD_DIM),
            k_gain[0].reshape(1, HEAD_DIM))


NSA_ROW = 4 * NSA_KV * HEAD_DIM
CHUNKS_PER_PAGE = PAGE_SIZE // NSA_CMP_STRIDE


def _nsa_sample_kernel(pt_ref, *refs, n_pages, past, t_new):
    del pt_ref
    pages = refs[:n_pages]
    (q_ref, sel_ref, win_ref, m0_ref, m1_ref, wb_ref, w1_ref, cpe_ref, cb1_ref, w2_ref, cb2_ref, kg0_ref,
     kg_ref, qg_ref, o_ref) = refs[n_pages:]
    nq = SAMPLE_ROWS
    d = HEAD_DIM
    n_ch = n_pages * CHUNKS_PER_PAGE
    n_cmp = (past + t_new) // NSA_CMP_STRIDE - 1
    n_sb = -(-(past + t_new) // NSA_SEL_BLOCK)
    n_sl = NSA_ROW // d
    n_wsl = 2 * NSA_KV

    def chunk_rows(j):
        return jnp.concatenate(
            [jnp.concatenate([pg[0, pl.ds(s * n_sl + j, CHUNKS_PER_PAGE, stride=NSA_CMP_STRIDE * n_sl), :]
                              for s in range(NSA_CMP_STRIDE)], axis=1) for pg in pages], axis=0)

    def past_rows(j):
        return jnp.concatenate([pg[0, pl.ds(j, PAGE_SIZE, stride=n_sl), :] for pg in pages], axis=0)

    kc, vc = _cmp_mlp(chunk_rows, n_ch, w1_ref, cpe_ref, cb1_ref, w2_ref, cb2_ref, kg0_ref)

    qpos = past + lax.broadcasted_iota(jnp.int32, (nq, 1), 0)
    nk = past + LANES
    kpos = lax.broadcasted_iota(jnp.int32, (1, nk), 1)
    esel = _block_expand(LANES, nk, NSA_SEL_BLOCK)
    causal = kpos <= qpos
    nw = NSA_WINDOW + LANES
    wmask = _nsa_window_mask(past - NSA_WINDOW + lax.broadcasted_iota(jnp.int32, (1, nw), 1), qpos)
    zpad = jnp.zeros((LANES - nq, d), F32)
    q = q_ref[0]
    sel_new = sel_ref[0]
    win_new = win_ref[0]
    kg_sel = kg_ref[1:2, :]
    kg_win = kg_ref[2:3, :]
    col = lambda x, c: x[:, c * d:(c + 1) * d]
    wb_rows = lambda j: wb_ref[0, pl.ds(j, NSA_WINDOW, stride=n_wsl), :]

    for g in range(NSA_KV):
        q4 = _nsa_q4(q[:, g * NSA_REP * d:(g + 1) * NSA_REP * d], qg_ref[...]).astype(BF16)
        o_cmp, pcs = _nsa_cmp_branch(q4, kc[g], vc[g], qpos, n_cmp)
        pcs = jnp.concatenate([pcs, jnp.zeros((LANES - nq, n_ch), F32)], axis=0)
        selb = _nsa_select(pcs, past, n_sb)[:nq].astype(BF16)

        ks = jnp.concatenate([_rms(past_rows(2 * NSA_KV + g), kg_sel), _rms(col(sel_new, g), kg_sel), zpad], axis=0)
        vs = jnp.concatenate([past_rows(3 * NSA_KV + g), col(sel_new, NSA_KV + g), zpad], axis=0)
        s = _dot_nt(q4, ks.astype(BF16)).reshape(NSA_REP, nq, nk)
        p = _masked_softmax_rows(s, (_dot(selb, esel) > 0.5) & causal)
        o_sel = _dot(p.reshape(NSA_REP * nq, nk).astype(BF16), vs.astype(BF16))

        kw = jnp.concatenate([_rms(wb_rows(g), kg_win), _rms(col(win_new, g), kg_win), zpad], axis=0)
        vw = jnp.concatenate([wb_rows(NSA_KV + g), col(win_new, NSA_KV + g), zpad], axis=0)
        s = _dot_nt(q4, kw.astype(BF16)).reshape(NSA_REP, nq, nw)
        p = _masked_softmax_rows(s, wmask)
        o_win = _dot(p.reshape(NSA_REP * nq, nw).astype(BF16), vw.astype(BF16))

        gsig = jax.nn.sigmoid((m0_ref, m1_ref)[g][0])
        o_ref[0, :, g * NSA_REP * d:(g + 1) * NSA_REP * d] = _nsa_gate_mix(gsig, o_cmp, o_sel, o_win, nq).astype(o_ref.dtype)


def nsa_sample(zs, page_table, cache_kv, cache_win, cmpw, k_gain, q_gain, *, t_new):
    db, n_pages = page_table.shape
    past = n_pages * PAGE_SIZE
    assert cache_win.shape[1] == NSA_WINDOW and (past + t_new) // NSA_CMP_STRIDE == n_pages * CHUNKS_PER_PAGE
    n_phys = cache_kv.shape[0]
    page_rows = PAGE_SIZE * NSA_ROW // HEAD_DIM
    win_rows = NSA_WINDOW * 2 * NSA_KV
    pages = cache_kv.reshape(n_phys, page_rows, HEAD_DIM)
    wb = cache_win.reshape(db, win_rows, HEAD_DIM)
    seg = 4 * HEAD_DIM
    qw = NSA_HEADS * HEAD_DIM
    zspec = lambda w, off: pl.BlockSpec((1, SAMPLE_ROWS, w), lambda b, pt: (b, 0, off // w))
    page_spec = lambda p: pl.BlockSpec((1, page_rows, HEAD_DIM), lambda b, pt: (pt[b, p], 0, 0))
    grid_spec = pltpu.PrefetchScalarGridSpec(
        num_scalar_prefetch=1,
        grid=(db,),
        in_specs=[page_spec(p) for p in range(n_pages)]
        + [zspec(qw, AB_NQ), zspec(seg, AB_SEL), zspec(seg, AB_WIN), zspec(LANES, AB_MISC0), zspec(LANES, AB_MISC1),
           pl.BlockSpec((1, win_rows, HEAD_DIM), lambda b, pt: (b, 0, 0))]
        + _cmp_weight_specs()
        + [pl.BlockSpec((3, HEAD_DIM), lambda b, pt: (0, 0)), pl.BlockSpec((1, HEAD_DIM), lambda b, pt: (0, 0))],
        out_specs=pl.BlockSpec((1, SAMPLE_ROWS, qw), lambda b, pt: (b, 0, 0)),
    )
    return pl.pallas_call(
        functools.partial(_nsa_sample_kernel, n_pages=n_pages, past=past, t_new=t_new),
        grid_spec=grid_spec,
        out_shape=jax.ShapeDtypeStruct((db, SAMPLE_ROWS, qw), BF16),
        compiler_params=pltpu.CompilerParams(dimension_semantics=("parallel",)),
        name="nsa_sample",
    )(page_table, *([pages] * n_pages), zs, zs, zs, zs, zs, wb, *cmpw, k_gain, q_gain.reshape(1, HEAD_DIM))


C_Q, C_K, C_V, C_TOTAL = 0, MOBA_HEADS * HEAD_DIM, (MOBA_HEADS + MOBA_KV) * HEAD_DIM, (MOBA_HEADS + 2 * MOBA_KV) * HEAD_DIM
MOBA_SUBTILE = 1024


def _masked_softmax2d(s, mask):
    sm = jnp.where(mask, s, NEG)
    m = jnp.max(sm, axis=-1, keepdims=True)
    e = jnp.where(mask, jnp.exp(sm - m), 0.0)
    dsum = jnp.sum(e, axis=-1, keepdims=True)
    return e / jnp.where(dsum > 0, dsum, 1.0)


def _moba_q4(q, qg):
    return jnp.concatenate([_rms(q[:, r * HEAD_DIM:(r + 1) * HEAD_DIM], qg) * HEAD_DIM ** -0.5
                            for r in range(MOBA_REP)], axis=0)


def _moba_select(q4, kmean_pad, cur, n_blocks):
    gs = _dot_nt(q4, kmean_pad.astype(BF16))
    blk = lax.broadcasted_iota(jnp.int32, (1, LANES), 1)
    score = jnp.where((blk < cur) & (blk < n_blocks), gs, NEG)
    rank = jnp.zeros(score.shape, F32)
    for k in range(n_blocks):
        sk = score[:, k:k + 1]
        rank = rank + jnp.where((sk > score) | ((sk == score) & (blk > k)), 1.0, 0.0)
    sel = jnp.where((rank < min(MOBA_TOPK, n_blocks)) & (score > 0.5 * NEG), 1.0, 0.0)
    return jnp.where(blk == cur, 1.0, sel)


def _moba_prep_kernel(k_ref, v_ref, kg_ref, o_ref, km_ref):
    k = k_ref[...]
    kn = jnp.concatenate([_rms(k[:, h * HEAD_DIM:(h + 1) * HEAD_DIM], kg_ref[...]) for h in range(MOBA_KV)], axis=1)
    km_ref[0] = jnp.mean(kn, axis=0, keepdims=True)
    o_ref[...] = jnp.concatenate([kn, v_ref[...]], axis=1).astype(BF16)


def moba_prep_prompt(zc, k_gain, *, rows):
    w = MOBA_KV * HEAD_DIM
    nblk = rows // MOBA_BLOCK
    return pl.pallas_call(
        _moba_prep_kernel,
        grid=(nblk,),
        in_specs=[pl.BlockSpec((MOBA_BLOCK, w), lambda i: (i, C_K // w)),
                  pl.BlockSpec((MOBA_BLOCK, w), lambda i: (i, C_V // w)),
                  pl.BlockSpec((1, HEAD_DIM), lambda i: (0, 0))],
        out_specs=[pl.BlockSpec((MOBA_BLOCK, 2 * w), lambda i: (i, 0)),
                   pl.BlockSpec((1, 1, w), lambda i: (i, 0, 0))],
        out_shape=[jax.ShapeDtypeStruct((rows, 2 * w), BF16), jax.ShapeDtypeStruct((nblk, 1, w), F32)],
        compiler_params=pltpu.CompilerParams(dimension_semantics=("parallel",)),
        name="moba_prep_prompt",
    )(zc, zc, k_gain.reshape(1, HEAD_DIM))


def _moba_attn_kernel(q_ref, km_ref, k_ref, v_ref, oh_ref, qg_ref, o_ref, *, seq):
    i = pl.program_id(2)
    nq = q_ref.shape[0]
    assert nq == MOBA_BLOCK
    nb = seq // MOBA_BLOCK
    rq = MOBA_REP * nq
    q4 = _moba_q4(q_ref[...], qg_ref[...]).astype(BF16)

    km = jnp.concatenate([km_ref[0], jnp.zeros((LANES - nb, HEAD_DIM), F32)], axis=0)
    nbp = -(-nb // SUBLANES) * SUBLANES
    gs = _dot_nt(km.astype(BF16), q4)[:nbp]
    blk = lax.broadcasted_iota(jnp.int32, (nbp, 1), 0)
    score = jnp.where((blk < i) & (blk < nb), gs, NEG)
    rank = jnp.zeros((nbp, rq), F32)
    for k in range(nb):
        sk = score[k:k + 1, :]
        rank = rank + jnp.where((sk > score) | ((sk == score) & (blk > k)), 1.0, 0.0)
    chosen = ((rank < min(MOBA_TOPK, nb)) & (score > 0.5 * NEG)) | (blk == i)
    bias_t = jnp.concatenate([jnp.where(chosen, 0.0, MASK_BIAS), jnp.zeros((LANES - nbp, rq), F32)], axis=0)
    q_aug = jnp.concatenate([q4, bias_t.T.astype(BF16)], axis=1)

    def update(carry, s, rows):
        m, l, acc = carry
        mn = jnp.maximum(m, jnp.max(s, axis=-1, keepdims=True))
        p = jnp.exp(s - mn)
        alpha = jnp.exp(m - mn)
        l = alpha * l + jnp.sum(p, axis=-1, keepdims=True)
        return mn, l, alpha * acc + _dot(p.astype(BF16), v_ref[rows, :])

    own = pl.ds(pl.multiple_of(i * MOBA_BLOCK, MOBA_BLOCK), MOBA_BLOCK)
    kcol = lax.broadcasted_iota(jnp.int32, (1, MOBA_BLOCK), 1)
    rs = min(MOBA_SUBTILE, rq)
    outs = []
    for t in range(rq // rs):
        qa = q_aug[t * rs:(t + 1) * rs]

        def past_step(n, carry, qa=qa):
            rows = pl.ds(pl.multiple_of(n * MOBA_BLOCK, MOBA_BLOCK), MOBA_BLOCK)
            k_aug = jnp.concatenate([k_ref[rows, :], oh_ref[rows, :]], axis=1)
            return update(carry, _dot_nt(qa, k_aug), rows)

        init = (jnp.full((rs, 1), NEG, F32), jnp.zeros((rs, 1), F32), jnp.zeros((rs, HEAD_DIM), F32))
        carry = lax.fori_loop(0, i, past_step, init)
        tq = (t * rs + lax.broadcasted_iota(jnp.int32, (rs, 1), 0)) % nq
        s_own = jnp.where(kcol <= tq, _dot_nt(qa[:, :HEAD_DIM], k_ref[own, :]), NEG)
        _, l, acc = update(carry, s_own, own)
        outs.append(acc / l)
    o = jnp.concatenate(outs, axis=0)
    o_ref[...] = jnp.concatenate([o[r * nq:(r + 1) * nq] for r in range(MOBA_REP)], axis=1).astype(o_ref.dtype)


def moba_attn_prompt(zc, kvp, kmean, q_gain, *, batch, seq):
    nt = seq // MOBA_BLOCK
    gw = MOBA_REP * HEAD_DIM
    km = kmean.reshape(batch, nt, MOBA_KV * HEAD_DIM)
    return pl.pallas_call(
        functools.partial(_moba_attn_kernel, seq=seq),
        grid=(batch, MOBA_KV, nt),
        in_specs=[pl.BlockSpec((MOBA_BLOCK, gw), lambda b, g, i: (b * nt + i, g)),
                  pl.BlockSpec((1, nt, HEAD_DIM), lambda b, g, i: (b, 0, g)),
                  pl.BlockSpec((seq, HEAD_DIM), lambda b, g, i: (b, g)),
                  pl.BlockSpec((seq, HEAD_DIM), lambda b, g, i: (b, MOBA_KV + g)),
                  pl.BlockSpec((seq, LANES), lambda b, g, i: (0, 0)),
                  pl.BlockSpec((1, HEAD_DIM), lambda b, g, i: (0, 0))],
        out_specs=pl.BlockSpec((MOBA_BLOCK, gw), lambda b, g, i: (b * nt + i, g)),
        out_shape=jax.ShapeDtypeStruct((batch * seq, MOBA_HEADS * HEAD_DIM), BF16),
        compiler_params=pltpu.CompilerParams(dimension_semantics=("parallel", "parallel", "arbitrary")),
        name="moba_attn_prompt",
    )(zc, km, kvp, kvp, _block_onehot(seq, MOBA_BLOCK), q_gain.reshape(1, HEAD_DIM))


def _moba_sample_kernel(pt_ref, *refs, n_pages, past, t_new):
    del pt_ref
    pages = refs[:n_pages]
    q_ref, kn_ref, vn_ref, kg_ref, qg_ref, o_ref = refs[n_pages:]
    nq = SAMPLE_ROWS
    d = HEAD_DIM
    rq = MOBA_REP * nq
    nb = -(-(past + t_new) // MOBA_BLOCK)
    n_past_blocks = past // MOBA_BLOCK
    nk = past + LANES
    qpos = past + lax.broadcasted_iota(jnp.int32, (nq, 1), 0)
    qpos4 = jnp.concatenate([qpos] * MOBA_REP, axis=0)
    cur = jnp.right_shift(qpos4, int(math.log2(MOBA_BLOCK)))
    kpos = lax.broadcasted_iota(jnp.int32, (1, nk), 1)
    expand = _block_expand(LANES, nk, MOBA_BLOCK)
    causal = kpos <= qpos4
    zpad = jnp.zeros((LANES - nq, d), F32)
    q = q_ref[0]
    k_new = kn_ref[0]
    v_new = vn_ref[0]
    n_sl = 2 * MOBA_KV
    past_rows = lambda j: jnp.concatenate([pg[0, pl.ds(j, PAGE_SIZE, stride=n_sl), :] for pg in pages], axis=0)
    for g in range(MOBA_KV):
        kn = _rms(past_rows(g), kg_ref[...])
        km = jnp.concatenate([jnp.mean(kn[n * MOBA_BLOCK:(n + 1) * MOBA_BLOCK], axis=0, keepdims=True)
                              for n in range(n_past_blocks)] + [jnp.zeros((LANES - n_past_blocks, d), F32)], axis=0)
        q4 = _moba_q4(q[:, g * MOBA_REP * d:(g + 1) * MOBA_REP * d], qg_ref[...]).astype(BF16)
        sel = _moba_select(q4, km, cur, nb)
        keys = jnp.concatenate([kn, _rms(k_new[:, g * d:(g + 1) * d], kg_ref[...]), zpad], axis=0)
        vals = jnp.concatenate([past_rows(MOBA_KV + g), v_new[:, g * d:(g + 1) * d], zpad], axis=0)
        s = _dot_nt(q4, keys.astype(BF16))
        p = _masked_softmax2d(s, (_dot(sel.astype(BF16), expand) > 0.5) & causal)
        o = _dot(p.astype(BF16), vals.astype(BF16))
        for r in range(MOBA_REP):
            h = g * MOBA_REP + r
            o_ref[0, :, h * d:(h + 1) * d] = o[r * nq:(r + 1) * nq].astype(o_ref.dtype)


def moba_sample(zcs, page_table, cache_kv, k_gain, q_gain, *, t_new):
    db, n_pages = page_table.shape
    past = n_pages * PAGE_SIZE
    assert past % MOBA_BLOCK == 0 and t_new <= MOBA_BLOCK
    n_phys = cache_kv.shape[0]
    w = MOBA_KV * HEAD_DIM
    page_rows = PAGE_SIZE * 2 * MOBA_KV
    pages = cache_kv.reshape(n_phys, page_rows, HEAD_DIM)
    qw = MOBA_HEADS * HEAD_DIM
    zspec = lambda wd, off: pl.BlockSpec((1, SAMPLE_ROWS, wd), lambda b, pt: (b, 0, off // wd))
    page_spec = lambda p: pl.BlockSpec((1, page_rows, HEAD_DIM), lambda b, pt: (pt[b, p], 0, 0))
    grid_spec = pltpu.PrefetchScalarGridSpec(
        num_scalar_prefetch=1,
        grid=(db,),
        in_specs=[page_spec(p) for p in range(n_pages)]
        + [zspec(qw, C_Q), zspec(w, C_K), zspec(w, C_V),
           pl.BlockSpec((1, HEAD_DIM), lambda b, pt: (0, 0)), pl.BlockSpec((1, HEAD_DIM), lambda b, pt: (0, 0))],
        out_specs=pl.BlockSpec((1, SAMPLE_ROWS, qw), lambda b, pt: (b, 0, 0)),
    )
    return pl.pallas_call(
        functools.partial(_moba_sample_kernel, n_pages=n_pages, past=past, t_new=t_new),
        grid_spec=grid_spec,
        out_shape=jax.ShapeDtypeStruct((db, SAMPLE_ROWS, qw), BF16),
        compiler_params=pltpu.CompilerParams(dimension_semantics=("parallel",)),
        name="moba_sample",
    )(page_table, *([pages] * n_pages), zcs, zcs, zcs, k_gain.reshape(1, HEAD_DIM), q_gain.reshape(1, HEAD_DIM))


ROW_TILE = 512
GLA_TILE = 128
NSA_Q_TILE = 128


def _ab_weight_layout(w_in_ab):
    d = w_in_ab.shape[0]
    widths = (GLA_HEADS * GLA_DK, GLA_HEADS * GLA_DK, GLA_HEADS * GLA_DV, GLA_HEADS * GLA_DV, GLA_GATE_RANK,
              NSA_HEADS * HEAD_DIM, NSA_HEADS * 3, 6 * NSA_KV * HEAD_DIM)
    offs = [0]
    for w in widths:
        offs.append(offs[-1] + w)
    gq, gk, gv, gr, ga, nq, ngt, nkv = (w_in_ab[:, offs[i]:offs[i + 1]] for i in range(len(widths)))
    half = NSA_REP * 3
    zeros = lambda n: jnp.zeros((d, n), w_in_ab.dtype)
    misc0 = jnp.concatenate([ga, ngt[:, :half], zeros(LANES - GLA_GATE_RANK - half)], axis=1)
    misc1 = jnp.concatenate([zeros(GLA_GATE_RANK), ngt[:, half:], zeros(LANES - GLA_GATE_RANK - half)], axis=1)
    w = jnp.concatenate([gq, gk, gv, gr, nq, nkv, misc0, misc1, zeros(AB_TOTAL - AB_MISC1 - LANES)], axis=1)
    return w.astype(BF16)


def _pad_sample_rows(z, db, t_new):
    return jnp.pad(z.reshape(db, t_new, z.shape[-1]), ((0, 0), (0, SAMPLE_ROWS - t_new), (0, 0)))


def _matmul_rows(n):
    return n // 8 if n % (8 * 2 * SUBLANES) == 0 else ROW_TILE


def _conv_ffn(h, l, n_p, batch, seq, t_new, state_ffn_conv, norm_ffn, w_up_bf16, ffn_conv_w, ffn_conv_b, w_down_bf16):
    d_ff = ffn_conv_w.shape[-1]
    db = state_ffn_conv.shape[1]
    act, tails, gate_s = ffn_up_act(h, norm_ffn[l], w_up_bf16, ffn_conv_w[l], ffn_conv_b[l], state_ffn_conv[l],
                                    batch=batch, seq=seq, t_new=t_new, tm=ROW_TILE, tn=d_ff // 4, layer=l)
    h = matmul_residual(act, w_down_bf16, h, tm=_matmul_rows(h.shape[0]), tn=min(1024, h.shape[1]), tk=d_ff // 2, layer=l)
    keep = FFN_CONV - 1
    tps = seq // ROW_TILE
    gate_p = jnp.stack([tails[(b + 1) * tps * SUBLANES - keep:(b + 1) * tps * SUBLANES] for b in range(batch)])
    gate_s = gate_s.reshape(db, t_new, d_ff)[:, t_new - keep:]
    return h, gate_p, gate_s


def kernel(x_prompt, x_sample, page_table, cache_nsa_kv, cache_nsa_win, state_gla, cache_moba_kv, state_ffn_conv, norm_mix, w_in_ab, gla_a_w2, gla_a_b, gla_o_norm, nsa_q_norm, nsa_k_norm, nsa_cmp_w1, nsa_cmp_b1, nsa_cmp_w2, nsa_cmp_b2, nsa_cmp_pe, w_out_ab, w_in_c, moba_q_norm, moba_k_norm, w_out_c, norm_ffn, ffn_w_up, ffn_conv_w, ffn_conv_b, ffn_w_down):
    batch, seq, d_model = x_prompt.shape
    db, t_new, _ = x_sample.shape
    n_p, n_s = batch * seq, db * t_new
    n = n_p + n_s
    assert norm_mix.shape[0] == 2 and w_in_ab.shape[0] == 1 and w_in_c.shape[0] == 1
    assert FFN_CONV - 1 <= t_new <= SAMPLE_ROWS and n % ROW_TILE == 0 and n_p % ROW_TILE == 0
    ffn_args = (state_ffn_conv, norm_ffn, ffn_w_up.astype(BF16), ffn_conv_w, ffn_conv_b, ffn_w_down.astype(BF16))
    tm_mm = _matmul_rows(n)

    h = jnp.concatenate([x_prompt.reshape(n_p, d_model), x_sample.reshape(n_s, d_model)], axis=0)
    nsa_cache = cache_nsa_kv.reshape(cache_nsa_kv.shape[1:])
    nsa_win = cache_nsa_win.reshape(cache_nsa_win.shape[1:])
    gla_state = state_gla.reshape(state_gla.shape[1:])
    moba_cache = cache_moba_kv.reshape(cache_moba_kv.shape[1:])
    tn_out = min(1024, d_model)

    z = norm_matmul(h, norm_mix[0], _ab_weight_layout(w_in_ab[0]), tm=tm_mm, tn=1024)
    zs = _pad_sample_rows(z[n_p:], db, t_new)
    w2p = jnp.pad(gla_a_w2[0], ((0, LANES - GLA_GATE_RANK), (0, 0))).astype(BF16)
    gla_zero = jnp.zeros((batch,) + state_gla.shape[2:], F32)
    zg = z if n % GLA_TILE == 0 else z[:n_p]
    og_p, gla_p = gla_mixer(zg.reshape(-1, GLA_TILE, AB_TOTAL), gla_zero, w2p, gla_a_b[0], gla_o_norm[0],
                            n_seq=batch, tiles_per_seq=seq // GLA_TILE, chunk=GLA_CHUNK, n_valid=GLA_CHUNK, out_dtype=BF16)
    og_s, gla_s = gla_mixer(zs, gla_state, w2p, gla_a_b[0], gla_o_norm[0],
                            n_seq=db, tiles_per_seq=1, chunk=SAMPLE_ROWS, n_valid=t_new, out_dtype=BF16)
    cmpw = nsa_cmp_weights(nsa_cmp_w1[0], nsa_cmp_b1[0], nsa_cmp_w2[0], nsa_cmp_b2[0], nsa_cmp_pe[0], nsa_k_norm[0])
    kc, vc = nsa_cmp_prompt(z, cmpw, batch=batch, seq=seq)
    kvp = nsa_prep_prompt(z, nsa_k_norm[0], rows=n_p, tm=ROW_TILE)
    on_p = nsa_attn_prompt(z, kvp, kc, vc, nsa_q_norm[0], batch=batch, seq=seq, tq=NSA_Q_TILE)
    on_s = nsa_sample(zs, page_table, nsa_cache, nsa_win, cmpw, nsa_k_norm[0], nsa_q_norm[0], t_new=t_new)
    mix_p = jnp.concatenate([og_p.reshape(n_p, -1), on_p], axis=1)
    mix_s = jnp.concatenate([og_s[:, :t_new].reshape(n_s, -1), on_s[:, :t_new].reshape(n_s, -1)], axis=1)
    h = matmul_residual(jnp.concatenate([mix_p, mix_s], axis=0), w_out_ab.astype(BF16), h,
                        tm=tm_mm, tn=tn_out, tk=mix_p.shape[1])
    h, conv_p0, conv_s0 = _conv_ffn(h, 0, n_p, batch, seq, t_new, *ffn_args)

    kv_w = 4 * NSA_KV * HEAD_DIM
    win_w = 2 * NSA_KV * HEAD_DIM
    win_keep = min(NSA_WINDOW, seq)
    nsa_kv_p = z[:n_p, AB_CMP:AB_CMP + kv_w].reshape(1, batch, seq, 4, NSA_KV, HEAD_DIM)
    nsa_kv_s = z[n_p:, AB_CMP:AB_CMP + kv_w].reshape(1, db, t_new, 4, NSA_KV, HEAD_DIM)
    nsa_win_p = z[:n_p, AB_WIN:AB_WIN + win_w].reshape(batch, seq, win_w)[:, seq - win_keep:]
    nsa_win_p = nsa_win_p.reshape(1, batch, win_keep, 2, NSA_KV, HEAD_DIM)
    win_new = z[n_p:, AB_WIN:AB_WIN + win_w].reshape(db, t_new, 2, NSA_KV, HEAD_DIM)
    nsa_win_s = jnp.concatenate([nsa_win, win_new], axis=1)[None, :, -NSA_WINDOW:]

    zc = norm_matmul(h, norm_mix[1], w_in_c[0].astype(BF16), tm=tm_mm, tn=1024)
    zcs = _pad_sample_rows(zc[n_p:], db, t_new)
    kvm, kmean = moba_prep_prompt(zc, moba_k_norm[0], rows=n_p)
    om_p = moba_attn_prompt(zc, kvm, kmean, moba_q_norm[0], batch=batch, seq=seq)
    om_s = moba_sample(zcs, page_table, moba_cache, moba_k_norm[0], moba_q_norm[0], t_new=t_new)
    om = jnp.concatenate([om_p, om_s[:, :t_new].reshape(n_s, -1)], axis=0)
    h = matmul_residual(om, w_out_c.astype(BF16), h, tm=tm_mm, tn=tn_out, tk=om.shape[1])
    h, conv_p1, conv_s1 = _conv_ffn(h, 1, n_p, batch, seq, t_new, *ffn_args)

    moba_kv_p = zc[:n_p, C_K:].reshape(1, batch, seq, 2, MOBA_KV, HEAD_DIM)
    moba_kv_s = zc[n_p:, C_K:].reshape(1, db, t_new, 2, MOBA_KV, HEAD_DIM)

    return (h[:n_p].reshape(batch, seq, d_model), h[n_p:].reshape(db, t_new, d_model),
            nsa_kv_p, nsa_kv_s, nsa_win_p, nsa_win_s, gla_p[None], gla_s[None], moba_kv_p, moba_kv_s,
            jnp.stack([conv_p0, conv_p1]), jnp.stack([conv_s0, conv_s1]))
```

```python
import functools
import math

import jax
import jax.numpy as jnp
from jax import lax
from jax.experimental import pallas as pl
from jax.experimental.pallas import tpu as pltpu

F32 = jnp.float32
BF16 = jnp.bfloat16

HEAD_DIM = 128
GLA_HEADS = 4
GLA_DK = 128
GLA_DV = 256
GLA_GATE_RANK = 16
GLA_TAU = 16.0
GLA_CHUNK = 16
NSA_HEADS = 8
NSA_KV = 2
NSA_REP = NSA_HEADS // NSA_KV
NSA_CMP_STRIDE = 16
NSA_CMP_LEN = 2 * NSA_CMP_STRIDE
NSA_CMP_HIDDEN = 256
NSA_SEL_BLOCK = 64
NSA_N_SEL = 16
NSA_WINDOW = 512
MOBA_HEADS = 16
MOBA_KV = 4
MOBA_REP = MOBA_HEADS // MOBA_KV
MOBA_BLOCK = 256
MOBA_TOPK = 3
FFN_CONV = 3
NORM_EPS = 1e-6
PAGE_SIZE = 128

LANES = 128
SUBLANES = 8
NEG = -1e30
LOG2E = 1.4426950408889634
MASK_BIAS = -(2.0 ** 100)
SAMPLE_ROWS = 8

AB_GQ, AB_GK, AB_GV, AB_GR, AB_NQ = 0, 512, 1024, 2048, 3072
AB_CMP, AB_SEL, AB_WIN, AB_MISC0, AB_MISC1, AB_TOTAL = 4096, 4608, 5120, 5632, 5760, 6144
MISC_GATE_COL = GLA_GATE_RANK


def _rms(x, g):
    return x * lax.rsqrt(jnp.mean(x * x, axis=-1, keepdims=True) + NORM_EPS) * g


def _dot_nt(a, b):
    return lax.dot_general(a, b, (((1,), (1,)), ((), ())), preferred_element_type=F32)


def _dot_tn(a, b):
    return lax.dot_general(a, b, (((0,), (0,)), ((), ())), preferred_element_type=F32)


def _dot(a, b):
    return jnp.dot(a, b, preferred_element_type=F32)


def _silu(x):
    return x * jax.nn.sigmoid(x)


def _norm_matmul_kernel(x_ref, g_ref, w_ref, o_ref, xn_ref):
    @pl.when(pl.program_id(1) == 0)
    def _():
        xn_ref[...] = _rms(x_ref[...], g_ref[...]).astype(BF16)

    o_ref[...] = _dot(xn_ref[...], w_ref[...])


def norm_matmul(x, g, w, *, tm, tn):
    n, k = x.shape
    nout = w.shape[1]
    return pl.pallas_call(
        _norm_matmul_kernel,
        grid=(n // tm, nout // tn),
        in_specs=[pl.BlockSpec((tm, k), lambda i, j: (i, 0)),
                  pl.BlockSpec((1, k), lambda i, j: (0, 0)),
                  pl.BlockSpec((k, tn), lambda i, j: (0, j))],
        out_specs=pl.BlockSpec((tm, tn), lambda i, j: (i, j)),
        out_shape=jax.ShapeDtypeStruct((n, nout), F32),
        scratch_shapes=[pltpu.VMEM((tm, k), BF16)],
        compiler_params=pltpu.CompilerParams(dimension_semantics=("parallel", "arbitrary")),
        name="norm_matmul",
    )(x, g.reshape(1, k), w)


def _matmul_res_kernel(a_ref, w_ref, r_ref, o_ref):
    d = _dot(a_ref[...], w_ref[...])

    @pl.when(pl.program_id(2) == 0)
    def _():
        o_ref[...] = r_ref[...] + d

    @pl.when(pl.program_id(2) > 0)
    def _():
        o_ref[...] += d


def matmul_residual(a, w, res, *, tm, tn, tk, layer=0):
    n, k = a.shape
    nout = w.shape[2]
    return pl.pallas_call(
        _matmul_res_kernel,
        grid=(n // tm, nout // tn, k // tk),
        in_specs=[pl.BlockSpec((tm, tk), lambda i, j, kk: (i, kk)),
                  pl.BlockSpec((None, tk, tn), lambda i, j, kk: (layer, kk, j)),
                  pl.BlockSpec((tm, tn), lambda i, j, kk: (i, j))],
        out_specs=pl.BlockSpec((tm, tn), lambda i, j, kk: (i, j)),
        out_shape=jax.ShapeDtypeStruct((n, nout), F32),
        compiler_params=pltpu.CompilerParams(dimension_semantics=("parallel", "parallel", "arbitrary")),
        name="matmul_residual",
    )(a, w, res)


def _conv_act(g, p1, p2, val, cw_ref, cb_ref):
    c = cb_ref[...] + cw_ref[0:1, :] * p2 + cw_ref[1:2, :] * p1 + cw_ref[2:3, :] * g
    return (_silu(c) * val).astype(BF16)


def _ffn_up_kernel(x_ref, g_ref, wg_ref, wv_ref, cw_ref, cb_ref, s1_ref, s2_ref, act_ref, tail_ref, gs_ref,
                   xn_ref, carry_ref, *, tiles_per_seq, n_prompt_tiles, t_new):
    i, j = pl.program_id(0), pl.program_id(1)

    @pl.when(j == 0)
    def _():
        xn_ref[...] = _rms(x_ref[...], g_ref[...]).astype(BF16)

    @pl.when(i == 0)
    def _():
        carry_ref[j] = jnp.zeros(carry_ref.shape[1:], F32)

    xn = xn_ref[...]
    gate = _dot(xn, wg_ref[...])
    val = _dot(xn, wv_ref[...])
    tm = gate.shape[0]
    tail = gate[tm - SUBLANES:, :]
    tail_ref[...] = tail
    gs_ref[...] = gate
    halo = jnp.where(i % tiles_per_seq == 0, 0.0, carry_ref[j])
    carry_ref[j] = tail
    row = lax.broadcasted_iota(jnp.int32, (tm, 1), 0)
    r1 = pltpu.roll(gate, 1, 0)
    r2 = pltpu.roll(gate, 2, 0)
    p1 = jnp.where(row == 0, halo[7:8, :], r1)
    p2 = jnp.where(row == 0, halo[6:7, :], jnp.where(row == 1, halo[7:8, :], r2))
    t = row % t_new
    is_sample = i >= n_prompt_tiles
    p1 = jnp.where(is_sample, jnp.where(t >= 1, r1, s1_ref[...]), p1)
    p2 = jnp.where(is_sample, jnp.where(t >= 2, r2, s2_ref[...]), p2)
    act_ref[...] = _conv_act(gate, p1, p2, val, cw_ref, cb_ref)


def ffn_up_act(h, norm_g, w_up, conv_w, conv_b, conv_state, *, batch, seq, t_new, tm, tn, layer=0):
    n, k = h.shape
    d_ff = w_up.shape[2] // 2
    nj = d_ff // tn
    db = conv_state.shape[0]
    n_s = db * t_new
    assert n_s == tm and seq % tm == 0 and n == batch * seq + n_s
    n_tiles = n // tm
    n_prompt_tiles = n_tiles - 1
    s1 = jnp.concatenate([conv_state[:, 1:2], jnp.zeros((db, t_new - 1, d_ff), F32)], axis=1).reshape(n_s, d_ff)
    s2 = jnp.concatenate([conv_state, jnp.zeros((db, t_new - 2, d_ff), F32)], axis=1).reshape(n_s, d_ff)
    sample_col = lambda i, j: (0, jnp.where(i == n_prompt_tiles, j, 0))
    return pl.pallas_call(
        functools.partial(_ffn_up_kernel, tiles_per_seq=seq // tm, n_prompt_tiles=n_prompt_tiles, t_new=t_new),
        grid=(n_tiles, nj),
        in_specs=[pl.BlockSpec((tm, k), lambda i, j: (i, 0)),
                  pl.BlockSpec((1, k), lambda i, j: (0, 0)),
                  pl.BlockSpec((None, k, tn), lambda i, j: (layer, 0, j)),
                  pl.BlockSpec((None, k, tn), lambda i, j: (layer, 0, nj + j)),
                  pl.BlockSpec((FFN_CONV, tn), lambda i, j: (0, j)),
                  pl.BlockSpec((1, tn), lambda i, j: (0, j)),
                  pl.BlockSpec((tm, tn), sample_col),
                  pl.BlockSpec((tm, tn), sample_col)],
        out_specs=[pl.BlockSpec((tm, tn), lambda i, j: (i, j)),
                   pl.BlockSpec((SUBLANES, tn), lambda i, j: (i, j)),
                   pl.BlockSpec((tm, tn), sample_col)],
        out_shape=[jax.ShapeDtypeStruct((n, d_ff), BF16),
                   jax.ShapeDtypeStruct((n_tiles * SUBLANES, d_ff), F32),
                   jax.ShapeDtypeStruct((tm, d_ff), F32)],
        scratch_shapes=[pltpu.VMEM((tm, k), BF16), pltpu.VMEM((nj, SUBLANES, tn), F32)],
        compiler_params=pltpu.CompilerParams(dimension_semantics=("arbitrary", "arbitrary")),
        name="ffn_up_act",
    )(h, norm_g.reshape(1, k), w_up, w_up, conv_w, conv_b.reshape(1, d_ff), s1, s2)


def _gla_kernel(q_ref, k_ref, v_ref, r_ref, m_ref, w2_ref, b2_ref, on_ref, s0_ref, o_ref, sout_ref,
                st_ref, cum_ref, *, chunk, n_valid, nsub):
    i = pl.program_id(1)
    tt = nsub * chunk

    @pl.when(i == 0)
    def _():
        for h in range(GLA_HEADS):
            st_ref[h] = s0_ref[0, h].T

    a = _dot(m_ref[0].astype(BF16), w2_ref[...]) + b2_ref[...]
    a = (jnp.minimum(a, 0.0) - jnp.log(1.0 + jnp.exp(-jnp.abs(a)))) / GLA_TAU
    pos = lax.broadcasted_iota(jnp.int32, (tt, 1), 0) % chunk
    if n_valid < chunk:
        a = jnp.where(pos < n_valid, a, 0.0)
    cum = a
    sh = 1
    while sh < chunk:
        cum = cum + jnp.where(pos >= sh, pltpu.roll(cum, sh, 0), 0.0)
        sh *= 2
    cum_ref[...] = cum

    ti = lax.broadcasted_iota(jnp.int32, (chunk, 1), 0)

    def step(c, carry):
        rows = pl.ds(pl.multiple_of(c * chunk, chunk), chunk)
        for h in range(GLA_HEADS):
            kcols = slice(h * GLA_DK, (h + 1) * GLA_DK)
            vcols = slice(h * GLA_DV, (h + 1) * GLA_DV)
            qh = q_ref[0, rows, kcols] * GLA_DK ** -0.5
            kh = k_ref[0, rows, kcols]
            vh = v_ref[0, rows, vcols]
            ch = cum_ref[rows, kcols]
            st = st_ref[h]
            o = _dot_nt((qh * jnp.exp(ch)).astype(BF16), st.astype(BF16))
            for s in range(n_valid):
                lo = s // SUBLANES * SUBLANES
                d = jnp.exp(jnp.where(ti[lo:] >= s, ch[lo:] - ch[s:s + 1, :], NEG))
                w = jnp.sum(qh[lo:] * kh[s:s + 1, :] * d, axis=-1, keepdims=True)
                upd = o[lo:] + w * vh[s:s + 1, :]
                o = upd if lo == 0 else jnp.concatenate([o[:lo], upd], axis=0)
            last = ch[chunk - 1:chunk, :]
            kt = kh * jnp.exp(last - ch)
            st_ref[h] = st * jnp.exp(last) + _dot_tn(vh.astype(BF16), kt.astype(BF16))
            rh = r_ref[0, rows, vcols]
            o_ref[0, rows, vcols] = (_rms(o, on_ref[...]) * _silu(rh)).astype(o_ref.dtype)
        return carry

    lax.fori_loop(0, nsub, step, 0)

    @pl.when(i == pl.num_programs(1) - 1)
    def _():
        for h in range(GLA_HEADS):
            sout_ref[0, h] = st_ref[h].T


def gla_mixer(z3, s0, w2p, b2, onorm, *, n_seq, tiles_per_seq, chunk, n_valid, out_dtype):
    tt = z3.shape[1]
    nsub = tt // chunk
    dqk = GLA_HEADS * GLA_DK
    dv = GLA_HEADS * GLA_DV
    tile = lambda b, i: b * tiles_per_seq + i
    return pl.pallas_call(
        functools.partial(_gla_kernel, chunk=chunk, n_valid=n_valid, nsub=nsub),
        grid=(n_seq, tiles_per_seq),
        in_specs=[pl.BlockSpec((1, tt, dqk), lambda b, i: (tile(b, i), 0, AB_GQ // dqk)),
                  pl.BlockSpec((1, tt, dqk), lambda b, i: (tile(b, i), 0, AB_GK // dqk)),
                  pl.BlockSpec((1, tt, dv), lambda b, i: (tile(b, i), 0, AB_GV // dv)),
                  pl.BlockSpec((1, tt, dv), lambda b, i: (tile(b, i), 0, AB_GR // dv)),
                  pl.BlockSpec((1, tt, LANES), lambda b, i: (tile(b, i), 0, AB_MISC0 // LANES)),
                  pl.BlockSpec((LANES, dqk), lambda b, i: (0, 0)),
                  pl.BlockSpec((1, dqk), lambda b, i: (0, 0)),
                  pl.BlockSpec((1, GLA_DV), lambda b, i: (0, 0)),
                  pl.BlockSpec((1, GLA_HEADS, GLA_DK, GLA_DV), lambda b, i: (b, 0, 0, 0))],
        out_specs=[pl.BlockSpec((1, tt, dv), lambda b, i: (tile(b, i), 0, 0)),
                   pl.BlockSpec((1, GLA_HEADS, GLA_DK, GLA_DV), lambda b, i: (b, 0, 0, 0))],
        out_shape=[jax.ShapeDtypeStruct((n_seq * tiles_per_seq, tt, dv), out_dtype),
                   jax.ShapeDtypeStruct((n_seq, GLA_HEADS, GLA_DK, GLA_DV), F32)],
        scratch_shapes=[pltpu.VMEM((GLA_HEADS, GLA_DV, GLA_DK), F32), pltpu.VMEM((tt, dqk), F32)],
        compiler_params=pltpu.CompilerParams(dimension_semantics=("parallel", "arbitrary")),
        name="gla_mixer",
    )(z3, z3, z3, z3, z3, w2p, b2.reshape(1, dqk), onorm.reshape(1, GLA_DV), s0)


def _masked_softmax_rows(s, mask):
    sm = jnp.where(mask[None], s, NEG)
    m = jnp.max(sm, axis=-1, keepdims=True)
    e = jnp.where(mask[None], jnp.exp2(sm - m), 0.0)
    d = jnp.sum(e, axis=-1, keepdims=True)
    return e / jnp.where(d > 0, d, 1.0)


def _block_expand(n_blocks_pad, n_keys, block, key0=0):
    bi = lax.broadcasted_iota(jnp.int32, (n_blocks_pad, n_keys), 0)
    ki = lax.broadcasted_iota(jnp.int32, (n_blocks_pad, n_keys), 1) + key0
    return (bi == jnp.right_shift(ki, int(math.log2(block)))).astype(BF16)


def _gelu_tanh(x):
    return x * (0.5 * (1.0 + jnp.tanh(math.sqrt(2.0 / math.pi) * (x + 0.044715 * (x * x * x)))))


def _cmp_mlp(chunk_rows, n_ch, w1_ref, cpe_ref, cb1_ref, w2_ref, cb2_ref, kg_ref):
    hid_w = NSA_CMP_HIDDEN
    out = {}
    for e in range(2):
        accp = _dot(cpe_ref[e], w1_ref[e])
        pe = accp[0:1, :hid_w] + accp[1:2, hid_w:] + cb1_ref[e]
        x = jnp.concatenate([chunk_rows(e * NSA_KV + g) for g in range(NSA_KV)], axis=0).astype(BF16)
        acc = _dot(x, w1_ref[e])
        hid = []
        for g in range(NSA_KV):
            a = acc[g * n_ch:(g + 1) * n_ch]
            hid.append(_gelu_tanh(a[:, :hid_w] + pltpu.roll(a[:, hid_w:], n_ch - 1, 0) + pe))
        ckv = _dot(jnp.concatenate(hid, axis=0).astype(BF16), w2_ref[e]) + cb2_ref[e]
        for g in range(NSA_KV):
            out[e, g] = ckv[g * n_ch:(g + 1) * n_ch]
    kc = [_rms(out[0, g], kg_ref[...]) for g in range(NSA_KV)]
    vc = [out[1, g] for g in range(NSA_KV)]
    return kc, vc


def _nsa_q4(q, qg):
    return jnp.concatenate([_rms(q[:, r * HEAD_DIM:(r + 1) * HEAD_DIM], qg) * (HEAD_DIM ** -0.5 * LOG2E)
                            for r in range(NSA_REP)], axis=0)


def _nsa_cmp_branch(q4, kc, vc, qpos, n_cmp):
    nq = qpos.shape[0]
    n_ch = kc.shape[0]
    sc = _dot_nt(q4, kc.astype(BF16)).reshape(NSA_REP, nq, n_ch)
    cidx = lax.broadcasted_iota(jnp.int32, (1, n_ch), 1)
    cmask = (NSA_CMP_STRIDE * cidx + NSA_CMP_LEN - 1 <= qpos) & (cidx < n_cmp)
    p = _masked_softmax_rows(sc, cmask)
    o = _dot(p.reshape(NSA_REP * nq, n_ch).astype(BF16), vc.astype(BF16))
    return o, jnp.sum(p, axis=0)


def _nsa_select(pcs, qpos0, n_sb):
    n_ch = pcs.shape[1]
    ratio = NSA_SEL_BLOCK // NSA_CMP_STRIDE
    bi = lax.broadcasted_iota(jnp.int32, (LANES, n_ch), 0)
    ci = lax.broadcasted_iota(jnp.int32, (LANES, n_ch), 1)
    mimp = ((ci >= ratio * bi - 1) & (ci <= ratio * bi + ratio - 1)).astype(BF16)
    hi = pcs.astype(BF16)
    r1 = pcs - hi.astype(F32)
    mid = r1.astype(BF16)
    lo = (r1 - mid.astype(F32)).astype(BF16)
    imp = _dot_nt(mimp, hi) + _dot_nt(mimp, mid) + _dot_nt(mimp, lo)
    nbp = -(-n_sb // SUBLANES) * SUBLANES
    blk = lax.broadcasted_iota(jnp.int32, (nbp, 1), 0)
    qpos = qpos0 + lax.broadcasted_iota(jnp.int32, (1, LANES), 1)
    cur = jnp.right_shift(qpos, int(math.log2(NSA_SEL_BLOCK)))
    valid = (blk <= cur) & (blk < n_sb)
    forced = (blk == 0) | (blk == cur) | (blk == cur - 1)
    score = jnp.where(forced, -NEG, jnp.where(valid, imp[:nbp], NEG))
    rank = jnp.zeros((nbp, LANES), F32)
    for k in range(n_sb):
        sk = score[k:k + 1, :]
        rank = rank + jnp.where((sk > score) | ((sk == score) & (blk > k)), 1.0, 0.0)
    sel_t = jnp.where((rank < min(NSA_N_SEL, n_sb)) & (score > 0.5 * NEG), 1.0, 0.0)
    sel_t = jnp.concatenate([sel_t, jnp.zeros((LANES - nbp, LANES), F32)], axis=0)
    return sel_t.T


def _nsa_window_mask(kpos, qpos):
    return (kpos <= qpos) & (qpos - kpos < NSA_WINDOW) & (kpos >= 0)


def _nsa_gate_mix(gsig, o_cmp, o_sel, o_win, nq):
    outs = []
    for r in range(NSA_REP):
        c0 = MISC_GATE_COL + 3 * r
        rows = slice(r * nq, (r + 1) * nq)
        outs.append(gsig[:, c0:c0 + 1] * o_cmp[rows] + gsig[:, c0 + 1:c0 + 2] * o_sel[rows]
                    + gsig[:, c0 + 2:c0 + 3] * o_win[rows])
    return jnp.concatenate(outs, axis=1)


def _nsa_cmp_kernel(*refs, n_ch):
    x_refs = refs[:NSA_CMP_STRIDE]
    w1_ref, cpe_ref, cb1_ref, w2_ref, cb2_ref, kg_ref, kc_ref, vc_ref = refs[NSA_CMP_STRIDE:]

    def chunk_rows(eg):
        return jnp.concatenate([x_refs[s][:, eg * HEAD_DIM:(eg + 1) * HEAD_DIM] for s in range(NSA_CMP_STRIDE)], axis=1)

    kc, vc = _cmp_mlp(chunk_rows, n_ch, w1_ref, cpe_ref, cb1_ref, w2_ref, cb2_ref, kg_ref)
    for g in range(NSA_KV):
        kc_ref[0, g] = kc[g]
        vc_ref[0, g] = vc[g]


def _cmp_weight_specs():
    z = (0,) * 8
    full = lambda shape: pl.BlockSpec(shape, lambda *a: z[:len(shape)])
    taps = NSA_CMP_STRIDE * HEAD_DIM
    return [full((2, taps, 2 * NSA_CMP_HIDDEN)), full((2, SUBLANES, taps)),
            full((2, 1, NSA_CMP_HIDDEN)), full((2, NSA_CMP_HIDDEN, HEAD_DIM)), full((2, 1, HEAD_DIM)),
            full((1, HEAD_DIM))]


def nsa_cmp_prompt(z, cmpw, *, batch, seq):
    n_ch = seq // NSA_CMP_STRIDE
    seg = 4 * HEAD_DIM
    st = NSA_CMP_STRIDE
    zc = z[:batch * seq, AB_CMP:AB_CMP + seg].reshape(batch * n_ch, st * seg)
    shp = jax.ShapeDtypeStruct((batch, NSA_KV, n_ch, HEAD_DIM), F32)
    ospec = pl.BlockSpec((1, NSA_KV, n_ch, HEAD_DIM), lambda b: (b, 0, 0, 0))
    xspec = lambda s: pl.BlockSpec((n_ch, seg), lambda b: (b, s))
    return pl.pallas_call(
        functools.partial(_nsa_cmp_kernel, n_ch=n_ch),
        grid=(batch,),
        in_specs=[xspec(s) for s in range(st)] + _cmp_weight_specs(),
        out_specs=[ospec, ospec],
        out_shape=[shp, shp],
        compiler_params=pltpu.CompilerParams(dimension_semantics=("parallel",)),
        name="nsa_cmp_prompt",
    )(*([zc] * st), *cmpw)


def _nsa_prep_kernel(sel_ref, win_ref, kg_ref, o_ref):
    d = HEAD_DIM
    parts = []
    for ref, row in ((sel_ref, 1), (win_ref, 2)):
        x = ref[...]
        for g in range(NSA_KV):
            parts.append(_rms(x[:, g * d:(g + 1) * d], kg_ref[row:row + 1, :]))
        parts.append(x[:, NSA_KV * d:])
    o_ref[...] = jnp.concatenate(parts, axis=1).astype(BF16)


def nsa_prep_prompt(z, k_gain, *, rows, tm):
    seg = 4 * HEAD_DIM
    return pl.pallas_call(
        _nsa_prep_kernel,
        grid=(rows // tm,),
        in_specs=[pl.BlockSpec((tm, seg), lambda i: (i, AB_SEL // seg)),
                  pl.BlockSpec((tm, seg), lambda i: (i, AB_WIN // seg)),
                  pl.BlockSpec((3, HEAD_DIM), lambda i: (0, 0))],
        out_specs=pl.BlockSpec((tm, 2 * seg), lambda i: (i, 0)),
        out_shape=jax.ShapeDtypeStruct((rows, 2 * seg), BF16),
        compiler_params=pltpu.CompilerParams(dimension_semantics=("parallel",)),
        name="nsa_prep_prompt",
    )(z, z, k_gain)


def _nsa_attn_kernel(q_ref, m_ref, kc_ref, vc_ref, ks_ref, vs_ref, kw_ref, vw_ref, oh_ref, qg_ref, o_ref, *, seq, kt):
    i = pl.program_id(2)
    nq = q_ref.shape[0]
    assert nq == LANES
    q4 = _nsa_q4(q_ref[...], qg_ref[...]).astype(BF16)
    qpos = i * nq + lax.broadcasted_iota(jnp.int32, (nq, 1), 0)
    n_ch = seq // NSA_CMP_STRIDE
    n_sb = -(-seq // NSA_SEL_BLOCK)

    o_cmp, pcs = _nsa_cmp_branch(q4, kc_ref[0, 0], vc_ref[0, 0], qpos, n_ch - 1)
    sel = _nsa_select(pcs, i * nq, n_sb)
    bias = jnp.where(sel > 0.5, 0.0, MASK_BIAS).astype(BF16)
    q_aug = jnp.concatenate([q4, jnp.concatenate([bias] * NSA_REP, axis=0)], axis=1)

    def sel_step(j, carry, causal):
        m, l, acc = carry
        rows = pl.ds(pl.multiple_of(j * kt, kt), kt)
        k_aug = jnp.concatenate([ks_ref[rows, :], oh_ref[rows, :]], axis=1)
        s = _dot_nt(q_aug, k_aug).reshape(NSA_REP, nq, kt)
        if causal:
            kpos = j * kt + lax.broadcasted_iota(jnp.int32, (1, kt), 1)
            s = jnp.where((kpos <= qpos)[None], s, NEG)
        mn = jnp.maximum(m, jnp.max(s, axis=-1, keepdims=True))
        p = jnp.exp2(s - mn)
        alpha = jnp.exp2(m - mn)
        l = alpha * l + jnp.sum(p, axis=-1, keepdims=True)
        pv = _dot(p.reshape(NSA_REP * nq, kt).astype(BF16), vs_ref[rows, :])
        return mn, l, alpha * acc + pv.reshape(NSA_REP, nq, HEAD_DIM)

    last = (i * nq + nq - 1) // kt
    init = (jnp.full((NSA_REP, nq, 1), NEG, F32), jnp.zeros((NSA_REP, nq, 1), F32),
            jnp.zeros((NSA_REP, nq, HEAD_DIM), F32))
    carry = lax.fori_loop(0, last, functools.partial(sel_step, causal=False), init)
    _, l, acc = sel_step(last, carry, True)
    o_sel = (acc / l).reshape(NSA_REP * nq, HEAD_DIM)

    span = min(NSA_WINDOW + nq, seq)
    start = pl.multiple_of(jnp.clip(i * nq - NSA_WINDOW, 0, seq - span), nq)
    s = _dot_nt(q4, kw_ref[pl.ds(start, span), :]).reshape(NSA_REP, nq, span)
    kpos = start + lax.broadcasted_iota(jnp.int32, (1, span), 1)
    p = _masked_softmax_rows(s, _nsa_window_mask(kpos, qpos))
    o_win = _dot(p.reshape(NSA_REP * nq, span).astype(BF16), vw_ref[pl.ds(start, span), :])

    o_ref[...] = _nsa_gate_mix(jax.nn.sigmoid(m_ref[...]), o_cmp, o_sel, o_win, nq).astype(o_ref.dtype)


def nsa_attn_prompt(z, kvp, kc, vc, q_gain, *, batch, seq, tq):
    nt = seq // tq
    gw = NSA_REP * HEAD_DIM
    kt = min(4 * tq, seq)
    row = lambda b, g, i: b * nt + i
    cspec = pl.BlockSpec((1, 1, seq // NSA_CMP_STRIDE, HEAD_DIM), lambda b, g, i: (b, g, 0, 0))
    kvspec = lambda c: pl.BlockSpec((seq, HEAD_DIM), lambda b, g, i: (b, c + g))
    return pl.pallas_call(
        functools.partial(_nsa_attn_kernel, seq=seq, kt=kt),
        grid=(batch, NSA_KV, nt),
        in_specs=[pl.BlockSpec((tq, gw), lambda b, g, i: (row(b, g, i), AB_NQ // gw + g)),
                  pl.BlockSpec((tq, LANES), lambda b, g, i: (row(b, g, i), AB_MISC0 // LANES + g)),
                  cspec, cspec, kvspec(0), kvspec(2), kvspec(4), kvspec(6),
                  pl.BlockSpec((seq, LANES), lambda b, g, i: (0, 0)),
                  pl.BlockSpec((1, HEAD_DIM), lambda b, g, i: (0, 0))],
        out_specs=pl.BlockSpec((tq, gw), lambda b, g, i: (row(b, g, i), g)),
        out_shape=jax.ShapeDtypeStruct((batch * seq, NSA_KV * gw), BF16),
        compiler_params=pltpu.CompilerParams(dimension_semantics=("parallel", "parallel", "arbitrary")),
        name="nsa_attn_prompt",
    )(z, z, kc, vc, kvp, kvp, kvp, kvp, _block_onehot(seq, NSA_SEL_BLOCK), q_gain.reshape(1, HEAD_DIM))


def _block_onehot(n_keys, block):
    k = lax.broadcasted_iota(jnp.int32, (n_keys, LANES), 0) // block
    return (k == lax.broadcasted_iota(jnp.int32, (n_keys, LANES), 1)).astype(BF16)


def nsa_cmp_weights(cw1, cb1, cw2, cb2, cpe, k_gain):
    st = NSA_CMP_STRIDE
    taps = st * HEAD_DIM
    w1cat = jnp.concatenate([cw1[:, :st], cw1[:, st:]], axis=-1).reshape(2, taps, 2 * NSA_CMP_HIDDEN).astype(BF16)
    pe_rows = jnp.stack([cpe[:, :st].reshape(2, taps), cpe[:, st:].reshape(2, taps)], axis=1)
    pe_rows = jnp.pad(pe_rows, ((0, 0), (0, SUBLANES - 2), (0, 0))).astype(BF16)
    return (w1cat, pe_rows, cb1.reshape(2, 1, NSA_CMP_HIDDEN), cw2.astype(BF16), cb2.reshape(2, 1, HEAD_DIM),
            k_gain[0].reshape(1, HEAD_DIM))


NSA_ROW = 4 * NSA_KV * HEAD_DIM
CHUNKS_PER_PAGE = PAGE_SIZE // NSA_CMP_STRIDE


def _nsa_sample_kernel(pt_ref, *refs, n_pages, past, t_new):
    del pt_ref
    pages = refs[:n_pages]
    (q_ref, sel_ref, win_ref, m0_ref, m1_ref, wb_ref, w1_ref, cpe_ref, cb1_ref, w2_ref, cb2_ref, kg0_ref,
     kg_ref, qg_ref, o_ref) = refs[n_pages:]
    nq = SAMPLE_ROWS
    d = HEAD_DIM
    n_ch = n_pages * CHUNKS_PER_PAGE
    n_cmp = (past + t_new) // NSA_CMP_STRIDE - 1
    n_sb = -(-(past + t_new) // NSA_SEL_BLOCK)
    n_sl = NSA_ROW // d
    n_wsl = 2 * NSA_KV

    def chunk_rows(j):
        return jnp.concatenate(
            [jnp.concatenate([pg[0, pl.ds(s * n_sl + j, CHUNKS_PER_PAGE, stride=NSA_CMP_STRIDE * n_sl), :]
                              for s in range(NSA_CMP_STRIDE)], axis=1) for pg in pages], axis=0)

    def past_rows(j):
        return jnp.concatenate([pg[0, pl.ds(j, PAGE_SIZE, stride=n_sl), :] for pg in pages], axis=0)

    kc, vc = _cmp_mlp(chunk_rows, n_ch, w1_ref, cpe_ref, cb1_ref, w2_ref, cb2_ref, kg0_ref)

    qpos = past + lax.broadcasted_iota(jnp.int32, (nq, 1), 0)
    nk = past + LANES
    kpos = lax.broadcasted_iota(jnp.int32, (1, nk), 1)
    esel = _block_expand(LANES, nk, NSA_SEL_BLOCK)
    causal = kpos <= qpos
    nw = NSA_WINDOW + LANES
    wmask = _nsa_window_mask(past - NSA_WINDOW + lax.broadcasted_iota(jnp.int32, (1, nw), 1), qpos)
    zpad = jnp.zeros((LANES - nq, d), F32)
    q = q_ref[0]
    sel_new = sel_ref[0]
    win_new = win_ref[0]
    kg_sel = kg_ref[1:2, :]
    kg_win = kg_ref[2:3, :]
    col = lambda x, c: x[:, c * d:(c + 1) * d]
    wb_rows = lambda j: wb_ref[0, pl.ds(j, NSA_WINDOW, stride=n_wsl), :]

    for g in range(NSA_KV):
        q4 = _nsa_q4(q[:, g * NSA_REP * d:(g + 1) * NSA_REP * d], qg_ref[...]).astype(BF16)
        o_cmp, pcs = _nsa_cmp_branch(q4, kc[g], vc[g], qpos, n_cmp)
        pcs = jnp.concatenate([pcs, jnp.zeros((LANES - nq, n_ch), F32)], axis=0)
        selb = _nsa_select(pcs, past, n_sb)[:nq].astype(BF16)

        ks = jnp.concatenate([_rms(past_rows(2 * NSA_KV + g), kg_sel), _rms(col(sel_new, g), kg_sel), zpad], axis=0)
        vs = jnp.concatenate([past_rows(3 * NSA_KV + g), col(sel_new, NSA_KV + g), zpad], axis=0)
        s = _dot_nt(q4, ks.astype(BF16)).reshape(NSA_REP, nq, nk)
        p = _masked_softmax_rows(s, (_dot(selb, esel) > 0.5) & causal)
        o_sel = _dot(p.reshape(NSA_REP * nq, nk).astype(BF16), vs.astype(BF16))

        kw = jnp.concatenate([_rms(wb_rows(g), kg_win), _rms(col(win_new, g), kg_win), zpad], axis=0)
        vw = jnp.concatenate([wb_rows(NSA_KV + g), col(win_new, NSA_KV + g), zpad], axis=0)
        s = _dot_nt(q4, kw.astype(BF16)).reshape(NSA_REP, nq, nw)
        p = _masked_softmax_rows(s, wmask)
        o_win = _dot(p.reshape(NSA_REP * nq, nw).astype(BF16), vw.astype(BF16))

        gsig = jax.nn.sigmoid((m0_ref, m1_ref)[g][0])
        o_ref[0, :, g * NSA_REP * d:(g + 1) * NSA_REP * d] = _nsa_gate_mix(gsig, o_cmp, o_sel, o_win, nq).astype(o_ref.dtype)


def nsa_sample(zs, page_table, cache_kv, cache_win, cmpw, k_gain, q_gain, *, t_new):
    db, n_pages = page_table.shape
    past = n_pages * PAGE_SIZE
    assert cache_win.shape[1] == NSA_WINDOW and (past + t_new) // NSA_CMP_STRIDE == n_pages * CHUNKS_PER_PAGE
    n_phys = cache_kv.shape[0]
    page_rows = PAGE_SIZE * NSA_ROW // HEAD_DIM
    win_rows = NSA_WINDOW * 2 * NSA_KV
    pages = cache_kv.reshape(n_phys, page_rows, HEAD_DIM)
    wb = cache_win.reshape(db, win_rows, HEAD_DIM)
    seg = 4 * HEAD_DIM
    qw = NSA_HEADS * HEAD_DIM
    zspec = lambda w, off: pl.BlockSpec((1, SAMPLE_ROWS, w), lambda b, pt: (b, 0, off // w))
    page_spec = lambda p: pl.BlockSpec((1, page_rows, HEAD_DIM), lambda b, pt: (pt[b, p], 0, 0))
    grid_spec = pltpu.PrefetchScalarGridSpec(
        num_scalar_prefetch=1,
        grid=(db,),
        in_specs=[page_spec(p) for p in range(n_pages)]
        + [zspec(qw, AB_NQ), zspec(seg, AB_SEL), zspec(seg, AB_WIN), zspec(LANES, AB_MISC0), zspec(LANES, AB_MISC1),
           pl.BlockSpec((1, win_rows, HEAD_DIM), lambda b, pt: (b, 0, 0))]
        + _cmp_weight_specs()
        + [pl.BlockSpec((3, HEAD_DIM), lambda b, pt: (0, 0)), pl.BlockSpec((1, HEAD_DIM), lambda b, pt: (0, 0))],
        out_specs=pl.BlockSpec((1, SAMPLE_ROWS, qw), lambda b, pt: (b, 0, 0)),
    )
    return pl.pallas_call(
        functools.partial(_nsa_sample_kernel, n_pages=n_pages, past=past, t_new=t_new),
        grid_spec=grid_spec,
        out_shape=jax.ShapeDtypeStruct((db, SAMPLE_ROWS, qw), BF16),
        compiler_params=pltpu.CompilerParams(dimension_semantics=("parallel",)),
        name="nsa_sample",
    )(page_table, *([pages] * n_pages), zs, zs, zs, zs, zs, wb, *cmpw, k_gain, q_gain.reshape(1, HEAD_DIM))


C_Q, C_K, C_V, C_TOTAL = 0, MOBA_HEADS * HEAD_DIM, (MOBA_HEADS + MOBA_KV) * HEAD_DIM, (MOBA_HEADS + 2 * MOBA_KV) * HEAD_DIM
MOBA_SUBTILE = 1024


def _masked_softmax2d(s, mask):
    sm = jnp.where(mask, s, NEG)
    m = jnp.max(sm, axis=-1, keepdims=True)
    e = jnp.where(mask, jnp.exp2(sm - m), 0.0)
    dsum = jnp.sum(e, axis=-1, keepdims=True)
    return e / jnp.where(dsum > 0, dsum, 1.0)


def _moba_q4(q, qg):
    return jnp.concatenate([_rms(q[:, r * HEAD_DIM:(r + 1) * HEAD_DIM], qg) * (HEAD_DIM ** -0.5 * LOG2E)
                            for r in range(MOBA_REP)], axis=0)


def _moba_select(q4, kmean_pad, cur, n_blocks):
    gs = _dot_nt(q4, kmean_pad.astype(BF16))
    blk = lax.broadcasted_iota(jnp.int32, (1, LANES), 1)
    score = jnp.where((blk < cur) & (blk < n_blocks), gs, NEG)
    rank = jnp.zeros(score.shape, F32)
    for k in range(n_blocks):
        sk = score[:, k:k + 1]
        rank = rank + jnp.where((sk > score) | ((sk == score) & (blk > k)), 1.0, 0.0)
    sel = jnp.where((rank < min(MOBA_TOPK, n_blocks)) & (score > 0.5 * NEG), 1.0, 0.0)
    return jnp.where(blk == cur, 1.0, sel)


def _moba_prep_kernel(k_ref, v_ref, kg_ref, o_ref, km_ref):
    k = k_ref[...]
    kn = jnp.concatenate([_rms(k[:, h * HEAD_DIM:(h + 1) * HEAD_DIM], kg_ref[...]) for h in range(MOBA_KV)], axis=1)
    km_ref[0] = jnp.mean(kn, axis=0, keepdims=True)
    o_ref[...] = jnp.concatenate([kn, v_ref[...]], axis=1).astype(BF16)


def moba_prep_prompt(zc, k_gain, *, rows):
    w = MOBA_KV * HEAD_DIM
    nblk = rows // MOBA_BLOCK
    return pl.pallas_call(
        _moba_prep_kernel,
        grid=(nblk,),
        in_specs=[pl.BlockSpec((MOBA_BLOCK, w), lambda i: (i, C_K // w)),
                  pl.BlockSpec((MOBA_BLOCK, w), lambda i: (i, C_V // w)),
                  pl.BlockSpec((1, HEAD_DIM), lambda i: (0, 0))],
        out_specs=[pl.BlockSpec((MOBA_BLOCK, 2 * w), lambda i: (i, 0)),
                   pl.BlockSpec((1, 1, w), lambda i: (i, 0, 0))],
        out_shape=[jax.ShapeDtypeStruct((rows, 2 * w), BF16), jax.ShapeDtypeStruct((nblk, 1, w), F32)],
        compiler_params=pltpu.CompilerParams(dimension_semantics=("parallel",)),
        name="moba_prep_prompt",
    )(zc, zc, k_gain.reshape(1, HEAD_DIM))


def _moba_attn_kernel(q_ref, km_ref, k_ref, v_ref, oh_ref, qg_ref, o_ref, *, seq):
    i = pl.program_id(2)
    nq = q_ref.shape[0]
    assert nq == MOBA_BLOCK
    nb = seq // MOBA_BLOCK
    rq = MOBA_REP * nq
    q4 = _moba_q4(q_ref[...], qg_ref[...]).astype(BF16)

    km = jnp.concatenate([km_ref[0], jnp.zeros((LANES - nb, HEAD_DIM), F32)], axis=0)
    nbp = -(-nb // SUBLANES) * SUBLANES
    gs = _dot_nt(km.astype(BF16), q4)[:nbp]
    blk = lax.broadcasted_iota(jnp.int32, (nbp, 1), 0)
    score = jnp.where((blk < i) & (blk < nb), gs, NEG)
    rank = jnp.zeros((nbp, rq), F32)
    for k in range(nb):
        sk = score[k:k + 1, :]
        rank = rank + jnp.where((sk > score) | ((sk == score) & (blk > k)), 1.0, 0.0)
    chosen = ((rank < min(MOBA_TOPK, nb)) & (score > 0.5 * NEG)) | (blk == i)
    bias_t = jnp.concatenate([jnp.where(chosen, 0.0, MASK_BIAS), jnp.zeros((LANES - nbp, rq), F32)], axis=0)
    q_aug = jnp.concatenate([q4, bias_t.T.astype(BF16)], axis=1)

    def update(carry, s, rows):
        m, l, acc = carry
        mn = jnp.maximum(m, jnp.max(s, axis=-1, keepdims=True))
        p = jnp.exp2(s - mn)
        alpha = jnp.exp2(m - mn)
        l = alpha * l + jnp.sum(p, axis=-1, keepdims=True)
        return mn, l, alpha * acc + _dot(p.astype(BF16), v_ref[rows, :])

    own = pl.ds(pl.multiple_of(i * MOBA_BLOCK, MOBA_BLOCK), MOBA_BLOCK)
    kcol = lax.broadcasted_iota(jnp.int32, (1, MOBA_BLOCK), 1)
    rs = min(MOBA_SUBTILE, rq)
    outs = []
    for t in range(rq // rs):
        qa = q_aug[t * rs:(t + 1) * rs]

        def past_step(n, carry, qa=qa):
            rows = pl.ds(pl.multiple_of(n * MOBA_BLOCK, MOBA_BLOCK), MOBA_BLOCK)
            k_aug = jnp.concatenate([k_ref[rows, :], oh_ref[rows, :]], axis=1)
            return update(carry, _dot_nt(qa, k_aug), rows)

        init = (jnp.full((rs, 1), NEG, F32), jnp.zeros((rs, 1), F32), jnp.zeros((rs, HEAD_DIM), F32))
        carry = lax.fori_loop(0, i, past_step, init)
        tq = (t * rs + lax.broadcasted_iota(jnp.int32, (rs, 1), 0)) % nq
        s_own = jnp.where(kcol <= tq, _dot_nt(qa[:, :HEAD_DIM], k_ref[own, :]), NEG)
        _, l, acc = update(carry, s_own, own)
        outs.append(acc / l)
    o = jnp.concatenate(outs, axis=0)
    o_ref[...] = jnp.concatenate([o[r * nq:(r + 1) * nq] for r in range(MOBA_REP)], axis=1).astype(o_ref.dtype)


def moba_attn_prompt(zc, kvp, kmean, q_gain, *, batch, seq):
    nt = seq // MOBA_BLOCK
    gw = MOBA_REP * HEAD_DIM
    km = kmean.reshape(batch, nt, MOBA_KV * HEAD_DIM)
    return pl.pallas_call(
        functools.partial(_moba_attn_kernel, seq=seq),
        grid=(batch, MOBA_KV, nt),
        in_specs=[pl.BlockSpec((MOBA_BLOCK, gw), lambda b, g, i: (b * nt + i, g)),
                  pl.BlockSpec((1, nt, HEAD_DIM), lambda b, g, i: (b, 0, g)),
                  pl.BlockSpec((seq, HEAD_DIM), lambda b, g, i: (b, g)),
                  pl.BlockSpec((seq, HEAD_DIM), lambda b, g, i: (b, MOBA_KV + g)),
                  pl.BlockSpec((seq, LANES), lambda b, g, i: (0, 0)),
                  pl.BlockSpec((1, HEAD_DIM), lambda b, g, i: (0, 0))],
        out_specs=pl.BlockSpec((MOBA_BLOCK, gw), lambda b, g, i: (b * nt + i, g)),
        out_shape=jax.ShapeDtypeStruct((batch * seq, MOBA_HEADS * HEAD_DIM), BF16),
        compiler_params=pltpu.CompilerParams(dimension_semantics=("parallel", "parallel", "arbitrary")),
        name="moba_attn_prompt",
    )(zc, km, kvp, kvp, _block_onehot(seq, MOBA_BLOCK), q_gain.reshape(1, HEAD_DIM))


def _moba_sample_kernel(pt_ref, *refs, n_pages, past, t_new):
    del pt_ref
    pages = refs[:n_pages]
    q_ref, kn_ref, vn_ref, kg_ref, qg_ref, o_ref = refs[n_pages:]
    nq = SAMPLE_ROWS
    d = HEAD_DIM
    rq = MOBA_REP * nq
    nb = -(-(past + t_new) // MOBA_BLOCK)
    n_past_blocks = past // MOBA_BLOCK
    nk = past + LANES
    qpos = past + lax.broadcasted_iota(jnp.int32, (nq, 1), 0)
    qpos4 = jnp.concatenate([qpos] * MOBA_REP, axis=0)
    cur = jnp.right_shift(qpos4, int(math.log2(MOBA_BLOCK)))
    kpos = lax.broadcasted_iota(jnp.int32, (1, nk), 1)
    expand = _block_expand(LANES, nk, MOBA_BLOCK)
    causal = kpos <= qpos4
    zpad = jnp.zeros((LANES - nq, d), F32)
    q = q_ref[0]
    k_new = kn_ref[0]
    v_new = vn_ref[0]
    n_sl = 2 * MOBA_KV
    past_rows = lambda j: jnp.concatenate([pg[0, pl.ds(j, PAGE_SIZE, stride=n_sl), :] for pg in pages], axis=0)
    for g in range(MOBA_KV):
        kn = _rms(past_rows(g), kg_ref[...])
        km = jnp.concatenate([jnp.mean(kn[n * MOBA_BLOCK:(n + 1) * MOBA_BLOCK], axis=0, keepdims=True)
                              for n in range(n_past_blocks)] + [jnp.zeros((LANES - n_past_blocks, d), F32)], axis=0)
        q4 = _moba_q4(q[:, g * MOBA_REP * d:(g + 1) * MOBA_REP * d], qg_ref[...]).astype(BF16)
        sel = _moba_select(q4, km, cur, nb)
        keys = jnp.concatenate([kn, _rms(k_new[:, g * d:(g + 1) * d], kg_ref[...]), zpad], axis=0)
        vals = jnp.concatenate([past_rows(MOBA_KV + g), v_new[:, g * d:(g + 1) * d], zpad], axis=0)
        s = _dot_nt(q4, keys.astype(BF16))
        p = _masked_softmax2d(s, (_dot(sel.astype(BF16), expand) > 0.5) & causal)
        o = _dot(p.astype(BF16), vals.astype(BF16))
        for r in range(MOBA_REP):
            h = g * MOBA_REP + r
            o_ref[0, :, h * d:(h + 1) * d] = o[r * nq:(r + 1) * nq].astype(o_ref.dtype)


def moba_sample(zcs, page_table, cache_kv, k_gain, q_gain, *, t_new):
    db, n_pages = page_table.shape
    past = n_pages * PAGE_SIZE
    assert past % MOBA_BLOCK == 0 and t_new <= MOBA_BLOCK
    n_phys = cache_kv.shape[0]
    w = MOBA_KV * HEAD_DIM
    page_rows = PAGE_SIZE * 2 * MOBA_KV
    pages = cache_kv.reshape(n_phys, page_rows, HEAD_DIM)
    qw = MOBA_HEADS * HEAD_DIM
    zspec = lambda wd, off: pl.BlockSpec((1, SAMPLE_ROWS, wd), lambda b, pt: (b, 0, off // wd))
    page_spec = lambda p: pl.BlockSpec((1, page_rows, HEAD_DIM), lambda b, pt: (pt[b, p], 0, 0))
    grid_spec = pltpu.PrefetchScalarGridSpec(
        num_scalar_prefetch=1,
        grid=(db,),
        in_specs=[page_spec(p) for p in range(n_pages)]
        + [zspec(qw, C_Q), zspec(w, C_K), zspec(w, C_V),
           pl.BlockSpec((1, HEAD_DIM), lambda b, pt: (0, 0)), pl.BlockSpec((1, HEAD_DIM), lambda b, pt: (0, 0))],
        out_specs=pl.BlockSpec((1, SAMPLE_ROWS, qw), lambda b, pt: (b, 0, 0)),
    )
    return pl.pallas_call(
        functools.partial(_moba_sample_kernel, n_pages=n_pages, past=past, t_new=t_new),
        grid_spec=grid_spec,
        out_shape=jax.ShapeDtypeStruct((db, SAMPLE_ROWS, qw), BF16),
        compiler_params=pltpu.CompilerParams(dimension_semantics=("parallel",)),
        name="moba_sample",
    )(page_table, *([pages] * n_pages), zcs, zcs, zcs, k_gain.reshape(1, HEAD_DIM), q_gain.reshape(1, HEAD_DIM))


ROW_TILE = 512
GLA_TILE = 128
NSA_Q_TILE = 128


def _ab_weight_layout(w_in_ab):
    d = w_in_ab.shape[0]
    widths = (GLA_HEADS * GLA_DK, GLA_HEADS * GLA_DK, GLA_HEADS * GLA_DV, GLA_HEADS * GLA_DV, GLA_GATE_RANK,
              NSA_HEADS * HEAD_DIM, NSA_HEADS * 3, 6 * NSA_KV * HEAD_DIM)
    offs = [0]
    for w in widths:
        offs.append(offs[-1] + w)
    gq, gk, gv, gr, ga, nq, ngt, nkv = (w_in_ab[:, offs[i]:offs[i + 1]] for i in range(len(widths)))
    half = NSA_REP * 3
    zeros = lambda n: jnp.zeros((d, n), w_in_ab.dtype)
    misc0 = jnp.concatenate([ga, ngt[:, :half], zeros(LANES - GLA_GATE_RANK - half)], axis=1)
    misc1 = jnp.concatenate([zeros(GLA_GATE_RANK), ngt[:, half:], zeros(LANES - GLA_GATE_RANK - half)], axis=1)
    w = jnp.concatenate([gq, gk, gv, gr, nq, nkv, misc0, misc1, zeros(AB_TOTAL - AB_MISC1 - LANES)], axis=1)
    return w.astype(BF16)


def _pad_sample_rows(z, db, t_new):
    return jnp.pad(z.reshape(db, t_new, z.shape[-1]), ((0, 0), (0, SAMPLE_ROWS - t_new), (0, 0)))


def _matmul_rows(n):
    return n // 8 if n % (8 * 2 * SUBLANES) == 0 else ROW_TILE


def _conv_ffn(h, l, n_p, batch, seq, t_new, state_ffn_conv, norm_ffn, w_up_bf16, ffn_conv_w, ffn_conv_b, w_down_bf16):
    d_ff = ffn_conv_w.shape[-1]
    db = state_ffn_conv.shape[1]
    act, tails, gate_s = ffn_up_act(h, norm_ffn[l], w_up_bf16, ffn_conv_w[l], ffn_conv_b[l], state_ffn_conv[l],
                                    batch=batch, seq=seq, t_new=t_new, tm=ROW_TILE, tn=d_ff // 4, layer=l)
    h = matmul_residual(act, w_down_bf16, h, tm=_matmul_rows(h.shape[0]), tn=min(1024, h.shape[1]), tk=d_ff // 2, layer=l)
    keep = FFN_CONV - 1
    tps = seq // ROW_TILE
    gate_p = jnp.stack([tails[(b + 1) * tps * SUBLANES - keep:(b + 1) * tps * SUBLANES] for b in range(batch)])
    gate_s = gate_s.reshape(db, t_new, d_ff)[:, t_new - keep:]
    return h, gate_p, gate_s


def kernel(x_prompt, x_sample, page_table, cache_nsa_kv, cache_nsa_win, state_gla, cache_moba_kv, state_ffn_conv, norm_mix, w_in_ab, gla_a_w2, gla_a_b, gla_o_norm, nsa_q_norm, nsa_k_norm, nsa_cmp_w1, nsa_cmp_b1, nsa_cmp_w2, nsa_cmp_b2, nsa_cmp_pe, w_out_ab, w_in_c, moba_q_norm, moba_k_norm, w_out_c, norm_ffn, ffn_w_up, ffn_conv_w, ffn_conv_b, ffn_w_down):
    batch, seq, d_model = x_prompt.shape
    db, t_new, _ = x_sample.shape
    n_p, n_s = batch * seq, db * t_new
    n = n_p + n_s
    assert norm_mix.shape[0] == 2 and w_in_ab.shape[0] == 1 and w_in_c.shape[0] == 1
    assert FFN_CONV - 1 <= t_new <= SAMPLE_ROWS and n % ROW_TILE == 0 and n_p % ROW_TILE == 0
    ffn_args = (state_ffn_conv, norm_ffn, ffn_w_up.astype(BF16), ffn_conv_w, ffn_conv_b, ffn_w_down.astype(BF16))
    tm_mm = _matmul_rows(n)

    h = jnp.concatenate([x_prompt.reshape(n_p, d_model), x_sample.reshape(n_s, d_model)], axis=0)
    nsa_cache = cache_nsa_kv.reshape(cache_nsa_kv.shape[1:])
    nsa_win = cache_nsa_win.reshape(cache_nsa_win.shape[1:])
    gla_state = state_gla.reshape(state_gla.shape[1:])
    moba_cache = cache_moba_kv.reshape(cache_moba_kv.shape[1:])
    tn_out = min(1024, d_model)

    z = norm_matmul(h, norm_mix[0], _ab_weight_layout(w_in_ab[0]), tm=tm_mm, tn=1024)
    zs = _pad_sample_rows(z[n_p:], db, t_new)
    w2p = jnp.pad(gla_a_w2[0], ((0, LANES - GLA_GATE_RANK), (0, 0))).astype(BF16)
    gla_zero = jnp.zeros((batch,) + state_gla.shape[2:], F32)
    zg = z if n % GLA_TILE == 0 else z[:n_p]
    og_p, gla_p = gla_mixer(zg.reshape(-1, GLA_TILE, AB_TOTAL), gla_zero, w2p, gla_a_b[0], gla_o_norm[0],
                            n_seq=batch, tiles_per_seq=seq // GLA_TILE, chunk=GLA_CHUNK, n_valid=GLA_CHUNK, out_dtype=BF16)
    og_s, gla_s = gla_mixer(zs, gla_state, w2p, gla_a_b[0], gla_o_norm[0],
                            n_seq=db, tiles_per_seq=1, chunk=SAMPLE_ROWS, n_valid=t_new, out_dtype=BF16)
    cmpw = nsa_cmp_weights(nsa_cmp_w1[0], nsa_cmp_b1[0], nsa_cmp_w2[0], nsa_cmp_b2[0], nsa_cmp_pe[0], nsa_k_norm[0])
    kc, vc = nsa_cmp_prompt(z, cmpw, batch=batch, seq=seq)
    kvp = nsa_prep_prompt(z, nsa_k_norm[0], rows=n_p, tm=ROW_TILE)
    on_p = nsa_attn_prompt(z, kvp, kc, vc, nsa_q_norm[0], batch=batch, seq=seq, tq=NSA_Q_TILE)
    on_s = nsa_sample(zs, page_table, nsa_cache, nsa_win, cmpw, nsa_k_norm[0], nsa_q_norm[0], t_new=t_new)
    mix_p = jnp.concatenate([og_p.reshape(n_p, -1), on_p], axis=1)
    mix_s = jnp.concatenate([og_s[:, :t_new].reshape(n_s, -1), on_s[:, :t_new].reshape(n_s, -1)], axis=1)
    h = matmul_residual(jnp.concatenate([mix_p, mix_s], axis=0), w_out_ab.astype(BF16), h,
                        tm=tm_mm, tn=tn_out, tk=mix_p.shape[1])
    h, conv_p0, conv_s0 = _conv_ffn(h, 0, n_p, batch, seq, t_new, *ffn_args)

    kv_w = 4 * NSA_KV * HEAD_DIM
    win_w = 2 * NSA_KV * HEAD_DIM
    win_keep = min(NSA_WINDOW, seq)
    nsa_kv_p = z[:n_p, AB_CMP:AB_CMP + kv_w].reshape(1, batch, seq, 4, NSA_KV, HEAD_DIM)
    nsa_kv_s = z[n_p:, AB_CMP:AB_CMP + kv_w].reshape(1, db, t_new, 4, NSA_KV, HEAD_DIM)
    nsa_win_p = z[:n_p, AB_WIN:AB_WIN + win_w].reshape(batch, seq, win_w)[:, seq - win_keep:]
    nsa_win_p = nsa_win_p.reshape(1, batch, win_keep, 2, NSA_KV, HEAD_DIM)
    win_new = z[n_p:, AB_WIN:AB_WIN + win_w].reshape(db, t_new, 2, NSA_KV, HEAD_DIM)
    nsa_win_s = jnp.concatenate([nsa_win, win_new], axis=1)[None, :, -NSA_WINDOW:]

    zc = norm_matmul(h, norm_mix[1], w_in_c[0].astype(BF16), tm=tm_mm, tn=1024)
    zcs = _pad_sample_rows(zc[n_p:], db, t_new)
    kvm, kmean = moba_prep_prompt(zc, moba_k_norm[0], rows=n_p)
    om_p = moba_attn_prompt(zc, kvm, kmean, moba_q_norm[0], batch=batch, seq=seq)
    om_s = moba_sample(zcs, page_table, moba_cache, moba_k_norm[0], moba_q_norm[0], t_new=t_new)
    om = jnp.concatenate([om_p, om_s[:, :t_new].reshape(n_s, -1)], axis=0)
    h = matmul_residual(om, w_out_c.astype(BF16), h, tm=tm_mm, tn=tn_out, tk=om.shape[1])
    h, conv_p1, conv_s1 = _conv_ffn(h, 1, n_p, batch, seq, t_new, *ffn_args)

    moba_kv_p = zc[:n_p, C_K:].reshape(1, batch, seq, 2, MOBA_KV, HEAD_DIM)
    moba_kv_s = zc[n_p:, C_K:].reshape(1, db, t_new, 2, MOBA_KV, HEAD_DIM)

    return (h[:n_p].reshape(batch, seq, d_model), h[n_p:].reshape(db, t_new, d_model),
            nsa_kv_p, nsa_kv_s, nsa_win_p, nsa_win_s, gla_p[None], gla_s[None], moba_kv_p, moba_kv_s,
            jnp.stack([conv_p0, conv_p1]), jnp.stack([conv_s0, conv_s1]))
```

```python
import functools
import math

import jax
import jax.numpy as jnp
from jax import lax
from jax.experimental import pallas as pl
from jax.experimental.pallas import tpu as pltpu

F32 = jnp.float32
BF16 = jnp.bfloat16

HEAD_DIM = 128
GLA_HEADS = 4
GLA_DK = 128
GLA_DV = 256
GLA_GATE_RANK = 16
GLA_TAU = 16.0
GLA_CHUNK = 16
NSA_HEADS = 8
NSA_KV = 2
NSA_REP = NSA_HEADS // NSA_KV
NSA_CMP_STRIDE = 16
NSA_CMP_LEN = 2 * NSA_CMP_STRIDE
NSA_CMP_HIDDEN = 256
NSA_SEL_BLOCK = 64
NSA_N_SEL = 16
NSA_WINDOW = 512
MOBA_HEADS = 16
MOBA_KV = 4
MOBA_REP = MOBA_HEADS // MOBA_KV
MOBA_BLOCK = 256
MOBA_TOPK = 3
FFN_CONV = 3
NORM_EPS = 1e-6
PAGE_SIZE = 128

LANES = 128
SUBLANES = 8
NEG = -1e30
LOG2E = 1.4426950408889634
MASK_BIAS = -(2.0 ** 100)
SAMPLE_ROWS = 8

AB_GQ, AB_GK, AB_GV, AB_GR, AB_NQ = 0, 512, 1024, 2048, 3072
AB_CMP, AB_SEL, AB_WIN, AB_MISC0, AB_MISC1, AB_TOTAL = 4096, 4608, 5120, 5632, 5760, 6144
MISC_GATE_COL = GLA_GATE_RANK


def _rms(x, g):
    return x * lax.rsqrt(jnp.mean(x * x, axis=-1, keepdims=True) + NORM_EPS) * g


def _dot_nt(a, b):
    return lax.dot_general(a, b, (((1,), (1,)), ((), ())), preferred_element_type=F32)


def _dot_tn(a, b):
    return lax.dot_general(a, b, (((0,), (0,)), ((), ())), preferred_element_type=F32)


def _dot(a, b):
    return jnp.dot(a, b, preferred_element_type=F32)


def _silu(x):
    return x * jax.nn.sigmoid(x)


def _norm_matmul_kernel(x_ref, g_ref, w_ref, o_ref, xn_ref):
    @pl.when(pl.program_id(1) == 0)
    def _():
        xn_ref[...] = _rms(x_ref[...], g_ref[...]).astype(BF16)

    o_ref[...] = _dot(xn_ref[...], w_ref[...])


def norm_matmul(x, g, w, *, tm, tn):
    n, k = x.shape
    nout = w.shape[1]
    return pl.pallas_call(
        _norm_matmul_kernel,
        grid=(n // tm, nout // tn),
        in_specs=[pl.BlockSpec((tm, k), lambda i, j: (i, 0)),
                  pl.BlockSpec((1, k), lambda i, j: (0, 0)),
                  pl.BlockSpec((k, tn), lambda i, j: (0, j))],
        out_specs=pl.BlockSpec((tm, tn), lambda i, j: (i, j)),
        out_shape=jax.ShapeDtypeStruct((n, nout), F32),
        scratch_shapes=[pltpu.VMEM((tm, k), BF16)],
        compiler_params=pltpu.CompilerParams(dimension_semantics=("parallel", "arbitrary")),
        name="norm_matmul",
    )(x, g.reshape(1, k), w)


def _matmul_res_kernel(a_ref, w_ref, r_ref, o_ref):
    d = _dot(a_ref[...], w_ref[...])

    @pl.when(pl.program_id(2) == 0)
    def _():
        o_ref[...] = r_ref[...] + d

    @pl.when(pl.program_id(2) > 0)
    def _():
        o_ref[...] += d


def matmul_residual(a, w, res, *, tm, tn, tk, layer=0):
    n, k = a.shape
    nout = w.shape[2]
    return pl.pallas_call(
        _matmul_res_kernel,
        grid=(n // tm, nout // tn, k // tk),
        in_specs=[pl.BlockSpec((tm, tk), lambda i, j, kk: (i, kk)),
                  pl.BlockSpec((None, tk, tn), lambda i, j, kk: (layer, kk, j)),
                  pl.BlockSpec((tm, tn), lambda i, j, kk: (i, j))],
        out_specs=pl.BlockSpec((tm, tn), lambda i, j, kk: (i, j)),
        out_shape=jax.ShapeDtypeStruct((n, nout), F32),
        compiler_params=pltpu.CompilerParams(dimension_semantics=("parallel", "parallel", "arbitrary")),
        name="matmul_residual",
    )(a, w, res)


def _conv_act(g, p1, p2, val, cw_ref, cb_ref):
    c = cb_ref[...] + cw_ref[0:1, :] * p2 + cw_ref[1:2, :] * p1 + cw_ref[2:3, :] * g
    return (_silu(c) * val).astype(BF16)


def _ffn_up_kernel(x_ref, g_ref, wg_ref, wv_ref, cw_ref, cb_ref, s1_ref, s2_ref, act_ref, tail_ref, gs_ref,
                   xn_ref, carry_ref, *, tiles_per_seq, n_prompt_tiles, t_new):
    i, j = pl.program_id(0), pl.program_id(1)

    @pl.when(j == 0)
    def _():
        xn_ref[...] = _rms(x_ref[...], g_ref[...]).astype(BF16)

    @pl.when(i == 0)
    def _():
        carry_ref[j] = jnp.zeros(carry_ref.shape[1:], F32)

    xn = xn_ref[...]
    gate = _dot(xn, wg_ref[...])
    val = _dot(xn, wv_ref[...])
    tm = gate.shape[0]
    tail = gate[tm - SUBLANES:, :]
    tail_ref[...] = tail
    gs_ref[...] = gate
    halo = jnp.where(i % tiles_per_seq == 0, 0.0, carry_ref[j])
    carry_ref[j] = tail
    row = lax.broadcasted_iota(jnp.int32, (tm, 1), 0)
    r1 = pltpu.roll(gate, 1, 0)
    r2 = pltpu.roll(gate, 2, 0)
    p1 = jnp.where(row == 0, halo[7:8, :], r1)
    p2 = jnp.where(row == 0, halo[6:7, :], jnp.where(row == 1, halo[7:8, :], r2))
    t = row % t_new
    is_sample = i >= n_prompt_tiles
    p1 = jnp.where(is_sample, jnp.where(t >= 1, r1, s1_ref[...]), p1)
    p2 = jnp.where(is_sample, jnp.where(t >= 2, r2, s2_ref[...]), p2)
    act_ref[...] = _conv_act(gate, p1, p2, val, cw_ref, cb_ref)


def ffn_up_act(h, norm_g, w_up, conv_w, conv_b, conv_state, *, batch, seq, t_new, tm, tn, layer=0):
    n, k = h.shape
    d_ff = w_up.shape[2] // 2
    nj = d_ff // tn
    db = conv_state.shape[0]
    n_s = db * t_new
    assert n_s == tm and seq % tm == 0 and n == batch * seq + n_s
    n_tiles = n // tm
    n_prompt_tiles = n_tiles - 1
    s1 = jnp.concatenate([conv_state[:, 1:2], jnp.zeros((db, t_new - 1, d_ff), F32)], axis=1).reshape(n_s, d_ff)
    s2 = jnp.concatenate([conv_state, jnp.zeros((db, t_new - 2, d_ff), F32)], axis=1).reshape(n_s, d_ff)
    sample_col = lambda i, j: (0, jnp.where(i == n_prompt_tiles, j, 0))
    return pl.pallas_call(
        functools.partial(_ffn_up_kernel, tiles_per_seq=seq // tm, n_prompt_tiles=n_prompt_tiles, t_new=t_new),
        grid=(n_tiles, nj),
        in_specs=[pl.BlockSpec((tm, k), lambda i, j: (i, 0)),
                  pl.BlockSpec((1, k), lambda i, j: (0, 0)),
                  pl.BlockSpec((None, k, tn), lambda i, j: (layer, 0, j)),
                  pl.BlockSpec((None, k, tn), lambda i, j: (layer, 0, nj + j)),
                  pl.BlockSpec((FFN_CONV, tn), lambda i, j: (0, j)),
                  pl.BlockSpec((1, tn), lambda i, j: (0, j)),
                  pl.BlockSpec((tm, tn), sample_col),
                  pl.BlockSpec((tm, tn), sample_col)],
        out_specs=[pl.BlockSpec((tm, tn), lambda i, j: (i, j)),
                   pl.BlockSpec((SUBLANES, tn), lambda i, j: (i, j)),
                   pl.BlockSpec((tm, tn), sample_col)],
        out_shape=[jax.ShapeDtypeStruct((n, d_ff), BF16),
                   jax.ShapeDtypeStruct((n_tiles * SUBLANES, d_ff), F32),
                   jax.ShapeDtypeStruct((tm, d_ff), F32)],
        scratch_shapes=[pltpu.VMEM((tm, k), BF16), pltpu.VMEM((nj, SUBLANES, tn), F32)],
        compiler_params=pltpu.CompilerParams(dimension_semantics=("arbitrary", "arbitrary")),
        name="ffn_up_act",
    )(h, norm_g.reshape(1, k), w_up, w_up, conv_w, conv_b.reshape(1, d_ff), s1, s2)


def _gla_kernel(q_ref, k_ref, v_ref, r_ref, m_ref, w2_ref, b2_ref, on_ref, s0_ref, o_ref, sout_ref,
                st_ref, cum_ref, *, chunk, n_valid, nsub, native_state):
    i = pl.program_id(1)
    tt = nsub * chunk

    @pl.when(i == 0)
    def _():
        for h in range(GLA_HEADS):
            st_ref[h] = s0_ref[0, h] if native_state else s0_ref[0, h].T

    a = _dot(m_ref[0].astype(BF16), w2_ref[...]) + b2_ref[...]
    a = (jnp.minimum(a, 0.0) - jnp.log(1.0 + jnp.exp(-jnp.abs(a)))) / GLA_TAU
    pos = lax.broadcasted_iota(jnp.int32, (tt, 1), 0) % chunk
    if n_valid < chunk:
        a = jnp.where(pos < n_valid, a, 0.0)
    cum = a
    sh = 1
    while sh < chunk:
        cum = cum + jnp.where(pos >= sh, pltpu.roll(cum, sh, 0), 0.0)
        sh *= 2
    cum_ref[...] = cum

    ti = lax.broadcasted_iota(jnp.int32, (chunk, 1), 0)

    def step(c, carry):
        rows = pl.ds(pl.multiple_of(c * chunk, chunk), chunk)
        for h in range(GLA_HEADS):
            kcols = slice(h * GLA_DK, (h + 1) * GLA_DK)
            vcols = slice(h * GLA_DV, (h + 1) * GLA_DV)
            qh = q_ref[0, rows, kcols] * GLA_DK ** -0.5
            kh = k_ref[0, rows, kcols]
            vh = v_ref[0, rows, vcols]
            ch = cum_ref[rows, kcols]
            st = st_ref[h]
            qd = (qh * jnp.exp(ch)).astype(BF16)
            o = _dot(qd, st.astype(BF16)) if native_state else _dot_nt(qd, st.astype(BF16))
            for s in range(n_valid):
                lo = s // SUBLANES * SUBLANES
                d = jnp.exp(jnp.where(ti[lo:] >= s, ch[lo:] - ch[s:s + 1, :], NEG))
                w = jnp.sum(qh[lo:] * kh[s:s + 1, :] * d, axis=-1, keepdims=True)
                upd = o[lo:] + w * vh[s:s + 1, :]
                o = upd if lo == 0 else jnp.concatenate([o[:lo], upd], axis=0)
            last = ch[chunk - 1:chunk, :]
            kt = kh * jnp.exp(last - ch)
            if native_state:
                dcol = jnp.broadcast_to(jnp.exp(last), (SUBLANES, GLA_DK)).T[:, 0:1]
                st_ref[h] = st * dcol + _dot_tn(kt.astype(BF16), vh.astype(BF16))
            else:
                st_ref[h] = st * jnp.exp(last) + _dot_tn(vh.astype(BF16), kt.astype(BF16))
            rh = r_ref[0, rows, vcols]
            o_ref[0, rows, vcols] = (_rms(o, on_ref[...]) * _silu(rh)).astype(o_ref.dtype)
        return carry

    lax.fori_loop(0, nsub, step, 0)

    @pl.when(i == pl.num_programs(1) - 1)
    def _():
        for h in range(GLA_HEADS):
            sout_ref[0, h] = st_ref[h] if native_state else st_ref[h].T


def gla_mixer(z3, s0, w2p, b2, onorm, *, n_seq, tiles_per_seq, chunk, n_valid, out_dtype, native_state=False):
    tt = z3.shape[1]
    nsub = tt // chunk
    dqk = GLA_HEADS * GLA_DK
    dv = GLA_HEADS * GLA_DV
    tile = lambda b, i: b * tiles_per_seq + i
    return pl.pallas_call(
        functools.partial(_gla_kernel, chunk=chunk, n_valid=n_valid, nsub=nsub, native_state=native_state),
        grid=(n_seq, tiles_per_seq),
        in_specs=[pl.BlockSpec((1, tt, dqk), lambda b, i: (tile(b, i), 0, AB_GQ // dqk)),
                  pl.BlockSpec((1, tt, dqk), lambda b, i: (tile(b, i), 0, AB_GK // dqk)),
                  pl.BlockSpec((1, tt, dv), lambda b, i: (tile(b, i), 0, AB_GV // dv)),
                  pl.BlockSpec((1, tt, dv), lambda b, i: (tile(b, i), 0, AB_GR // dv)),
                  pl.BlockSpec((1, tt, LANES), lambda b, i: (tile(b, i), 0, AB_MISC0 // LANES)),
                  pl.BlockSpec((LANES, dqk), lambda b, i: (0, 0)),
                  pl.BlockSpec((1, dqk), lambda b, i: (0, 0)),
                  pl.BlockSpec((1, GLA_DV), lambda b, i: (0, 0)),
                  pl.BlockSpec((1, GLA_HEADS, GLA_DK, GLA_DV), lambda b, i: (b, 0, 0, 0))],
        out_specs=[pl.BlockSpec((1, tt, dv), lambda b, i: (tile(b, i), 0, 0)),
                   pl.BlockSpec((1, GLA_HEADS, GLA_DK, GLA_DV), lambda b, i: (b, 0, 0, 0))],
        out_shape=[jax.ShapeDtypeStruct((n_seq * tiles_per_seq, tt, dv), out_dtype),
                   jax.ShapeDtypeStruct((n_seq, GLA_HEADS, GLA_DK, GLA_DV), F32)],
        scratch_shapes=[pltpu.VMEM((GLA_HEADS, GLA_DK, GLA_DV) if native_state else (GLA_HEADS, GLA_DV, GLA_DK), F32),
                        pltpu.VMEM((tt, dqk), F32)],
        compiler_params=pltpu.CompilerParams(dimension_semantics=("parallel", "arbitrary")),
        name="gla_mixer",
    )(z3, z3, z3, z3, z3, w2p, b2.reshape(1, dqk), onorm.reshape(1, GLA_DV), s0)


def _masked_softmax_rows(s, mask):
    sm = jnp.where(mask[None], s, NEG)
    m = jnp.max(sm, axis=-1, keepdims=True)
    e = jnp.where(mask[None], jnp.exp2(sm - m), 0.0)
    d = jnp.sum(e, axis=-1, keepdims=True)
    return e / jnp.where(d > 0, d, 1.0)


def _block_expand(n_blocks_pad, n_keys, block, key0=0):
    bi = lax.broadcasted_iota(jnp.int32, (n_blocks_pad, n_keys), 0)
    ki = lax.broadcasted_iota(jnp.int32, (n_blocks_pad, n_keys), 1) + key0
    return (bi == jnp.right_shift(ki, int(math.log2(block)))).astype(BF16)


def _gelu_tanh(x):
    return x * (0.5 * (1.0 + jnp.tanh(math.sqrt(2.0 / math.pi) * (x + 0.044715 * (x * x * x)))))


def _cmp_mlp(chunk_rows, n_ch, w1_ref, cpe_ref, cb1_ref, w2_ref, cb2_ref, kg_ref):
    hid_w = NSA_CMP_HIDDEN
    out = {}
    for e in range(2):
        accp = _dot(cpe_ref[e], w1_ref[e])
        pe = accp[0:1, :hid_w] + accp[1:2, hid_w:] + cb1_ref[e]
        x = jnp.concatenate([chunk_rows(e * NSA_KV + g) for g in range(NSA_KV)], axis=0).astype(BF16)
        acc = _dot(x, w1_ref[e])
        hid = []
        for g in range(NSA_KV):
            a = acc[g * n_ch:(g + 1) * n_ch]
            hid.append(_gelu_tanh(a[:, :hid_w] + pltpu.roll(a[:, hid_w:], n_ch - 1, 0) + pe))
        ckv = _dot(jnp.concatenate(hid, axis=0).astype(BF16), w2_ref[e]) + cb2_ref[e]
        for g in range(NSA_KV):
            out[e, g] = ckv[g * n_ch:(g + 1) * n_ch]
    kc = [_rms(out[0, g], kg_ref[...]) for g in range(NSA_KV)]
    vc = [out[1, g] for g in range(NSA_KV)]
    return kc, vc


def _nsa_q4(q, qg):
    return jnp.concatenate([_rms(q[:, r * HEAD_DIM:(r + 1) * HEAD_DIM], qg) * (HEAD_DIM ** -0.5 * LOG2E)
                            for r in range(NSA_REP)], axis=0)


def _nsa_cmp_branch(q4, kc, vc, qpos, n_cmp):
    nq = qpos.shape[0]
    n_ch = kc.shape[0]
    sc = _dot_nt(q4, kc.astype(BF16)).reshape(NSA_REP, nq, n_ch)
    cidx = lax.broadcasted_iota(jnp.int32, (1, n_ch), 1)
    cmask = (NSA_CMP_STRIDE * cidx + NSA_CMP_LEN - 1 <= qpos) & (cidx < n_cmp)
    p = _masked_softmax_rows(sc, cmask)
    o = _dot(p.reshape(NSA_REP * nq, n_ch).astype(BF16), vc.astype(BF16))
    return o, jnp.sum(p, axis=0)


def _nsa_select(pcs, qpos0, n_sb):
    n_ch = pcs.shape[1]
    ratio = NSA_SEL_BLOCK // NSA_CMP_STRIDE
    bi = lax.broadcasted_iota(jnp.int32, (LANES, n_ch), 0)
    ci = lax.broadcasted_iota(jnp.int32, (LANES, n_ch), 1)
    mimp = ((ci >= ratio * bi - 1) & (ci <= ratio * bi + ratio - 1)).astype(BF16)
    hi = pcs.astype(BF16)
    r1 = pcs - hi.astype(F32)
    mid = r1.astype(BF16)
    lo = (r1 - mid.astype(F32)).astype(BF16)
    imp = _dot_nt(mimp, hi) + _dot_nt(mimp, mid) + _dot_nt(mimp, lo)
    nbp = -(-n_sb // SUBLANES) * SUBLANES
    blk = lax.broadcasted_iota(jnp.int32, (nbp, 1), 0)
    qpos = qpos0 + lax.broadcasted_iota(jnp.int32, (1, LANES), 1)
    cur = jnp.right_shift(qpos, int(math.log2(NSA_SEL_BLOCK)))
    valid = (blk <= cur) & (blk < n_sb)
    forced = (blk == 0) | (blk == cur) | (blk == cur - 1)
    score = jnp.where(forced, -NEG, jnp.where(valid, imp[:nbp], NEG))
    rank = jnp.zeros((nbp, LANES), F32)
    for k in range(n_sb):
        sk = score[k:k + 1, :]
        rank = rank + jnp.where((sk > score) | ((sk == score) & (blk > k)), 1.0, 0.0)
    sel_t = jnp.where((rank < min(NSA_N_SEL, n_sb)) & (score > 0.5 * NEG), 1.0, 0.0)
    sel_t = jnp.concatenate([sel_t, jnp.zeros((LANES - nbp, LANES), F32)], axis=0)
    return sel_t.T


def _nsa_window_mask(kpos, qpos):
    return (kpos <= qpos) & (qpos - kpos < NSA_WINDOW) & (kpos >= 0)


def _nsa_gate_mix(gsig, o_cmp, o_sel, o_win, nq):
    outs = []
    for r in range(NSA_REP):
        c0 = MISC_GATE_COL + 3 * r
        rows = slice(r * nq, (r + 1) * nq)
        outs.append(gsig[:, c0:c0 + 1] * o_cmp[rows] + gsig[:, c0 + 1:c0 + 2] * o_sel[rows]
                    + gsig[:, c0 + 2:c0 + 3] * o_win[rows])
    return jnp.concatenate(outs, axis=1)


def _nsa_cmp_kernel(*refs, n_ch):
    x_refs = refs[:NSA_CMP_STRIDE]
    w1_ref, cpe_ref, cb1_ref, w2_ref, cb2_ref, kg_ref, kc_ref, vc_ref = refs[NSA_CMP_STRIDE:]

    def chunk_rows(eg):
        return jnp.concatenate([x_refs[s][:, eg * HEAD_DIM:(eg + 1) * HEAD_DIM] for s in range(NSA_CMP_STRIDE)], axis=1)

    kc, vc = _cmp_mlp(chunk_rows, n_ch, w1_ref, cpe_ref, cb1_ref, w2_ref, cb2_ref, kg_ref)
    for g in range(NSA_KV):
        kc_ref[0, g] = kc[g]
        vc_ref[0, g] = vc[g]


def _cmp_weight_specs():
    z = (0,) * 8
    full = lambda shape: pl.BlockSpec(shape, lambda *a: z[:len(shape)])
    taps = NSA_CMP_STRIDE * HEAD_DIM
    return [full((2, taps, 2 * NSA_CMP_HIDDEN)), full((2, SUBLANES, taps)),
            full((2, 1, NSA_CMP_HIDDEN)), full((2, NSA_CMP_HIDDEN, HEAD_DIM)), full((2, 1, HEAD_DIM)),
            full((1, HEAD_DIM))]


def nsa_cmp_prompt(z, cmpw, *, batch, seq):
    n_ch = seq // NSA_CMP_STRIDE
    seg = 4 * HEAD_DIM
    st = NSA_CMP_STRIDE
    zc = z[:batch * seq, AB_CMP:AB_CMP + seg].reshape(batch * n_ch, st * seg)
    shp = jax.ShapeDtypeStruct((batch, NSA_KV, n_ch, HEAD_DIM), F32)
    ospec = pl.BlockSpec((1, NSA_KV, n_ch, HEAD_DIM), lambda b: (b, 0, 0, 0))
    xspec = lambda s: pl.BlockSpec((n_ch, seg), lambda b: (b, s))
    return pl.pallas_call(
        functools.partial(_nsa_cmp_kernel, n_ch=n_ch),
        grid=(batch,),
        in_specs=[xspec(s) for s in range(st)] + _cmp_weight_specs(),
        out_specs=[ospec, ospec],
        out_shape=[shp, shp],
        compiler_params=pltpu.CompilerParams(dimension_semantics=("parallel",)),
        name="nsa_cmp_prompt",
    )(*([zc] * st), *cmpw)


def _nsa_prep_kernel(sel_ref, win_ref, kg_ref, o_ref):
    d = HEAD_DIM
    parts = []
    for ref, row in ((sel_ref, 1), (win_ref, 2)):
        x = ref[...]
        for g in range(NSA_KV):
            parts.append(_rms(x[:, g * d:(g + 1) * d], kg_ref[row:row + 1, :]))
        parts.append(x[:, NSA_KV * d:])
    o_ref[...] = jnp.concatenate(parts, axis=1).astype(BF16)


def nsa_prep_prompt(z, k_gain, *, rows, tm):
    seg = 4 * HEAD_DIM
    return pl.pallas_call(
        _nsa_prep_kernel,
        grid=(rows // tm,),
        in_specs=[pl.BlockSpec((tm, seg), lambda i: (i, AB_SEL // seg)),
                  pl.BlockSpec((tm, seg), lambda i: (i, AB_WIN // seg)),
                  pl.BlockSpec((3, HEAD_DIM), lambda i: (0, 0))],
        out_specs=pl.BlockSpec((tm, 2 * seg), lambda i: (i, 0)),
        out_shape=jax.ShapeDtypeStruct((rows, 2 * seg), BF16),
        compiler_params=pltpu.CompilerParams(dimension_semantics=("parallel",)),
        name="nsa_prep_prompt",
    )(z, z, k_gain)


def _nsa_attn_kernel(q_ref, m_ref, kc_ref, vc_ref, ks_ref, vs_ref, kw_ref, vw_ref, oh_ref, qg_ref, o_ref, *, seq, kt):
    i = pl.program_id(2)
    nq = q_ref.shape[0]
    assert nq == LANES
    q4 = _nsa_q4(q_ref[...], qg_ref[...]).astype(BF16)
    qpos = i * nq + lax.broadcasted_iota(jnp.int32, (nq, 1), 0)
    n_ch = seq // NSA_CMP_STRIDE
    n_sb = -(-seq // NSA_SEL_BLOCK)

    o_cmp, pcs = _nsa_cmp_branch(q4, kc_ref[0, 0], vc_ref[0, 0], qpos, n_ch - 1)
    sel = _nsa_select(pcs, i * nq, n_sb)
    bias = jnp.where(sel > 0.5, 0.0, MASK_BIAS).astype(BF16)
    q_aug = jnp.concatenate([q4, jnp.concatenate([bias] * NSA_REP, axis=0)], axis=1)

    def sel_step(j, carry, causal):
        m, l, acc = carry
        rows = pl.ds(pl.multiple_of(j * kt, kt), kt)
        k_aug = jnp.concatenate([ks_ref[rows, :], oh_ref[rows, :]], axis=1)
        s = _dot_nt(q_aug, k_aug).reshape(NSA_REP, nq, kt)
        if causal:
            kpos = j * kt + lax.broadcasted_iota(jnp.int32, (1, kt), 1)
            s = jnp.where((kpos <= qpos)[None], s, NEG)
        mn = jnp.maximum(m, jnp.max(s, axis=-1, keepdims=True))
        p = jnp.exp2(s - mn)
        alpha = jnp.exp2(m - mn)
        l = alpha * l + jnp.sum(p, axis=-1, keepdims=True)
        pv = _dot(p.reshape(NSA_REP * nq, kt).astype(BF16), vs_ref[rows, :])
        return mn, l, alpha * acc + pv.reshape(NSA_REP, nq, HEAD_DIM)

    last = (i * nq + nq - 1) // kt
    init = (jnp.full((NSA_REP, nq, 1), NEG, F32), jnp.zeros((NSA_REP, nq, 1), F32),
            jnp.zeros((NSA_REP, nq, HEAD_DIM), F32))
    carry = lax.fori_loop(0, last, functools.partial(sel_step, causal=False), init)
    _, l, acc = sel_step(last, carry, True)
    o_sel = (acc / l).reshape(NSA_REP * nq, HEAD_DIM)

    span = min(NSA_WINDOW + nq, seq)
    start = pl.multiple_of(jnp.clip(i * nq - NSA_WINDOW, 0, seq - span), nq)
    s = _dot_nt(q4, kw_ref[pl.ds(start, span), :]).reshape(NSA_REP, nq, span)
    kpos = start + lax.broadcasted_iota(jnp.int32, (1, span), 1)
    p = _masked_softmax_rows(s, _nsa_window_mask(kpos, qpos))
    o_win = _dot(p.reshape(NSA_REP * nq, span).astype(BF16), vw_ref[pl.ds(start, span), :])

    o_ref[...] = _nsa_gate_mix(jax.nn.sigmoid(m_ref[...]), o_cmp, o_sel, o_win, nq).astype(o_ref.dtype)


def nsa_attn_prompt(z, kvp, kc, vc, q_gain, *, batch, seq, tq):
    nt = seq // tq
    gw = NSA_REP * HEAD_DIM
    kt = min(4 * tq, seq)
    row = lambda b, g, i: b * nt + i
    cspec = pl.BlockSpec((1, 1, seq // NSA_CMP_STRIDE, HEAD_DIM), lambda b, g, i: (b, g, 0, 0))
    kvspec = lambda c: pl.BlockSpec((seq, HEAD_DIM), lambda b, g, i: (b, c + g))
    return pl.pallas_call(
        functools.partial(_nsa_attn_kernel, seq=seq, kt=kt),
        grid=(batch, NSA_KV, nt),
        in_specs=[pl.BlockSpec((tq, gw), lambda b, g, i: (row(b, g, i), AB_NQ // gw + g)),
                  pl.BlockSpec((tq, LANES), lambda b, g, i: (row(b, g, i), AB_MISC0 // LANES + g)),
                  cspec, cspec, kvspec(0), kvspec(2), kvspec(4), kvspec(6),
                  pl.BlockSpec((seq, LANES), lambda b, g, i: (0, 0)),
                  pl.BlockSpec((1, HEAD_DIM), lambda b, g, i: (0, 0))],
        out_specs=pl.BlockSpec((tq, gw), lambda b, g, i: (row(b, g, i), g)),
        out_shape=jax.ShapeDtypeStruct((batch * seq, NSA_KV * gw), BF16),
        compiler_params=pltpu.CompilerParams(dimension_semantics=("parallel", "parallel", "arbitrary")),
        name="nsa_attn_prompt",
    )(z, z, kc, vc, kvp, kvp, kvp, kvp, _block_onehot(seq, NSA_SEL_BLOCK), q_gain.reshape(1, HEAD_DIM))


def _block_onehot(n_keys, block):
    k = lax.broadcasted_iota(jnp.int32, (n_keys, LANES), 0) // block
    return (k == lax.broadcasted_iota(jnp.int32, (n_keys, LANES), 1)).astype(BF16)


def nsa_cmp_weights(cw1, cb1, cw2, cb2, cpe, k_gain):
    st = NSA_CMP_STRIDE
    taps = st * HEAD_DIM
    w1cat = jnp.concatenate([cw1[:, :st], cw1[:, st:]], axis=-1).reshape(2, taps, 2 * NSA_CMP_HIDDEN).astype(BF16)
    pe_rows = jnp.stack([cpe[:, :st].reshape(2, taps), cpe[:, st:].reshape(2, taps)], axis=1)
    pe_rows = jnp.pad(pe_rows, ((0, 0), (0, SUBLANES - 2), (0, 0))).astype(BF16)
    return (w1cat, pe_rows, cb1.reshape(2, 1, NSA_CMP_HIDDEN), cw2.astype(BF16), cb2.reshape(2, 1, HEAD_DIM),
            k_gain[0].reshape(1, HEAD_DIM))


NSA_ROW = 4 * NSA_KV * HEAD_DIM
CHUNKS_PER_PAGE = PAGE_SIZE // NSA_CMP_STRIDE


def _nsa_sample_kernel(pt_ref, *refs, n_pages, past, t_new):
    del pt_ref
    pages = refs[:n_pages]
    (q_ref, sel_ref, win_ref, m0_ref, m1_ref, wb_ref, w1_ref, cpe_ref, cb1_ref, w2_ref, cb2_ref, kg0_ref,
     kg_ref, qg_ref, o_ref) = refs[n_pages:]
    nq = SAMPLE_ROWS
    d = HEAD_DIM
    n_ch = n_pages * CHUNKS_PER_PAGE
    n_cmp = (past + t_new) // NSA_CMP_STRIDE - 1
    n_sb = -(-(past + t_new) // NSA_SEL_BLOCK)
    n_sl = NSA_ROW // d
    n_wsl = 2 * NSA_KV

    def chunk_rows(j):
        return jnp.concatenate(
            [jnp.concatenate([pg[0, pl.ds(s * n_sl + j, CHUNKS_PER_PAGE, stride=NSA_CMP_STRIDE * n_sl), :]
                              for s in range(NSA_CMP_STRIDE)], axis=1) for pg in pages], axis=0)

    def past_rows(j):
        return jnp.concatenate([pg[0, pl.ds(j, PAGE_SIZE, stride=n_sl), :] for pg in pages], axis=0)

    kc, vc = _cmp_mlp(chunk_rows, n_ch, w1_ref, cpe_ref, cb1_ref, w2_ref, cb2_ref, kg0_ref)

    qpos = past + lax.broadcasted_iota(jnp.int32, (nq, 1), 0)
    nk = past + LANES
    kpos = lax.broadcasted_iota(jnp.int32, (1, nk), 1)
    esel = _block_expand(LANES, nk, NSA_SEL_BLOCK)
    causal = kpos <= qpos
    nw = NSA_WINDOW + LANES
    wmask = _nsa_window_mask(past - NSA_WINDOW + lax.broadcasted_iota(jnp.int32, (1, nw), 1), qpos)
    zpad = jnp.zeros((LANES - nq, d), F32)
    q = q_ref[0]
    sel_new = sel_ref[0]
    win_new = win_ref[0]
    kg_sel = kg_ref[1:2, :]
    kg_win = kg_ref[2:3, :]
    col = lambda x, c: x[:, c * d:(c + 1) * d]
    wb_rows = lambda j: wb_ref[0, pl.ds(j, NSA_WINDOW, stride=n_wsl), :]

    for g in range(NSA_KV):
        q4 = _nsa_q4(q[:, g * NSA_REP * d:(g + 1) * NSA_REP * d], qg_ref[...]).astype(BF16)
        o_cmp, pcs = _nsa_cmp_branch(q4, kc[g], vc[g], qpos, n_cmp)
        pcs = jnp.concatenate([pcs, jnp.zeros((LANES - nq, n_ch), F32)], axis=0)
        selb = _nsa_select(pcs, past, n_sb)[:nq].astype(BF16)

        ks = jnp.concatenate([_rms(past_rows(2 * NSA_KV + g), kg_sel), _rms(col(sel_new, g), kg_sel), zpad], axis=0)
        vs = jnp.concatenate([past_rows(3 * NSA_KV + g), col(sel_new, NSA_KV + g), zpad], axis=0)
        s = _dot_nt(q4, ks.astype(BF16)).reshape(NSA_REP, nq, nk)
        p = _masked_softmax_rows(s, (_dot(selb, esel) > 0.5) & causal)
        o_sel = _dot(p.reshape(NSA_REP * nq, nk).astype(BF16), vs.astype(BF16))

        kw = jnp.concatenate([_rms(wb_rows(g), kg_win), _rms(col(win_new, g), kg_win), zpad], axis=0)
        vw = jnp.concatenate([wb_rows(NSA_KV + g), col(win_new, NSA_KV + g), zpad], axis=0)
        s = _dot_nt(q4, kw.astype(BF16)).reshape(NSA_REP, nq, nw)
        p = _masked_softmax_rows(s, wmask)
        o_win = _dot(p.reshape(NSA_REP * nq, nw).astype(BF16), vw.astype(BF16))

        gsig = jax.nn.sigmoid((m0_ref, m1_ref)[g][0])
        o_ref[0, :, g * NSA_REP * d:(g + 1) * NSA_REP * d] = _nsa_gate_mix(gsig, o_cmp, o_sel, o_win, nq).astype(o_ref.dtype)


def nsa_sample(zs, page_table, cache_kv, cache_win, cmpw, k_gain, q_gain, *, t_new):
    db, n_pages = page_table.shape
    past = n_pages * PAGE_SIZE
    assert cache_win.shape[1] == NSA_WINDOW and (past + t_new) // NSA_CMP_STRIDE == n_pages * CHUNKS_PER_PAGE
    n_phys = cache_kv.shape[0]
    page_rows = PAGE_SIZE * NSA_ROW // HEAD_DIM
    win_rows = NSA_WINDOW * 2 * NSA_KV
    pages = cache_kv.reshape(n_phys, page_rows, HEAD_DIM)
    wb = cache_win.reshape(db, win_rows, HEAD_DIM)
    seg = 4 * HEAD_DIM
    qw = NSA_HEADS * HEAD_DIM
    zspec = lambda w, off: pl.BlockSpec((1, SAMPLE_ROWS, w), lambda b, pt: (b, 0, off // w))
    page_spec = lambda p: pl.BlockSpec((1, page_rows, HEAD_DIM), lambda b, pt: (pt[b, p], 0, 0))
    grid_spec = pltpu.PrefetchScalarGridSpec(
        num_scalar_prefetch=1,
        grid=(db,),
        in_specs=[page_spec(p) for p in range(n_pages)]
        + [zspec(qw, AB_NQ), zspec(seg, AB_SEL), zspec(seg, AB_WIN), zspec(LANES, AB_MISC0), zspec(LANES, AB_MISC1),
           pl.BlockSpec((1, win_rows, HEAD_DIM), lambda b, pt: (b, 0, 0))]
        + _cmp_weight_specs()
        + [pl.BlockSpec((3, HEAD_DIM), lambda b, pt: (0, 0)), pl.BlockSpec((1, HEAD_DIM), lambda b, pt: (0, 0))],
        out_specs=pl.BlockSpec((1, SAMPLE_ROWS, qw), lambda b, pt: (b, 0, 0)),
    )
    return pl.pallas_call(
        functools.partial(_nsa_sample_kernel, n_pages=n_pages, past=past, t_new=t_new),
        grid_spec=grid_spec,
        out_shape=jax.ShapeDtypeStruct((db, SAMPLE_ROWS, qw), BF16),
        compiler_params=pltpu.CompilerParams(dimension_semantics=("parallel",)),
        name="nsa_sample",
    )(page_table, *([pages] * n_pages), zs, zs, zs, zs, zs, wb, *cmpw, k_gain, q_gain.reshape(1, HEAD_DIM))


C_Q, C_K, C_V, C_TOTAL = 0, MOBA_HEADS * HEAD_DIM, (MOBA_HEADS + MOBA_KV) * HEAD_DIM, (MOBA_HEADS + 2 * MOBA_KV) * HEAD_DIM
MOBA_SUBTILE = 1024


def _masked_softmax2d(s, mask):
    sm = jnp.where(mask, s, NEG)
    m = jnp.max(sm, axis=-1, keepdims=True)
    e = jnp.where(mask, jnp.exp2(sm - m), 0.0)
    dsum = jnp.sum(e, axis=-1, keepdims=True)
    return e / jnp.where(dsum > 0, dsum, 1.0)


def _moba_q4(q, qg):
    return jnp.concatenate([_rms(q[:, r * HEAD_DIM:(r + 1) * HEAD_DIM], qg) * (HEAD_DIM ** -0.5 * LOG2E)
                            for r in range(MOBA_REP)], axis=0)


def _moba_select(q4, kmean_pad, cur, n_blocks):
    gs = _dot_nt(q4, kmean_pad.astype(BF16))
    blk = lax.broadcasted_iota(jnp.int32, (1, LANES), 1)
    score = jnp.where((blk < cur) & (blk < n_blocks), gs, NEG)
    rank = jnp.zeros(score.shape, F32)
    for k in range(n_blocks):
        sk = score[:, k:k + 1]
        rank = rank + jnp.where((sk > score) | ((sk == score) & (blk > k)), 1.0, 0.0)
    sel = jnp.where((rank < min(MOBA_TOPK, n_blocks)) & (score > 0.5 * NEG), 1.0, 0.0)
    return jnp.where(blk == cur, 1.0, sel)


def _moba_prep_kernel(k_ref, v_ref, kg_ref, o_ref, km_ref):
    k = k_ref[...]
    kn = jnp.concatenate([_rms(k[:, h * HEAD_DIM:(h + 1) * HEAD_DIM], kg_ref[...]) for h in range(MOBA_KV)], axis=1)
    km_ref[0] = jnp.mean(kn, axis=0, keepdims=True)
    o_ref[...] = jnp.concatenate([kn, v_ref[...]], axis=1).astype(BF16)


def moba_prep_prompt(zc, k_gain, *, rows):
    w = MOBA_KV * HEAD_DIM
    nblk = rows // MOBA_BLOCK
    return pl.pallas_call(
        _moba_prep_kernel,
        grid=(nblk,),
        in_specs=[pl.BlockSpec((MOBA_BLOCK, w), lambda i: (i, C_K // w)),
                  pl.BlockSpec((MOBA_BLOCK, w), lambda i: (i, C_V // w)),
                  pl.BlockSpec((1, HEAD_DIM), lambda i: (0, 0))],
        out_specs=[pl.BlockSpec((MOBA_BLOCK, 2 * w), lambda i: (i, 0)),
                   pl.BlockSpec((1, 1, w), lambda i: (i, 0, 0))],
        out_shape=[jax.ShapeDtypeStruct((rows, 2 * w), BF16), jax.ShapeDtypeStruct((nblk, 1, w), F32)],
        compiler_params=pltpu.CompilerParams(dimension_semantics=("parallel",)),
        name="moba_prep_prompt",
    )(zc, zc, k_gain.reshape(1, HEAD_DIM))


def _moba_attn_kernel(q_ref, km_ref, k_ref, v_ref, oh_ref, qg_ref, o_ref, *, seq):
    i = pl.program_id(2)
    nq = q_ref.shape[0]
    assert nq == MOBA_BLOCK
    nb = seq // MOBA_BLOCK
    rq = MOBA_REP * nq
    q4 = _moba_q4(q_ref[...], qg_ref[...]).astype(BF16)

    km = jnp.concatenate([km_ref[0], jnp.zeros((LANES - nb, HEAD_DIM), F32)], axis=0)
    nbp = -(-nb // SUBLANES) * SUBLANES
    gs = _dot_nt(km.astype(BF16), q4)[:nbp]
    blk = lax.broadcasted_iota(jnp.int32, (nbp, 1), 0)
    score = jnp.where((blk < i) & (blk < nb), gs, NEG)
    rank = jnp.zeros((nbp, rq), F32)
    for k in range(nb):
        sk = score[k:k + 1, :]
        rank = rank + jnp.where((sk > score) | ((sk == score) & (blk > k)), 1.0, 0.0)
    chosen = ((rank < min(MOBA_TOPK, nb)) & (score > 0.5 * NEG)) | (blk == i)
    bias_t = jnp.concatenate([jnp.where(chosen, 0.0, MASK_BIAS), jnp.zeros((LANES - nbp, rq), F32)], axis=0)
    q_aug = jnp.concatenate([q4, bias_t.T.astype(BF16)], axis=1)

    def update(carry, s, rows):
        m, l, acc = carry
        mn = jnp.maximum(m, jnp.max(s, axis=-1, keepdims=True))
        p = jnp.exp2(s - mn)
        alpha = jnp.exp2(m - mn)
        l = alpha * l + jnp.sum(p, axis=-1, keepdims=True)
        return mn, l, alpha * acc + _dot(p.astype(BF16), v_ref[rows, :])

    own = pl.ds(pl.multiple_of(i * MOBA_BLOCK, MOBA_BLOCK), MOBA_BLOCK)
    kcol = lax.broadcasted_iota(jnp.int32, (1, MOBA_BLOCK), 1)
    rs = min(MOBA_SUBTILE, rq)
    outs = []
    for t in range(rq // rs):
        qa = q_aug[t * rs:(t + 1) * rs]

        def past_step(n, carry, qa=qa):
            rows = pl.ds(pl.multiple_of(n * MOBA_BLOCK, MOBA_BLOCK), MOBA_BLOCK)
            k_aug = jnp.concatenate([k_ref[rows, :], oh_ref[rows, :]], axis=1)
            return update(carry, _dot_nt(qa, k_aug), rows)

        init = (jnp.full((rs, 1), NEG, F32), jnp.zeros((rs, 1), F32), jnp.zeros((rs, HEAD_DIM), F32))
        carry = lax.fori_loop(0, i, past_step, init)
        tq = (t * rs + lax.broadcasted_iota(jnp.int32, (rs, 1), 0)) % nq
        s_own = jnp.where(kcol <= tq, _dot_nt(qa[:, :HEAD_DIM], k_ref[own, :]), NEG)
        _, l, acc = update(carry, s_own, own)
        outs.append(acc / l)
    o = jnp.concatenate(outs, axis=0)
    o_ref[...] = jnp.concatenate([o[r * nq:(r + 1) * nq] for r in range(MOBA_REP)], axis=1).astype(o_ref.dtype)


def moba_attn_prompt(zc, kvp, kmean, q_gain, *, batch, seq):
    nt = seq // MOBA_BLOCK
    gw = MOBA_REP * HEAD_DIM
    km = kmean.reshape(batch, nt, MOBA_KV * HEAD_DIM)
    return pl.pallas_call(
        functools.partial(_moba_attn_kernel, seq=seq),
        grid=(batch, MOBA_KV, nt),
        in_specs=[pl.BlockSpec((MOBA_BLOCK, gw), lambda b, g, i: (b * nt + i, g)),
                  pl.BlockSpec((1, nt, HEAD_DIM), lambda b, g, i: (b, 0, g)),
                  pl.BlockSpec((seq, HEAD_DIM), lambda b, g, i: (b, g)),
                  pl.BlockSpec((seq, HEAD_DIM), lambda b, g, i: (b, MOBA_KV + g)),
                  pl.BlockSpec((seq, LANES), lambda b, g, i: (0, 0)),
                  pl.BlockSpec((1, HEAD_DIM), lambda b, g, i: (0, 0))],
        out_specs=pl.BlockSpec((MOBA_BLOCK, gw), lambda b, g, i: (b * nt + i, g)),
        out_shape=jax.ShapeDtypeStruct((batch * seq, MOBA_HEADS * HEAD_DIM), BF16),
        compiler_params=pltpu.CompilerParams(dimension_semantics=("parallel", "parallel", "arbitrary")),
        name="moba_attn_prompt",
    )(zc, km, kvp, kvp, _block_onehot(seq, MOBA_BLOCK), q_gain.reshape(1, HEAD_DIM))


def _moba_sample_kernel(pt_ref, *refs, n_pages, past, t_new):
    del pt_ref
    pages = refs[:n_pages]
    q_ref, kn_ref, vn_ref, kg_ref, qg_ref, o_ref = refs[n_pages:]
    nq = SAMPLE_ROWS
    d = HEAD_DIM
    rq = MOBA_REP * nq
    nb = -(-(past + t_new) // MOBA_BLOCK)
    n_past_blocks = past // MOBA_BLOCK
    nk = past + LANES
    qpos = past + lax.broadcasted_iota(jnp.int32, (nq, 1), 0)
    qpos4 = jnp.concatenate([qpos] * MOBA_REP, axis=0)
    cur = jnp.right_shift(qpos4, int(math.log2(MOBA_BLOCK)))
    kpos = lax.broadcasted_iota(jnp.int32, (1, nk), 1)
    expand = _block_expand(LANES, nk, MOBA_BLOCK)
    causal = kpos <= qpos4
    zpad = jnp.zeros((LANES - nq, d), F32)
    q = q_ref[0]
    k_new = kn_ref[0]
    v_new = vn_ref[0]
    n_sl = 2 * MOBA_KV
    past_rows = lambda j: jnp.concatenate([pg[0, pl.ds(j, PAGE_SIZE, stride=n_sl), :] for pg in pages], axis=0)
    for g in range(MOBA_KV):
        kn = _rms(past_rows(g), kg_ref[...])
        km = jnp.concatenate([jnp.mean(kn[n * MOBA_BLOCK:(n + 1) * MOBA_BLOCK], axis=0, keepdims=True)
                              for n in range(n_past_blocks)] + [jnp.zeros((LANES - n_past_blocks, d), F32)], axis=0)
        q4 = _moba_q4(q[:, g * MOBA_REP * d:(g + 1) * MOBA_REP * d], qg_ref[...]).astype(BF16)
        sel = _moba_select(q4, km, cur, nb)
        keys = jnp.concatenate([kn, _rms(k_new[:, g * d:(g + 1) * d], kg_ref[...]), zpad], axis=0)
        vals = jnp.concatenate([past_rows(MOBA_KV + g), v_new[:, g * d:(g + 1) * d], zpad], axis=0)
        s = _dot_nt(q4, keys.astype(BF16))
        p = _masked_softmax2d(s, (_dot(sel.astype(BF16), expand) > 0.5) & causal)
        o = _dot(p.astype(BF16), vals.astype(BF16))
        for r in range(MOBA_REP):
            h = g * MOBA_REP + r
            o_ref[0, :, h * d:(h + 1) * d] = o[r * nq:(r + 1) * nq].astype(o_ref.dtype)


def moba_sample(zcs, page_table, cache_kv, k_gain, q_gain, *, t_new):
    db, n_pages = page_table.shape
    past = n_pages * PAGE_SIZE
    assert past % MOBA_BLOCK == 0 and t_new <= MOBA_BLOCK
    n_phys = cache_kv.shape[0]
    w = MOBA_KV * HEAD_DIM
    page_rows = PAGE_SIZE * 2 * MOBA_KV
    pages = cache_kv.reshape(n_phys, page_rows, HEAD_DIM)
    qw = MOBA_HEADS * HEAD_DIM
    zspec = lambda wd, off: pl.BlockSpec((1, SAMPLE_ROWS, wd), lambda b, pt: (b, 0, off // wd))
    page_spec = lambda p: pl.BlockSpec((1, page_rows, HEAD_DIM), lambda b, pt: (pt[b, p], 0, 0))
    grid_spec = pltpu.PrefetchScalarGridSpec(
        num_scalar_prefetch=1,
        grid=(db,),
        in_specs=[page_spec(p) for p in range(n_pages)]
        + [zspec(qw, C_Q), zspec(w, C_K), zspec(w, C_V),
           pl.BlockSpec((1, HEAD_DIM), lambda b, pt: (0, 0)), pl.BlockSpec((1, HEAD_DIM), lambda b, pt: (0, 0))],
        out_specs=pl.BlockSpec((1, SAMPLE_ROWS, qw), lambda b, pt: (b, 0, 0)),
    )
    return pl.pallas_call(
        functools.partial(_moba_sample_kernel, n_pages=n_pages, past=past, t_new=t_new),
        grid_spec=grid_spec,
        out_shape=jax.ShapeDtypeStruct((db, SAMPLE_ROWS, qw), BF16),
        compiler_params=pltpu.CompilerParams(dimension_semantics=("parallel",)),
        name="moba_sample",
    )(page_table, *([pages] * n_pages), zcs, zcs, zcs, k_gain.reshape(1, HEAD_DIM), q_gain.reshape(1, HEAD_DIM))


ROW_TILE = 512
GLA_TILE = 128
NSA_Q_TILE = 128


def _ab_weight_layout(w_in_ab):
    d = w_in_ab.shape[0]
    widths = (GLA_HEADS * GLA_DK, GLA_HEADS * GLA_DK, GLA_HEADS * GLA_DV, GLA_HEADS * GLA_DV, GLA_GATE_RANK,
              NSA_HEADS * HEAD_DIM, NSA_HEADS * 3, 6 * NSA_KV * HEAD_DIM)
    offs = [0]
    for w in widths:
        offs.append(offs[-1] + w)
    gq, gk, gv, gr, ga, nq, ngt, nkv = (w_in_ab[:, offs[i]:offs[i + 1]] for i in range(len(widths)))
    half = NSA_REP * 3
    zeros = lambda n: jnp.zeros((d, n), w_in_ab.dtype)
    misc0 = jnp.concatenate([ga, ngt[:, :half], zeros(LANES - GLA_GATE_RANK - half)], axis=1)
    misc1 = jnp.concatenate([zeros(GLA_GATE_RANK), ngt[:, half:], zeros(LANES - GLA_GATE_RANK - half)], axis=1)
    w = jnp.concatenate([gq, gk, gv, gr, nq, nkv, misc0, misc1, zeros(AB_TOTAL - AB_MISC1 - LANES)], axis=1)
    return w.astype(BF16)


def _pad_sample_rows(z, db, t_new):
    return jnp.pad(z.reshape(db, t_new, z.shape[-1]), ((0, 0), (0, SAMPLE_ROWS - t_new), (0, 0)))


def _matmul_rows(n):
    return n // 8 if n % (8 * 2 * SUBLANES) == 0 else ROW_TILE


def _conv_ffn(h, l, n_p, batch, seq, t_new, state_ffn_conv, norm_ffn, w_up_bf16, ffn_conv_w, ffn_conv_b, w_down_bf16):
    d_ff = ffn_conv_w.shape[-1]
    db = state_ffn_conv.shape[1]
    act, tails, gate_s = ffn_up_act(h, norm_ffn[l], w_up_bf16, ffn_conv_w[l], ffn_conv_b[l], state_ffn_conv[l],
                                    batch=batch, seq=seq, t_new=t_new, tm=ROW_TILE, tn=d_ff // 4, layer=l)
    h = matmul_residual(act, w_down_bf16, h, tm=_matmul_rows(h.shape[0]), tn=min(1024, h.shape[1]), tk=d_ff // 2, layer=l)
    keep = FFN_CONV - 1
    tps = seq // ROW_TILE
    gate_p = jnp.stack([tails[(b + 1) * tps * SUBLANES - keep:(b + 1) * tps * SUBLANES] for b in range(batch)])
    gate_s = gate_s.reshape(db, t_new, d_ff)[:, t_new - keep:]
    return h, gate_p, gate_s


def kernel(x_prompt, x_sample, page_table, cache_nsa_kv, cache_nsa_win, state_gla, cache_moba_kv, state_ffn_conv, norm_mix, w_in_ab, gla_a_w2, gla_a_b, gla_o_norm, nsa_q_norm, nsa_k_norm, nsa_cmp_w1, nsa_cmp_b1, nsa_cmp_w2, nsa_cmp_b2, nsa_cmp_pe, w_out_ab, w_in_c, moba_q_norm, moba_k_norm, w_out_c, norm_ffn, ffn_w_up, ffn_conv_w, ffn_conv_b, ffn_w_down):
    batch, seq, d_model = x_prompt.shape
    db, t_new, _ = x_sample.shape
    n_p, n_s = batch * seq, db * t_new
    n = n_p + n_s
    assert norm_mix.shape[0] == 2 and w_in_ab.shape[0] == 1 and w_in_c.shape[0] == 1
    assert FFN_CONV - 1 <= t_new <= SAMPLE_ROWS and n % ROW_TILE == 0 and n_p % ROW_TILE == 0
    ffn_args = (state_ffn_conv, norm_ffn, ffn_w_up.astype(BF16), ffn_conv_w, ffn_conv_b, ffn_w_down.astype(BF16))
    tm_mm = _matmul_rows(n)

    h = jnp.concatenate([x_prompt.reshape(n_p, d_model), x_sample.reshape(n_s, d_model)], axis=0)
    nsa_cache = cache_nsa_kv.reshape(cache_nsa_kv.shape[1:])
    nsa_win = cache_nsa_win.reshape(cache_nsa_win.shape[1:])
    gla_state = state_gla.reshape(state_gla.shape[1:])
    moba_cache = cache_moba_kv.reshape(cache_moba_kv.shape[1:])
    tn_out = min(1024, d_model)

    z = norm_matmul(h, norm_mix[0], _ab_weight_layout(w_in_ab[0]), tm=tm_mm, tn=1024)
    zs = _pad_sample_rows(z[n_p:], db, t_new)
    w2p = jnp.pad(gla_a_w2[0], ((0, LANES - GLA_GATE_RANK), (0, 0))).astype(BF16)
    gla_zero = jnp.zeros((batch,) + state_gla.shape[2:], F32)
    zg = z if n % GLA_TILE == 0 else z[:n_p]
    og_p, gla_p = gla_mixer(zg.reshape(-1, GLA_TILE, AB_TOTAL), gla_zero, w2p, gla_a_b[0], gla_o_norm[0],
                            n_seq=batch, tiles_per_seq=seq // GLA_TILE, chunk=GLA_CHUNK, n_valid=GLA_CHUNK, out_dtype=BF16)
    og_s, gla_s = gla_mixer(zs, gla_state, w2p, gla_a_b[0], gla_o_norm[0],
                            n_seq=db, tiles_per_seq=1, chunk=SAMPLE_ROWS, n_valid=t_new, out_dtype=BF16, native_state=True)
    cmpw = nsa_cmp_weights(nsa_cmp_w1[0], nsa_cmp_b1[0], nsa_cmp_w2[0], nsa_cmp_b2[0], nsa_cmp_pe[0], nsa_k_norm[0])
    kc, vc = nsa_cmp_prompt(z, cmpw, batch=batch, seq=seq)
    kvp = nsa_prep_prompt(z, nsa_k_norm[0], rows=n_p, tm=ROW_TILE)
    on_p = nsa_attn_prompt(z, kvp, kc, vc, nsa_q_norm[0], batch=batch, seq=seq, tq=NSA_Q_TILE)
    on_s = nsa_sample(zs, page_table, nsa_cache, nsa_win, cmpw, nsa_k_norm[0], nsa_q_norm[0], t_new=t_new)
    mix_p = jnp.concatenate([og_p.reshape(n_p, -1), on_p], axis=1)
    mix_s = jnp.concatenate([og_s[:, :t_new].reshape(n_s, -1), on_s[:, :t_new].reshape(n_s, -1)], axis=1)
    h = matmul_residual(jnp.concatenate([mix_p, mix_s], axis=0), w_out_ab.astype(BF16), h,
                        tm=tm_mm, tn=tn_out, tk=mix_p.shape[1])
    h, conv_p0, conv_s0 = _conv_ffn(h, 0, n_p, batch, seq, t_new, *ffn_args)

    kv_w = 4 * NSA_KV * HEAD_DIM
    win_w = 2 * NSA_KV * HEAD_DIM
    win_keep = min(NSA_WINDOW, seq)
    nsa_kv_p = z[:n_p, AB_CMP:AB_CMP + kv_w].reshape(1, batch, seq, 4, NSA_KV, HEAD_DIM)
    nsa_kv_s = z[n_p:, AB_CMP:AB_CMP + kv_w].reshape(1, db, t_new, 4, NSA_KV, HEAD_DIM)
    nsa_win_p = z[:n_p, AB_WIN:AB_WIN + win_w].reshape(batch, seq, win_w)[:, seq - win_keep:]
    nsa_win_p = nsa_win_p.reshape(1, batch, win_keep, 2, NSA_KV, HEAD_DIM)
    win_new = z[n_p:, AB_WIN:AB_WIN + win_w].reshape(db, t_new, 2, NSA_KV, HEAD_DIM)
    nsa_win_s = jnp.concatenate([nsa_win, win_new], axis=1)[None, :, -NSA_WINDOW:]

    zc = norm_matmul(h, norm_mix[1], w_in_c[0].astype(BF16), tm=tm_mm, tn=1024)
    zcs = _pad_sample_rows(zc[n_p:], db, t_new)
    kvm, kmean = moba_prep_prompt(zc, moba_k_norm[0], rows=n_p)
    om_p = moba_attn_prompt(zc, kvm, kmean, moba_q_norm[0], batch=batch, seq=seq)
    om_s = moba_sample(zcs, page_table, moba_cache, moba_k_norm[0], moba_q_norm[0], t_new=t_new)
    om = jnp.concatenate([om_p, om_s[:, :t_new].reshape(n_s, -1)], axis=0)
    h = matmul_residual(om, w_out_c.astype(BF16), h, tm=tm_mm, tn=tn_out, tk=om.shape[1])
    h, conv_p1, conv_s1 = _conv_ffn(h, 1, n_p, batch, seq, t_new, *ffn_args)

    moba_kv_p = zc[:n_p, C_K:].reshape(1, batch, seq, 2, MOBA_KV, HEAD_DIM)
    moba_kv_s = zc[n_p:, C_K:].reshape(1, db, t_new, 2, MOBA_KV, HEAD_DIM)

    return (h[:n_p].reshape(batch, seq, d_model), h[n_p:].reshape(db, t_new, d_model),
            nsa_kv_p, nsa_kv_s, nsa_win_p, nsa_win_s, gla_p[None], gla_s[None], moba_kv_p, moba_kv_s,
            jnp.stack([conv_p0, conv_p1]), jnp.stack([conv_s0, conv_s1]))
```
